```python
import math
import jax
import jax.numpy as jnp
from jax import lax
import numpy as np

D_MODEL = 2048
BATCH = 8
SEQ = 2048
DEPTH = 1
DEC_BATCH = 4
DEC_SEQ = 4096
PAST_LEN = 128

N_META = 16
S5_WIDTH = D_MODEL // 2
S5_GROUP = 16
S5_GROUPS = S5_WIDTH // S5_GROUP
S5_STATE = 64
RW_WIDTH = D_MODEL // 2
RW_HEAD = 64
RW_HEADS = RW_WIDTH // RW_HEAD
RW_DECAY_LORA = 64
RW_ICLR_LORA = 64
RW_GATE_LORA = 160
RW_IN = 3 * RW_WIDTH + 2 * RW_DECAY_LORA + 2 * RW_ICLR_LORA + RW_GATE_LORA
N_BRANCH = 2
C_IN = S5_WIDTH + RW_IN + N_BRANCH * D_MODEL
MOE_GROUPS = 4
EXPERTS_PER_GROUP = 8
N_EXPERTS = MOE_GROUPS * EXPERTS_PER_GROUP
TOP_K = 2
D_EXPERT = 512
MOE_BLOCK = 128
ALPHA = (2 * DEPTH) ** 0.25
BETA = (8 * DEPTH) ** -0.25
LN_EPS = 1e-5
GN_EPS = 64e-5

kernel_name = 'hybrid_s5_rwkv7_hmoe_encoder'


def _layernorm(x, g, b):
    xf = x.astype(jnp.float32)
    mu = jnp.mean(xf, axis=-1, keepdims=True)
    var = jnp.mean(jnp.square(xf - mu), axis=-1, keepdims=True)
    return ((xf - mu) * lax.rsqrt(var + LN_EPS) * g + b).astype(x.dtype)


def _complex_affine_combine(e1, e2):
    a1r, a1i, b1r, b1i = e1
    a2r, a2i, b2r, b2i = e2
    return (a2r * a1r - a2i * a1i,
            a2r * a1i + a2i * a1r,
            a2r * b1r - a2i * b1i + b2r,
            a2r * b1i + a2i * b1r + b2i)


def _s5_branch(u, b_re, b_im, a_re, a_im, log_dt, c_re, c_im, d_skip, glu_w, glu_b):
    bsz, t, _ = u.shape
    uf = u.astype(jnp.float32)
    ug = uf.reshape(bsz, t, S5_GROUPS, S5_GROUP)
    y = uf * d_skip
    for z in range(2):
        ar, ai = a_re[z], a_im[z]
        dt = jnp.exp(log_dt[z])[:, None]
        mag = jnp.exp(ar * dt)
        abr = mag * jnp.cos(ai * dt)
        abi = mag * jnp.sin(ai * dt)
        den = ar * ar + ai * ai
        nr = abr - 1.0
        cr = (nr * ar + abi * ai) / den
        ci = (abi * ar - nr * ai) / den
        bbr = cr[..., None] * b_re - ci[..., None] * b_im
        bbi = cr[..., None] * b_im + ci[..., None] * b_re
        bur = jnp.einsum('btgc,gpc->btgp', ug, bbr)
        bui = jnp.einsum('btgc,gpc->btgp', ug, bbi)
        if z == 1:
            bur = jnp.flip(bur, 1)
            bui = jnp.flip(bui, 1)
        elems = (jnp.broadcast_to(abr, (1, t) + abr.shape),
                 jnp.broadcast_to(abi, (1, t) + abi.shape), bur, bui)
        _, _, xr, xi = lax.associative_scan(_complex_affine_combine, elems, axis=1)
        if z == 1:
            xr = jnp.flip(xr, 1)
            xi = jnp.flip(xi, 1)
        yz = jnp.einsum('btgp,gcp->btgc', xr, c_re[z]) - jnp.einsum('btgp,gcp->btgc', xi, c_im[z])
        y = y + yz.reshape(bsz, t, S5_WIDTH)
    act = jax.nn.gelu(y)
    return act * jax.nn.sigmoid(act @ glu_w + glu_b)


def _centred_shift(p, mu):
    prev = jnp.pad(p[:, :-1], ((0, 0), (1, 0), (0, 0)))
    nxt = jnp.pad(p[:, 1:], ((0, 0), (0, 1), (0, 0)))
    return p + mu[0] * (prev - p) + mu[1] * (nxt - p)


def _rwkv7_step(state, inp):
    r, w, k, v, kk, a = inp
    sk = jnp.einsum('zhvk,zhk->zhv', state, kk)
    state = (state * w[:, :, None, :] - sk[..., None] * (kk * a)[:, :, None, :]
             + v[..., None] * k[:, :, None, :])
    return state, jnp.einsum('zhvk,zhk->zhv', state, r)


def _rwkv7_branch(cols, mu, w0, w2, a0, a2, g2, k_k, k_a, r_k, lnx_g, lnx_b):
    f32 = jnp.float32
    bsz, t, _ = cols.shape
    xs = _centred_shift(cols.astype(f32), mu)
    cuts = [RW_WIDTH, 2 * RW_WIDTH, 3 * RW_WIDTH, 3 * RW_WIDTH + 2 * RW_DECAY_LORA,
            3 * RW_WIDTH + 2 * RW_DECAY_LORA + 2 * RW_ICLR_LORA]
    r, k, v, lw, la, lg = jnp.split(xs, cuts, axis=-1)
    lw = lw.reshape(bsz, t, 2, RW_DECAY_LORA)
    la = la.reshape(bsz, t, 2, RW_ICLR_LORA)
    w_log = w0 + jnp.einsum('btzl,zlc->btzc', jnp.tanh(lw), w2)
    decay = jnp.exp(-jnp.exp(-jax.nn.softplus(-w_log) - 0.5))
    a = jax.nn.sigmoid(a0 + jnp.einsum('btzl,zlc->btzc', la, a2))
    g = jax.nn.sigmoid(lg) @ g2
    kk = (k * k_k).reshape(bsz, t, RW_HEADS, RW_HEAD)
    kk = kk / jnp.maximum(jnp.linalg.norm(kk, axis=-1, keepdims=True), 1e-12)
    kk = kk.reshape(bsz, t, RW_WIDTH)
    k_dir = k[:, :, None, :] * (1.0 + (a - 1.0) * k_a)
    bonus = jnp.sum((r[:, :, None, :] * k_dir * r_k.reshape(-1)).reshape(bsz, t, 2, RW_HEADS, RW_HEAD),
                    axis=(2, 4))
    bonus = bonus[..., None] * v.reshape(bsz, t, RW_HEADS, RW_HEAD)

    def shared(z):
        return jnp.stack([z, jnp.flip(z, 1)], 0)

    def per_dir(z):
        return jnp.stack([z[:, :, 0], jnp.flip(z[:, :, 1], 1)], 0)

    def to_scan(z):
        return jnp.moveaxis(z.reshape(2 * bsz, t, RW_HEADS, RW_HEAD), 1, 0)

    seqs = (to_scan(shared(r)), to_scan(per_dir(decay)), to_scan(per_dir(k_dir)),
            to_scan(shared(v)), to_scan(shared(kk)), to_scan(per_dir(a)))
    s0 = jnp.zeros((2 * bsz, RW_HEADS, RW_HEAD, RW_HEAD), f32)
    _, ys = lax.scan(_rwkv7_step, s0, seqs)
    ys = jnp.moveaxis(ys, 0, 1).reshape(2, bsz, t, RW_HEADS, RW_HEAD)
    y = ys[0] + jnp.flip(ys[1], 1)
    mean = jnp.mean(y, axis=-1, keepdims=True)
    var = jnp.mean(jnp.square(y - mean), axis=-1, keepdims=True)
    y = ((y - mean) * lax.rsqrt(var + GN_EPS)).reshape(bsz, t, RW_WIDTH) * lnx_g + lnx_b
    return (y + bonus.reshape(bsz, t, RW_WIDTH)) * g


def _hier_moe(h, rc, rcb, rf, rfb, wg, wu, wd):
    f32 = jnp.float32
    bsz, t, d = h.shape
    n_tok = bsz * t
    xf = h.reshape(n_tok, d)
    lc = (xf @ rc + rcb).astype(f32)
    grp = jnp.argmax(lc, axis=-1)
    gate_c = jnp.take_along_axis(jax.nn.softmax(lc, axis=-1), grp[:, None], axis=1)[:, 0]
    lf = (xf @ rf + rfb).astype(f32).reshape(n_tok, MOE_GROUPS, EXPERTS_PER_GROUP)
    lf = jnp.take_along_axis(lf, grp[:, None, None], axis=1)[:, 0]
    top_v, top_i = lax.top_k(lf, TOP_K)
    wts = gate_c[:, None] * jax.nn.softmax(top_v, axis=-1)
    expert = (grp[:, None] * EXPERTS_PER_GROUP + top_i).astype(jnp.int32)
    n_asg = n_tok * TOP_K
    e_flat = expert.reshape(-1)
    tok_flat = jnp.repeat(jnp.arange(n_tok, dtype=jnp.int32), TOP_K)
    w_flat = wts.reshape(-1)
    order = jnp.argsort(e_flat)
    e_s, tok_s, w_s = e_flat[order], tok_flat[order], w_flat[order]
    counts = jnp.zeros((N_EXPERTS,), jnp.int32).at[e_flat].add(1)
    padded = (counts + MOE_BLOCK - 1) // MOE_BLOCK * MOE_BLOCK
    start = jnp.cumsum(counts) - counts
    pend = jnp.cumsum(padded)
    pstart = pend - padded
    dest = pstart[e_s] + (jnp.arange(n_asg, dtype=jnp.int32) - start[e_s])
    n_blocks = -(-n_asg // MOE_BLOCK) + N_EXPERTS
    buf_tok = jnp.full((n_blocks * MOE_BLOCK,), n_tok, jnp.int32).at[dest].set(tok_s)
    blk_exp = jnp.minimum(jnp.searchsorted(pend, jnp.arange(n_blocks, dtype=jnp.int32) * MOE_BLOCK,
                                           side='right'), N_EXPERTS - 1)
    x_pad = jnp.concatenate([xf, jnp.zeros((1, d), xf.dtype)], axis=0)

    def expert_block(args):
        idx, e = args
        xb = x_pad[idx]
        hb = jax.nn.silu(xb @ wg[e]) * (xb @ wu[e])
        return hb @ wd[e]

    out = lax.map(expert_block, (buf_tok.reshape(n_blocks, MOE_BLOCK), blk_exp)).reshape(-1, d)
    y = jnp.zeros((n_tok, d), f32).at[tok_s].add(w_s[:, None] * out[dest].astype(f32))
    return y.astype(h.dtype).reshape(bsz, t, d)


def _layer(h, p, l):
    bsz, t, _ = h.shape
    proj = h @ p['w_in'][l]
    u_s5 = proj[..., :S5_WIDTH]
    rw_cols = proj[..., S5_WIDTH:S5_WIDTH + RW_IN]
    gate_logits = proj[..., S5_WIDTH + RW_IN:].reshape(bsz, t, N_BRANCH, D_MODEL) + p['gate_b'][l]
    gates = jax.nn.sigmoid(gate_logits.astype(jnp.float32)).astype(h.dtype)
    s5_out = _s5_branch(u_s5, p['s5_B_re'][l], p['s5_B_im'][l], p['s5_A_re'][l], p['s5_A_im'][l],
                        p['s5_log_dt'][l], p['s5_C_re'][l], p['s5_C_im'][l], p['s5_D'][l],
                        p['s5_glu_w'][l], p['s5_glu_b'][l]).astype(h.dtype)
    rw_out = _rwkv7_branch(rw_cols, p['shift_mu'][l], p['rw_w0'][l], p['rw_w2'][l], p['rw_a0'][l],
                           p['rw_a2'][l], p['rw_g2'][l], p['rw_k_k'][l], p['rw_k_a'][l], p['rw_r_k'][l],
                           p['rw_lnx_g'][l], p['rw_lnx_b'][l]).astype(h.dtype)
    merged = (gates[:, :, 0] * (s5_out @ p['proj_s5'][l])
              + gates[:, :, 1] * (rw_out @ p['proj_rwkv'][l]))
    h = _layernorm(ALPHA * h + merged @ p['w_out'][l], p['ln1_g'][l], p['ln1_b'][l])
    moe = _hier_moe(h, p['router_coarse'][l], p['router_coarse_b'][l], p['router_fine'][l],
                    p['router_fine_b'][l], p['exp_w_gate'][l], p['exp_w_up'][l], p['exp_w_down'][l])
    return _layernorm(ALPHA * h + moe, p['ln2_g'][l], p['ln2_b'][l])


def _encoder(x, p):
    bsz = x.shape[0]
    meta = jnp.broadcast_to(p['meta'][None].astype(x.dtype), (bsz, N_META, D_MODEL))
    h = jnp.concatenate([meta, x], axis=1)
    h = _layernorm(h, p['ln_in_g'], p['ln_in_b'])
    for l in range(DEPTH):
        h = _layer(h, p, l)
    return h[:, N_META:]


def setup_inputs(seed: int = 0) -> dict:
    key = jax.random.key(seed)
    keys = iter(list(jax.random.split(key, 64)))
    f32 = jnp.float32
    L = DEPTH

    def nrm(shape, scale):
        return scale * jax.random.normal(next(keys), shape, f32)

    def uni(shape, lo, hi):
        return jax.random.uniform(next(keys), shape, f32, lo, hi)

    ratio = jnp.arange(RW_WIDTH, dtype=f32) / (RW_WIDTH - 1)
    w0_base = -7.0 + 5.0 * ratio ** 0.85 + 0.5
    a_im_base = jnp.pi * jnp.arange(S5_STATE, dtype=f32)
    return {
        'x_prompt': nrm((BATCH, SEQ, D_MODEL), 1.0),
        'x_sample': nrm((DEC_BATCH, DEC_SEQ, D_MODEL), 1.0),
        'meta': nrm((N_META, D_MODEL), 1.0),
        'ln_in_g': 1.0 + nrm((D_MODEL,), 0.02),
        'ln_in_b': nrm((D_MODEL,), 0.02),
        'w_in': nrm((L, D_MODEL, C_IN), D_MODEL ** -0.5),
        'shift_mu': uni((L, 2, RW_IN), 0.0, 0.5),
        's5_B_re': nrm((L, S5_GROUPS, S5_STATE, S5_GROUP), (2 * S5_GROUP) ** -0.5),
        's5_B_im': nrm((L, S5_GROUPS, S5_STATE, S5_GROUP), (2 * S5_GROUP) ** -0.5),
        's5_A_re': -0.5 + nrm((L, 2, S5_GROUPS, S5_STATE), 0.01),
        's5_A_im': a_im_base + nrm((L, 2, S5_GROUPS, S5_STATE), 0.01),
        's5_log_dt': uni((L, 2, S5_GROUPS), math.log(1e-3), math.log(1e-1)),
        's5_C_re': nrm((L, 2, S5_GROUPS, S5_GROUP, S5_STATE), S5_STATE ** -0.5),
        's5_C_im': nrm((L, 2, S5_GROUPS, S5_GROUP, S5_STATE), S5_STATE ** -0.5),
        's5_D': nrm((L, S5_WIDTH), 1.0),
        's5_glu_w': nrm((L, S5_WIDTH, S5_WIDTH), S5_WIDTH ** -0.5),
        's5_glu_b': nrm((L, S5_WIDTH), 0.02),
        'rw_w0': w0_base + nrm((L, 2, RW_WIDTH), 0.1),
        'rw_w2': nrm((L, 2, RW_DECAY_LORA, RW_WIDTH), 0.1 * RW_DECAY_LORA ** -0.5),
        'rw_a0': nrm((L, 2, RW_WIDTH), 0.1),
        'rw_a2': nrm((L, 2, RW_ICLR_LORA, RW_WIDTH), 0.1 * RW_ICLR_LORA ** -0.5),
        'rw_g2': nrm((L, RW_GATE_LORA, RW_WIDTH), RW_GATE_LORA ** -0.5),
        'rw_k_k': 0.85 + nrm((L, RW_WIDTH), 0.02),
        'rw_k_a': 1.0 + nrm((L, RW_WIDTH), 0.02),
        'rw_r_k': nrm((L, RW_HEADS, RW_HEAD), 0.1),
        'rw_lnx_g': 1.0 + nrm((L, RW_WIDTH), 0.02),
        'rw_lnx_b': nrm((L, RW_WIDTH), 0.02),
        'proj_s5': nrm((L, S5_WIDTH, D_MODEL), S5_WIDTH ** -0.5),
        'proj_rwkv': nrm((L, RW_WIDTH, D_MODEL), RW_WIDTH ** -0.5),
        'gate_b': nrm((L, N_BRANCH, D_MODEL), 0.02),
        'w_out': nrm((L, D_MODEL, D_MODEL), BETA * D_MODEL ** -0.5),
        'ln1_g': 1.0 + nrm((L, D_MODEL), 0.02),
        'ln1_b': nrm((L, D_MODEL), 0.02),
        'router_coarse': nrm((L, D_MODEL, MOE_GROUPS), D_MODEL ** -0.5),
        'router_coarse_b': nrm((L, MOE_GROUPS), 0.01),
        'router_fine': nrm((L, D_MODEL, N_EXPERTS), D_MODEL ** -0.5),
        'router_fine_b': nrm((L, N_EXPERTS), 0.01),
        'exp_w_gate': nrm((L, N_EXPERTS, D_MODEL, D_EXPERT), D_MODEL ** -0.5),
        'exp_w_up': nrm((L, N_EXPERTS, D_MODEL, D_EXPERT), D_MODEL ** -0.5),
        'exp_w_down': nrm((L, N_EXPERTS, D_EXPERT, D_MODEL), BETA * D_EXPERT ** -0.5),
        'ln2_g': 1.0 + nrm((L, D_MODEL), 0.02),
        'ln2_b': nrm((L, D_MODEL), 0.02),
    }


def reference(x_prompt, x_sample, meta, ln_in_g, ln_in_b, w_in, shift_mu, s5_B_re, s5_B_im,
              s5_A_re, s5_A_im, s5_log_dt, s5_C_re, s5_C_im, s5_D, s5_glu_w, s5_glu_b,
              rw_w0, rw_w2, rw_a0, rw_a2, rw_g2, rw_k_k, rw_k_a, rw_r_k, rw_lnx_g, rw_lnx_b,
              proj_s5, proj_rwkv, gate_b, w_out, ln1_g, ln1_b, router_coarse, router_coarse_b,
              router_fine, router_fine_b, exp_w_gate, exp_w_up, exp_w_down, ln2_g, ln2_b):
    p = {
        'meta': meta, 'ln_in_g': ln_in_g, 'ln_in_b': ln_in_b, 'w_in': w_in, 'shift_mu': shift_mu,
        's5_B_re': s5_B_re, 's5_B_im': s5_B_im, 's5_A_re': s5_A_re, 's5_A_im': s5_A_im,
        's5_log_dt': s5_log_dt, 's5_C_re': s5_C_re, 's5_C_im': s5_C_im, 's5_D': s5_D,
        's5_glu_w': s5_glu_w, 's5_glu_b': s5_glu_b,
        'rw_w0': rw_w0, 'rw_w2': rw_w2, 'rw_a0': rw_a0, 'rw_a2': rw_a2, 'rw_g2': rw_g2,
        'rw_k_k': rw_k_k, 'rw_k_a': rw_k_a, 'rw_r_k': rw_r_k, 'rw_lnx_g': rw_lnx_g, 'rw_lnx_b': rw_lnx_b,
        'proj_s5': proj_s5, 'proj_rwkv': proj_rwkv, 'gate_b': gate_b, 'w_out': w_out,
        'ln1_g': ln1_g, 'ln1_b': ln1_b,
        'router_coarse': router_coarse, 'router_coarse_b': router_coarse_b,
        'router_fine': router_fine, 'router_fine_b': router_fine_b,
        'exp_w_gate': exp_w_gate, 'exp_w_up': exp_w_up, 'exp_w_down': exp_w_down,
        'ln2_g': ln2_g, 'ln2_b': ln2_b,
    }
    y_prompt = _encoder(x_prompt, p)
    y_sample = _encoder(x_sample, p)
    return (y_prompt, y_sample)
```

```python
import functools
import math

import jax
import jax.numpy as jnp
from jax import lax
from jax.experimental import pallas as pl
from jax.experimental.pallas import tpu as pltpu

F32 = jnp.float32
BF16 = jnp.bfloat16

D_MODEL = 2048
N_META = 16
S5_WIDTH = 1024
S5_GROUP = 16
S5_GROUPS = 64
S5_STATE = 64
S5_CHUNK = 16
RW_WIDTH = 1024
RW_HEAD = 64
RW_HEADS = 16
RW_DECAY_LORA = 64
RW_ICLR_LORA = 64
RW_GATE_LORA = 160
RW_GATE_PAD = 256
RW_COLS = 3 * RW_WIDTH + 2 * RW_DECAY_LORA + 2 * RW_ICLR_LORA + RW_GATE_PAD
RW_CHUNK = 64
RW_HEADS_PER_STEP = 4
MOE_GROUPS = 4
EXPERTS_PER_GROUP = 8
N_EXPERTS = 32
D_EXPERT = 512
MOE_ROWS = 256
ROUTER_PAD = 128
DEPTH = 1
ALPHA = (2 * DEPTH) ** 0.25
LN_EPS = 1e-5
GN_EPS = 64e-5
SEQ_ALIGN = 64
VMEM_LIMIT = 56 * 1024 * 1024


def _cparams(*sem):
    return pltpu.CompilerParams(dimension_semantics=sem, vmem_limit_bytes=VMEM_LIMIT)


def _row_block(t_pad, cap, mult=8):
    best = mult
    for d in range(mult, cap + 1, mult):
        if t_pad % d == 0:
            best = d
    return best


def _dot(a, b):
    return jnp.dot(a, b, preferred_element_type=F32)


def _dot_nt(a, b):
    return lax.dot_general(a, b, (((1,), (1,)), ((), ())), preferred_element_type=F32)


def _dot_tn(a, b):
    return lax.dot_general(a, b, (((0,), (0,)), ((), ())), preferred_element_type=F32)


def _split(x):
    hi = x.astype(BF16)
    lo = (x - hi.astype(F32)).astype(BF16)
    return hi, lo


def _layernorm(x, g, b):
    mu = jnp.mean(x, axis=-1, keepdims=True)
    xc = x - mu
    var = jnp.mean(xc * xc, axis=-1, keepdims=True)
    return xc * lax.rsqrt(var + LN_EPS) * g + b


def _ln_in_kernel(x_ref, g_ref, b_ref, of_ref, ob_ref, *, t_valid, bm):
    y = _layernorm(x_ref[0], g_ref[...], b_ref[...])
    pos = pl.program_id(1) * bm + lax.broadcasted_iota(jnp.int32, (bm, 1), 0)
    y = jnp.where(pos < t_valid, y, 0.0)
    of_ref[0] = y
    ob_ref[0] = y.astype(BF16)


def _ln_in(x3, g, b, t_valid):
    bsz, t_pad, d = x3.shape
    bm = _row_block(t_pad, 1024)
    row = pl.BlockSpec((1, bm, d), lambda i, j: (i, j, 0))
    vec = pl.BlockSpec((1, d), lambda i, j: (0, 0))
    return pl.pallas_call(
        functools.partial(_ln_in_kernel, t_valid=t_valid, bm=bm),
        grid=(bsz, t_pad // bm),
        in_specs=[row, vec, vec],
        out_specs=[row, row],
        out_shape=[jax.ShapeDtypeStruct(x3.shape, F32), jax.ShapeDtypeStruct(x3.shape, BF16)],
        compiler_params=_cparams("parallel", "parallel"),
        name="ln_in",
    )(x3, g.reshape(1, d), b.reshape(1, d))


def _mm_kernel(x_ref, w_ref, o_ref):
    o_ref[...] = _dot(x_ref[...], w_ref[...]).astype(o_ref.dtype)


def _mm(x, w, bm, bn, out_dtype, name):
    n, k = x.shape
    m = w.shape[1]
    return pl.pallas_call(
        _mm_kernel,
        grid=(m // bn, n // bm),
        in_specs=[pl.BlockSpec((bm, k), lambda j, i: (i, 0)),
                  pl.BlockSpec((k, bn), lambda j, i: (0, j))],
        out_specs=pl.BlockSpec((bm, bn), lambda j, i: (i, j)),
        out_shape=jax.ShapeDtypeStruct((n, m), out_dtype),
        compiler_params=_cparams("parallel", "parallel"),
        name=name,
    )(x, w)


def _s5_matrices(b_re, b_im, a_re, a_im, log_dt, c_re, c_im):
    L = S5_CHUNK
    dt = jnp.exp(log_dt)[..., None]
    mag = jnp.exp(a_re * dt)
    abr = mag * jnp.cos(a_im * dt)
    abi = mag * jnp.sin(a_im * dt)
    den = a_re * a_re + a_im * a_im
    nr = abr - 1.0
    cr = (nr * a_re + abi * a_im) / den
    ci = (abi * a_re - nr * a_im) / den
    bbr = cr[..., None] * b_re - ci[..., None] * b_im
    bbi = cr[..., None] * b_im + ci[..., None] * b_re
    tau = jnp.arange(L + 1, dtype=F32)[:, None, None, None]
    pmag = jnp.exp(tau * a_re * dt)
    pr = pmag * jnp.cos(tau * a_im * dt)
    pi = pmag * jnp.sin(tau * a_im * dt)
    wr = pr[..., None] * bbr - pi[..., None] * bbi
    wi = pr[..., None] * bbi + pi[..., None] * bbr
    kern = (jnp.einsum('zgop,tzgpi->tzgoi', c_re, wr)
            - jnp.einsum('zgop,tzgpi->tzgoi', c_im, wi))
    s = jnp.arange(L)[:, None]
    t = jnp.arange(L)[None, :]
    lag = t - s
    kf = jnp.where((lag >= 0)[..., None, None, None], kern[jnp.clip(lag, 0, L), 0], 0.0)
    kb = jnp.where((lag <= 0)[..., None, None, None], kern[jnp.clip(-lag, 0, L), 1], 0.0)
    toep = (kf + kb).transpose(2, 0, 4, 1, 3).reshape(S5_GROUPS, L * S5_GROUP, L * S5_GROUP)
    wf_r, wf_i = wr[::-1][1:, 0], wi[::-1][1:, 0]
    wb_r, wb_i = wr[:L, 1], wi[:L, 1]
    bmat = jnp.concatenate([wf_r, wb_r, wf_i, wb_i], axis=2)
    bmat = bmat.transpose(1, 0, 3, 2).reshape(S5_GROUPS, L * S5_GROUP, 4 * S5_STATE)
    pf_r, pf_i = pr[1:, 0], pi[1:, 0]
    pb_r, pb_i = pr[::-1][:L, 1], pi[::-1][:L, 1]
    c0r, c0i, c1r, c1i = c_re[0], c_im[0], c_re[1], c_im[1]

    def cpow(cre, cim, p_r, p_i):
        re = cre[None] * p_r[:, :, None, :] - cim[None] * p_i[:, :, None, :]
        im = cre[None] * p_i[:, :, None, :] + cim[None] * p_r[:, :, None, :]
        return re, -im

    f_re, f_im = cpow(c0r, c0i, pf_r, pf_i)
    g_re, g_im = cpow(c1r, c1i, pb_r, pb_i)
    cmat = jnp.concatenate([f_re, g_re, f_im, g_im], axis=3)
    cmat = cmat.transpose(1, 3, 0, 2).reshape(S5_GROUPS, 4 * S5_STATE, L * S5_GROUP)
    lam_re = jnp.concatenate([pr[L, 0], pr[L, 1]], axis=-1)[:, None, :]
    lam_im = jnp.concatenate([pi[L, 0], pi[L, 1]], axis=-1)[:, None, :]
    return bmat.astype(BF16), toep.astype(BF16), cmat.astype(BF16), lam_re, lam_im


def _s5_kernel(u_ref, bmat_ref, toep_ref, cmat_ref, lre_ref, lim_ref, y_ref,
               s_scr, xf_scr, xb_scr, *, n_chunks, bp):
    u = u_ref[0].astype(BF16)
    s_scr[...] = _dot(u, bmat_ref[0])
    a_re = lre_ref[0]
    a_im = lim_ref[0]
    is_fwd = lax.broadcasted_iota(jnp.int32, (bp, 128), 1) < S5_STATE

    def step(j, carry):
        xr, xi = carry
        rf = pl.multiple_of(j * bp, bp)
        rb = pl.multiple_of((n_chunks - 1 - j) * bp, bp)
        x_in = jnp.concatenate([xr, xi], axis=1)
        xf_scr[pl.ds(rf, bp), :] = x_in
        xb_scr[pl.ds(rb, bp), :] = x_in
        sf = s_scr[pl.ds(rf, bp), :]
        sb = s_scr[pl.ds(rb, bp), :]
        s_re = jnp.where(is_fwd, sf[:, :128], sb[:, :128])
        s_im = jnp.where(is_fwd, sf[:, 128:], sb[:, 128:])
        return (a_re * xr - a_im * xi + s_re, a_re * xi + a_im * xr + s_im)

    zero = jnp.zeros((bp, 128), F32)
    lax.fori_loop(0, n_chunks, step, (zero, zero))
    lane = lax.broadcasted_iota(jnp.int32, (1, 256), 1)
    fwd_lane = (lane % 128) < S5_STATE
    x_in = jnp.where(fwd_lane, xf_scr[...], xb_scr[...]).astype(BF16)
    y_ref[0] = _dot(u, toep_ref[0]) + _dot(x_in, cmat_ref[0])


def _s5_ssm(u, mats, bsz, t_pad):
    bmat, toep, cmat, lam_re, lam_im = mats
    L = S5_CHUNK
    n_chunks = t_pad // L
    bp = -(-bsz // 8) * 8
    ug = u.reshape(bsz, n_chunks, L, S5_GROUPS, S5_GROUP).transpose(3, 1, 0, 2, 4)
    ug = jnp.pad(ug, ((0, 0), (0, 0), (0, bp - bsz), (0, 0), (0, 0)))
    rows = n_chunks * bp
    ug = ug.reshape(S5_GROUPS, rows, L * S5_GROUP)
    blk = pl.BlockSpec((1, rows, 256), lambda g: (g, 0, 0))
    mat = pl.BlockSpec((1, 256, 256), lambda g: (g, 0, 0))
    vec = pl.BlockSpec((1, 1, 128), lambda g: (g, 0, 0))
    yg = pl.pallas_call(
        functools.partial(_s5_kernel, n_chunks=n_chunks, bp=bp),
        grid=(S5_GROUPS,),
        in_specs=[blk, mat, mat, mat, vec, vec],
        out_specs=blk,
        out_shape=jax.ShapeDtypeStruct(ug.shape, F32),
        scratch_shapes=[pltpu.VMEM((rows, 256), F32)] * 3,
        compiler_params=_cparams("parallel"),
        name="s5_ssm",
    )(ug, bmat, toep, cmat, lam_re, lam_im)
    yg = yg.reshape(S5_GROUPS, n_chunks, bp, L, S5_GROUP)[:, :, :bsz]
    return yg.transpose(2, 1, 3, 0, 4).reshape(bsz * t_pad, S5_WIDTH)


def _s5_post_kernel(y_ref, u_ref, d_ref, w_ref, b_ref, o_ref):
    y = y_ref[...] + u_ref[...] * d_ref[...]
    act = y * (0.5 * (1.0 + jnp.tanh(math.sqrt(2.0 / math.pi) * (y + 0.044715 * (y * y * y)))))
    z = _dot(act.astype(BF16), w_ref[...]) + b_ref[...]
    o_ref[...] = (act * jax.nn.sigmoid(z)).astype(o_ref.dtype)


def _s5_post(y, u, d_skip, glu_w, glu_b, bm):
    n = y.shape[0]
    row = pl.BlockSpec((bm, S5_WIDTH), lambda i: (i, 0))
    vec = pl.BlockSpec((1, S5_WIDTH), lambda i: (0, 0))
    return pl.pallas_call(
        _s5_post_kernel,
        grid=(n // bm,),
        in_specs=[row, row, vec, pl.BlockSpec((S5_WIDTH, S5_WIDTH), lambda i: (0, 0)), vec],
        out_specs=row,
        out_shape=jax.ShapeDtypeStruct((n, S5_WIDTH), BF16),
        compiler_params=_cparams("parallel"),
        name="s5_post",
    )(y, u, d_skip.reshape(1, -1), glu_w, glu_b.reshape(1, -1))


def _head_sum(x, e_ref, et_ref):
    hi, lo = _split(x)
    s = _dot(hi, e_ref[...]) + _dot(lo, e_ref[...])
    shi, slo = _split(s)
    return _dot(shi, et_ref[...]) + _dot(slo, et_ref[...])


def _rwkv_prep_kernel(cur_ref, prev_ref, next_ref, mu_ref, w2_ref, a2_ref, g2_ref, w0_ref, a0_ref,
                      kk_ref, ka_ref, rk_ref, e_ref, et_ref,
                      r_o, kk_o, v_o, g_o, bv_o, lw_o, kd_o, bb_o, *, t_valid, bm):
    j = pl.program_id(1)
    p = cur_ref[0]
    row = lax.broadcasted_iota(jnp.int32, (bm, 1), 0)
    prev_row = jnp.where(j > 0, prev_ref[0, 7:8, :], 0.0)
    next_row = jnp.where(j < pl.num_programs(1) - 1, next_ref[0, 0:1, :], 0.0)
    prev = jnp.where(row == 0, prev_row, pltpu.roll(p, 1, 0))
    nxt = jnp.where(row == bm - 1, next_row, pltpu.roll(p, bm - 1, 0))
    xs = p + mu_ref[0:1, :] * (prev - p) + mu_ref[1:2, :] * (nxt - p)
    w = RW_WIDTH
    r = xs[:, 0:w]
    k = xs[:, w:2 * w]
    v = xs[:, 2 * w:3 * w]
    lw = xs[:, 3 * w:3 * w + 128]
    la = xs[:, 3 * w + 128:3 * w + 256]
    lg = xs[:, 3 * w + 256:]
    w_log = _dot(jnp.tanh(lw).astype(BF16), w2_ref[...])
    a_lin = _dot(la.astype(BF16), a2_ref[...])
    g = _dot(jax.nn.sigmoid(lg).astype(BF16), g2_ref[...])
    kk = k * kk_ref[...]
    n2 = _head_sum(kk * kk, e_ref, et_ref)
    kk = kk / jnp.maximum(jnp.sqrt(n2), 1e-12)
    valid = (j * bm + row) < t_valid
    v = jnp.where(valid, v, 0.0)
    kd_sum = jnp.zeros_like(k)
    for z in range(2):
        wl = w_log[:, z * w:(z + 1) * w] + w0_ref[z:z + 1, :]
        lw_o[z, 0] = -math.exp(-0.5) * jax.nn.sigmoid(wl)
        a = jax.nn.sigmoid(a_lin[:, z * w:(z + 1) * w] + a0_ref[z:z + 1, :])
        kd = k * (1.0 + (a - 1.0) * ka_ref[...])
        kd_o[z, 0] = kd.astype(BF16)
        bb_o[z, 0] = (kk * a).astype(BF16)
        kd_sum = kd_sum + kd
    bonus = _head_sum(r * kd_sum * rk_ref[...], e_ref, et_ref)
    r_o[0] = r.astype(BF16)
    kk_o[0] = kk.astype(BF16)
    v_o[0] = v.astype(BF16)
    g_o[0] = g.astype(BF16)
    bv_o[0] = (bonus * v).astype(BF16)


def _rwkv_prep(rw3, wts, t_valid):
    bsz, t_pad, _ = rw3.shape
    bm = _row_block(t_pad, 320, 64)
    nb8 = bm // 8
    last8 = t_pad // 8 - 1
    w = RW_WIDTH
    cur = pl.BlockSpec((1, bm, RW_COLS), lambda b, j: (b, j, 0))
    prev = pl.BlockSpec((1, 8, RW_COLS), lambda b, j: (b, jnp.maximum(j * nb8 - 1, 0), 0))
    nxt = pl.BlockSpec((1, 8, RW_COLS), lambda b, j: (b, jnp.minimum((j + 1) * nb8, last8), 0))

    def full(a):
        return pl.BlockSpec(a.shape, lambda b, j: (0,) * a.ndim)

    shared = pl.BlockSpec((1, bm, w), lambda b, j: (b, j, 0))
    per_dir = pl.BlockSpec((2, 1, bm, w), lambda b, j: (0, b, j, 0))
    consts = [wts['mu'], wts['w2'], wts['a2'], wts['g2'], wts['w0'], wts['a0'],
              wts['k_k'], wts['k_a'], wts['r_k'], wts['head_e'], wts['head_et']]
    sds = jax.ShapeDtypeStruct
    return pl.pallas_call(
        functools.partial(_rwkv_prep_kernel, t_valid=t_valid, bm=bm),
        grid=(bsz, t_pad // bm),
        in_specs=[cur, prev, nxt] + [full(a) for a in consts],
        out_specs=[shared] * 5 + [per_dir] * 3,
        out_shape=[sds((bsz, t_pad, w), BF16)] * 5
        + [sds((2, bsz, t_pad, w), F32), sds((2, bsz, t_pad, w), BF16), sds((2, bsz, t_pad, w), BF16)],
        compiler_params=_cparams("parallel", "parallel"),
        name="rwkv_prep",
    )(rw3, rw3, rw3, *consts)


def _rwkv_chunk(fwd, r_ref, kk_ref, v_ref, lw_ref, kd_ref, bb_ref, y_ref, st_ref, z):
    L = RW_CHUNK
    hd = RW_HEAD
    row = lax.broadcasted_iota(jnp.int32, (L, L), 0)
    col = lax.broadcasted_iota(jnp.int32, (L, L), 1)
    incl = (col <= row) if fwd else (col >= row)
    strict = (col < row) if fwd else (col > row)
    tri = jnp.where(incl, 1.0, 0.0).astype(BF16)
    lw = lw_ref[0, 0]
    lw_hi, lw_lo = _split(lw)
    c = _dot(tri, lw_hi) + _dot(tri, lw_lo)
    e = c - lw
    c_tot = c[L - 1:L, :] if fwd else c[0:1, :]
    r = r_ref[0].astype(F32)
    kk = kk_ref[0].astype(F32)
    kd = kd_ref[0, 0].astype(F32)
    bb = bb_ref[0, 0].astype(F32)
    v = v_ref[0]
    q1 = (kk * jnp.exp(e)).astype(BF16)
    q2 = (r * jnp.exp(c)).astype(BF16)
    inv = jnp.exp(-c)
    k1 = (kd * inv).astype(BF16)
    k2 = (bb * inv).astype(BF16)
    rest = jnp.exp(c_tot - c)
    k1p = (kd * rest).astype(BF16)
    k2p = (bb * rest).astype(BF16)
    dec_tot = jnp.exp(c_tot)
    outs = []
    for i in range(RW_HEADS_PER_STEP):
        sl = slice(i * hd, (i + 1) * hd)
        s0 = st_ref[z, i]
        s0_hi, s0_lo = _split(s0)
        lhs = jnp.concatenate([q1[:, sl], q2[:, sl]], axis=0)
        rhs = jnp.concatenate([k1[:, sl], k2[:, sl], s0_hi, s0_lo], axis=0)
        m1 = _dot_nt(lhs, rhs)
        a_kd = jnp.where(strict, m1[:L, 0:L], 0.0)
        a_b = jnp.where(strict, m1[:L, L:2 * L], 0.0)
        q1s = m1[:L, 2 * L:2 * L + hd] + m1[:L, 2 * L + hd:]
        b_kd = jnp.where(incl, m1[L:, 0:L], 0.0)
        b_b = jnp.where(incl, m1[L:, L:2 * L], 0.0)
        q2s = m1[L:, 2 * L:2 * L + hd] + m1[L:, 2 * L + hd:]
        vh = v[:, sl]
        x = q1s + _dot(a_kd.astype(BF16), vh)
        m = -a_b
        levels = L.bit_length() - 1
        for lvl in range(levels):
            mb = m.astype(BF16)
            if lvl < levels - 1:
                rr = _dot(mb, jnp.concatenate([x.astype(BF16), mb], axis=1))
                x = x + rr[:, :hd]
                m = rr[:, hd:]
            else:
                x = x + _dot(mb, x.astype(BF16))
        ub = x.astype(BF16)
        y = q2s + _dot(jnp.concatenate([b_kd, -b_b], axis=1).astype(BF16),
                       jnp.concatenate([vh, ub], axis=0))
        outs.append(y)
        st_ref[z, i] = s0 * dec_tot[:, sl] + _dot_tn(
            jnp.concatenate([vh, ub], axis=0),
            jnp.concatenate([k1p[:, sl], -k2p[:, sl]], axis=0))
    y_ref[0] = jnp.concatenate(outs, axis=1)


def _rwkv_scan_kernel(rf, kkf, vf, lwf, kdf, bbf, rb, kkb, vb, lwb, kdb, bbb, yf_ref, yb_ref, st_ref):
    @pl.when(pl.program_id(2) == 0)
    def _():
        st_ref[...] = jnp.zeros_like(st_ref)

    _rwkv_chunk(True, rf, kkf, vf, lwf, kdf, bbf, yf_ref, st_ref, 0)
    _rwkv_chunk(False, rb, kkb, vb, lwb, kdb, bbb, yb_ref, st_ref, 1)


def _rwkv_scan(r, kk, v, lw, kd, bb):
    bsz, t_pad, w = r.shape
    L = RW_CHUNK
    nc = t_pad // L
    hw = RW_HEADS_PER_STEP * RW_HEAD
    grid = (bsz, w // hw, nc)

    def shared(fwd):
        if fwd:
            return pl.BlockSpec((1, L, hw), lambda b, h, j: (b, j, h))
        return pl.BlockSpec((1, L, hw), lambda b, h, j: (b, nc - 1 - j, h))

    def per_dir(fwd):
        if fwd:
            return pl.BlockSpec((1, 1, L, hw), lambda b, h, j: (0, b, j, h))
        return pl.BlockSpec((1, 1, L, hw), lambda b, h, j: (1, b, nc - 1 - j, h))

    in_specs = []
    for fwd in (True, False):
        in_specs += [shared(fwd), shared(fwd), shared(fwd), per_dir(fwd), per_dir(fwd), per_dir(fwd)]
    return pl.pallas_call(
        _rwkv_scan_kernel,
        grid=grid,
        in_specs=in_specs,
        out_specs=[shared(True), shared(False)],
        out_shape=[jax.ShapeDtypeStruct((bsz, t_pad, w), F32)] * 2,
        scratch_shapes=[pltpu.VMEM((2, RW_HEADS_PER_STEP, RW_HEAD, RW_HEAD), F32)],
        compiler_params=_cparams("parallel", "parallel", "arbitrary"),
        name="rwkv_scan",
    )(r, kk, v, lw, kd, bb, r, kk, v, lw, kd, bb)


def _rwkv_post_kernel(yf_ref, yb_ref, bv_ref, g_ref, lg_ref, lb_ref, e_ref, et_ref, o_ref):
    y = yf_ref[...] + yb_ref[...]
    mean = _head_sum(y, e_ref, et_ref) * (1.0 / RW_HEAD)
    yc = y - mean
    var = _head_sum(yc * yc, e_ref, et_ref) * (1.0 / RW_HEAD)
    y = yc * lax.rsqrt(var + GN_EPS) * lg_ref[...] + lb_ref[...]
    o_ref[...] = ((y + bv_ref[...].astype(F32)) * g_ref[...].astype(F32)).astype(o_ref.dtype)


def _rwkv_post(yf, yb, bv, g, wts, bm):
    n, w = yf.shape
    row = pl.BlockSpec((bm, w), lambda i: (i, 0))
    vec = pl.BlockSpec((1, w), lambda i: (0, 0))
    e, et = wts['head_e'], wts['head_et']
    return pl.pallas_call(
        _rwkv_post_kernel,
        grid=(n // bm,),
        in_specs=[row, row, row, row, vec, vec,
                  pl.BlockSpec(e.shape, lambda i: (0, 0)), pl.BlockSpec(et.shape, lambda i: (0, 0))],
        out_specs=row,
        out_shape=jax.ShapeDtypeStruct((n, w), BF16),
        compiler_params=_cparams("parallel"),
        name="rwkv_post",
    )(yf, yb, bv, g, wts['lnx_g'], wts['lnx_b'], e, et)


def _merge_kernel(h_ref, s5_ref, rw_ref, wg0_ref, wg1_ref, gb_ref, p0_ref, p1_ref, o_ref):
    h = h_ref[...]
    g0 = jax.nn.sigmoid(_dot(h, wg0_ref[...]) + gb_ref[0:1, :])
    g1 = jax.nn.sigmoid(_dot(h, wg1_ref[...]) + gb_ref[1:2, :])
    merged = g0 * _dot(s5_ref[...], p0_ref[...]) + g1 * _dot(rw_ref[...], p1_ref[...])
    o_ref[...] = merged.astype(o_ref.dtype)


def _merge(h0b, s5_out, rw_out, wts, bm):
    n, d = h0b.shape
    bn = 1024
    nj = d // bn
    return pl.pallas_call(
        _merge_kernel,
        grid=(nj, n // bm),
        in_specs=[pl.BlockSpec((bm, d), lambda j, i: (i, 0)),
                  pl.BlockSpec((bm, S5_WIDTH), lambda j, i: (i, 0)),
                  pl.BlockSpec((bm, RW_WIDTH), lambda j, i: (i, 0)),
                  pl.BlockSpec((d, bn), lambda j, i: (0, j)),
                  pl.BlockSpec((d, bn), lambda j, i: (0, nj + j)),
                  pl.BlockSpec((2, bn), lambda j, i: (0, j)),
                  pl.BlockSpec((S5_WIDTH, bn), lambda j, i: (0, j)),
                  pl.BlockSpec((RW_WIDTH, bn), lambda j, i: (0, j))],
        out_specs=pl.BlockSpec((bm, bn), lambda j, i: (i, j)),
        out_shape=jax.ShapeDtypeStruct((n, d), BF16),
        compiler_params=_cparams("parallel", "parallel"),
        name="merge",
    )(h0b, s5_out, rw_out, wts['w_gate'], wts['w_gate'], wts['gate_b'], wts['proj_s5'], wts['proj_rwkv'])


def _out_kernel(m_ref, h_ref, w_ref, g_ref, b_ref, rh_ref, rl_ref, o_ref, lg_ref):
    x = ALPHA * h_ref[...] + _dot(m_ref[...], w_ref[...])
    h1 = _layernorm(x, g_ref[...], b_ref[...])
    o_ref[...] = h1
    hi, lo = _split(h1)
    lg_ref[...] = _dot(hi, rh_ref[...]) + _dot(lo, rh_ref[...]) + _dot(hi, rl_ref[...])


def _out_proj(merged, h0, wts, bm):
    n, d = h0.shape
    row = pl.BlockSpec((bm, d), lambda i: (i, 0))
    vec = pl.BlockSpec((1, d), lambda i: (0, 0))
    rt = pl.BlockSpec((d, ROUTER_PAD), lambda i: (0, 0))
    return pl.pallas_call(
        _out_kernel,
        grid=(n // bm,),
        in_specs=[row, row, pl.BlockSpec((d, d), lambda i: (0, 0)), vec, vec, rt, rt],
        out_specs=[row, pl.BlockSpec((bm, ROUTER_PAD), lambda i: (i, 0))],
        out_shape=[jax.ShapeDtypeStruct((n, d), F32), jax.ShapeDtypeStruct((n, ROUTER_PAD), F32)],
        compiler_params=_cparams("parallel"),
        name="out_proj",
    )(merged, h0, wts['w_out'], wts['ln1_g'], wts['ln1_b'], wts['router_hi'], wts['router_lo'])


def _route(logits, wts, valid):
    lc = logits[:, :MOE_GROUPS] + wts['router_coarse_b']
    grp = jnp.argmax(lc, axis=-1)
    gate_c = jnp.take_along_axis(jax.nn.softmax(lc, axis=-1), grp[:, None], axis=1)[:, 0]
    lf = (logits[:, MOE_GROUPS:MOE_GROUPS + N_EXPERTS] + wts['router_fine_b'])
    lf = lf.reshape(-1, MOE_GROUPS, EXPERTS_PER_GROUP)
    lf = jnp.take_along_axis(lf, grp[:, None, None], axis=1)[:, 0]
    top_v, top_i = lax.top_k(lf, 2)
    w = gate_c[:, None] * jax.nn.softmax(top_v, axis=-1)
    expert = (grp[:, None] * EXPERTS_PER_GROUP + top_i).astype(jnp.int32)
    expert = jnp.where(valid[:, None], expert, N_EXPERTS)
    w = jnp.where(valid[:, None], w, 0.0)
    n_tok = logits.shape[0]
    n_asg = 2 * n_tok
    e_flat = expert.reshape(-1)
    order = jnp.argsort(e_flat).astype(jnp.int32)
    e_s = e_flat[order]
    counts = jnp.zeros((N_EXPERTS + 1,), jnp.int32).at[e_flat].add(1)[:N_EXPERTS]
    padded = (counts + MOE_ROWS - 1) // MOE_ROWS * MOE_ROWS
    start = jnp.cumsum(counts) - counts
    pend = jnp.cumsum(padded)
    pstart = pend - padded
    n_blocks = -(-n_asg // MOE_ROWS) + N_EXPERTS
    n_rows = n_blocks * MOE_ROWS
    e_c = jnp.minimum(e_s, N_EXPERTS - 1)
    dest = pstart[e_c] + (jnp.arange(n_asg, dtype=jnp.int32) - start[e_c])
    dest = jnp.where(e_s < N_EXPERTS, dest, n_rows)
    row_tok = jnp.zeros((n_rows,), jnp.int32).at[dest].set(order // 2, mode='drop')
    pos = jnp.zeros((n_asg,), jnp.int32).at[order].set(jnp.minimum(dest, n_rows - 1))
    n_used = (pend[-1] // MOE_ROWS).astype(jnp.int32)
    blk = jnp.arange(n_blocks, dtype=jnp.int32)
    blk_exp = jnp.searchsorted(pend, jnp.minimum(blk, n_used - 1) * MOE_ROWS, side='right')
    blk_exp = jnp.minimum(blk_exp, N_EXPERTS - 1).astype(jnp.int32)
    return row_tok, pos, w, blk_exp, n_used.reshape(1), n_blocks


def _gather_kernel(nused_ref, idx_ref, x_hbm, o_ref, buf, sem):
    i = pl.program_id(0)

    @pl.when(i < nused_ref[0])
    def _():
        def issue(r, carry):
            pltpu.make_async_copy(x_hbm.at[pl.ds(idx_ref[0, 0, r], 1)], buf.at[pl.ds(r, 1)], sem).start()
            return carry

        lax.fori_loop(0, MOE_ROWS, issue, 0, unroll=8)

        def wait(r, carry):
            pltpu.make_async_copy(x_hbm.at[pl.ds(0, 1)], buf.at[pl.ds(r, 1)], sem).wait()
            return carry

        lax.fori_loop(0, MOE_ROWS, wait, 0, unroll=8)
        o_ref[...] = buf[...].astype(o_ref.dtype)

    @pl.when(i >= nused_ref[0])
    def _():
        o_ref[...] = jnp.zeros_like(o_ref)


def _moe_gather(h1, row_tok, n_used, n_blocks):
    d = h1.shape[1]
    return pl.pallas_call(
        _gather_kernel,
        grid_spec=pltpu.PrefetchScalarGridSpec(
            num_scalar_prefetch=1,
            grid=(n_blocks,),
            in_specs=[pl.BlockSpec((1, 1, MOE_ROWS), lambda i, nu: (i, 0, 0), memory_space=pltpu.SMEM),
                      pl.BlockSpec(memory_space=pl.ANY)],
            out_specs=pl.BlockSpec((MOE_ROWS, d), lambda i, nu: (i, 0)),
            scratch_shapes=[pltpu.VMEM((MOE_ROWS, d), F32), pltpu.SemaphoreType.DMA(())],
        ),
        out_shape=jax.ShapeDtypeStruct((n_blocks * MOE_ROWS, d), BF16),
        compiler_params=_cparams("arbitrary"),
        name="moe_gather",
    )(n_used, row_tok.reshape(n_blocks, 1, MOE_ROWS), h1)


def _expert_kernel(nused_ref, bexp_ref, x_ref, wg_ref, wu_ref, wd_ref, o_ref):
    @pl.when(pl.program_id(0) < nused_ref[0])
    def _():
        x = x_ref[...]
        hb = jax.nn.silu(_dot(x, wg_ref[0])) * _dot(x, wu_ref[0])
        o_ref[...] = _dot(hb.astype(BF16), wd_ref[0])

    @pl.when(pl.program_id(0) >= nused_ref[0])
    def _():
        o_ref[...] = jnp.zeros_like(o_ref)


def _moe_experts(xg, blk_exp, n_used, wts):
    n_rows, d = xg.shape
    n_blocks = n_rows // MOE_ROWS
    return pl.pallas_call(
        _expert_kernel,
        grid_spec=pltpu.PrefetchScalarGridSpec(
            num_scalar_prefetch=2,
            grid=(n_blocks,),
            in_specs=[pl.BlockSpec((MOE_ROWS, d), lambda i, nu, be: (i, 0)),
                      pl.BlockSpec((1, d, D_EXPERT), lambda i, nu, be: (be[i], 0, 0)),
                      pl.BlockSpec((1, d, D_EXPERT), lambda i, nu, be: (be[i], 0, 0)),
                      pl.BlockSpec((1, D_EXPERT, d), lambda i, nu, be: (be[i], 0, 0))],
            out_specs=pl.BlockSpec((MOE_ROWS, d), lambda i, nu, be: (i, 0)),
        ),
        out_shape=jax.ShapeDtypeStruct((n_rows, d), F32),
        compiler_params=_cparams("arbitrary"),
        name="moe_experts",
    )(n_used, blk_exp, xg, wts['exp_w_gate'], wts['exp_w_up'], wts['exp_w_down'])


def _combine_kernel(pos_ref, eo_hbm, h_ref, w_ref, g_ref, b_ref, o_ref, buf0, buf1, sem, *, bm):
    def issue(r, carry):
        pltpu.make_async_copy(eo_hbm.at[pl.ds(pos_ref[0, 0, 2 * r], 1)], buf0.at[pl.ds(r, 1)], sem).start()
        pltpu.make_async_copy(eo_hbm.at[pl.ds(pos_ref[0, 0, 2 * r + 1], 1)], buf1.at[pl.ds(r, 1)], sem).start()
        return carry

    lax.fori_loop(0, bm, issue, 0, unroll=8)

    def wait(r, carry):
        pltpu.make_async_copy(eo_hbm.at[pl.ds(0, 1)], buf0.at[pl.ds(r, 1)], sem).wait()
        pltpu.make_async_copy(eo_hbm.at[pl.ds(0, 1)], buf1.at[pl.ds(r, 1)], sem).wait()
        return carry

    lax.fori_loop(0, bm, wait, 0, unroll=8)
    w = w_ref[...]
    moe = w[:, 0:1] * buf0[...] + w[:, 1:2] * buf1[...]
    o_ref[...] = _layernorm(ALPHA * h_ref[...] + moe, g_ref[...], b_ref[...])


def _moe_combine(eo, pos, w, h1, wts, bm):
    n, d = h1.shape
    row = pl.BlockSpec((bm, d), lambda i: (i, 0))
    vec = pl.BlockSpec((1, d), lambda i: (0, 0))
    return pl.pallas_call(
        functools.partial(_combine_kernel, bm=bm),
        grid=(n // bm,),
        in_specs=[pl.BlockSpec((1, 1, 2 * bm), lambda i: (i, 0, 0), memory_space=pltpu.SMEM),
                  pl.BlockSpec(memory_space=pl.ANY),
                  row, pl.BlockSpec((bm, 2), lambda i: (i, 0)), vec, vec],
        out_specs=row,
        out_shape=jax.ShapeDtypeStruct((n, d), F32),
        scratch_shapes=[pltpu.VMEM((bm, d), F32), pltpu.VMEM((bm, d), F32), pltpu.SemaphoreType.DMA(())],
        compiler_params=_cparams("arbitrary"),
        name="moe_combine",
    )(pos.reshape(n // bm, 1, 2 * bm), eo, h1, w, wts['ln2_g'], wts['ln2_b'])


def _prepare_weights(p):
    l = 0
    w_in = p['w_in'][l]
    c0 = S5_WIDTH
    c1 = c0 + 3 * RW_WIDTH + 2 * RW_DECAY_LORA + 2 * RW_ICLR_LORA + RW_GATE_LORA
    gpad = RW_GATE_PAD - RW_GATE_LORA
    wts = {}
    wts['w_u'] = w_in[:, :c0].astype(BF16)
    wts['w_rw'] = jnp.pad(w_in[:, c0:c1], ((0, 0), (0, gpad))).astype(BF16)
    wts['w_gate'] = w_in[:, c1:].astype(BF16)
    wts['mu'] = jnp.pad(p['shift_mu'][l], ((0, 0), (0, gpad)))
    z = jnp.zeros((RW_DECAY_LORA, RW_WIDTH), F32)
    wts['w2'] = jnp.block([[p['rw_w2'][l, 0], z], [z, p['rw_w2'][l, 1]]]).astype(BF16)
    wts['a2'] = jnp.block([[p['rw_a2'][l, 0], z], [z, p['rw_a2'][l, 1]]]).astype(BF16)
    wts['g2'] = jnp.pad(p['rw_g2'][l], ((0, gpad), (0, 0))).astype(BF16)
    wts['w0'] = p['rw_w0'][l]
    wts['a0'] = p['rw_a0'][l]
    wts['k_k'] = p['rw_k_k'][l].reshape(1, -1)
    wts['k_a'] = p['rw_k_a'][l].reshape(1, -1)
    wts['r_k'] = p['rw_r_k'][l].reshape(1, -1)
    wts['lnx_g'] = p['rw_lnx_g'][l].reshape(1, -1)
    wts['lnx_b'] = p['rw_lnx_b'][l].reshape(1, -1)
    head = jnp.arange(RW_WIDTH) // RW_HEAD
    e = (head[:, None] == jnp.arange(RW_HEADS)[None, :]).astype(BF16)
    wts['head_e'] = e
    wts['head_et'] = e.T
    wts['s5'] = _s5_matrices(p['s5_B_re'][l], p['s5_B_im'][l], p['s5_A_re'][l], p['s5_A_im'][l],
                             p['s5_log_dt'][l], p['s5_C_re'][l], p['s5_C_im'][l])
    wts['s5_D'] = p['s5_D'][l]
    wts['glu_w'] = p['s5_glu_w'][l].astype(BF16)
    wts['glu_b'] = p['s5_glu_b'][l]
    wts['proj_s5'] = p['proj_s5'][l].astype(BF16)
    wts['proj_rwkv'] = p['proj_rwkv'][l].astype(BF16)
    wts['gate_b'] = p['gate_b'][l]
    wts['w_out'] = p['w_out'][l].astype(BF16)
    wts['ln1_g'] = p['ln1_g'][l].reshape(1, -1)
    wts['ln1_b'] = p['ln1_b'][l].reshape(1, -1)
    router = jnp.concatenate([p['router_coarse'][l], p['router_fine'][l]], axis=1)
    router = jnp.pad(router, ((0, 0), (0, ROUTER_PAD - router.shape[1])))
    wts['router_hi'], wts['router_lo'] = _split(router)
    wts['router_coarse_b'] = p['router_coarse_b'][l]
    wts['router_fine_b'] = p['router_fine_b'][l]
    wts['exp_w_gate'] = p['exp_w_gate'][l].astype(BF16)
    wts['exp_w_up'] = p['exp_w_up'][l].astype(BF16)
    wts['exp_w_down'] = p['exp_w_down'][l].astype(BF16)
    wts['ln2_g'] = p['ln2_g'][l].reshape(1, -1)
    wts['ln2_b'] = p['ln2_b'][l].reshape(1, -1)
    return wts


def _encode(x, p, wts):
    bsz, t, d = x.shape
    t_valid = t + N_META
    t_pad = -(-t_valid // SEQ_ALIGN) * SEQ_ALIGN
    n = bsz * t_pad
    meta = jnp.broadcast_to(p['meta'][None].astype(x.dtype), (bsz, N_META, d))
    h_in = jnp.concatenate([meta, x, jnp.zeros((bsz, t_pad - t_valid, d), x.dtype)], axis=1)
    h0, h0b = _ln_in(h_in, p['ln_in_g'], p['ln_in_b'], t_valid)
    h0 = h0.reshape(n, d)
    h0b = h0b.reshape(n, d)
    bm = _row_block(t_pad, 1024)
    u = _mm(h0b, wts['w_u'], bm, S5_WIDTH, F32, "proj_s5_in")
    rw = _mm(h0b, wts['w_rw'], bm, RW_COLS // 4, F32, "proj_rwkv_in")
    y_ssm = _s5_ssm(u, wts['s5'], bsz, t_pad)
    s5_out = _s5_post(y_ssm, u, wts['s5_D'], wts['glu_w'], wts['glu_b'], bm)
    r, kk, v, g, bv, lw, kd, bb = _rwkv_prep(rw.reshape(bsz, t_pad, RW_COLS), wts, t_valid)
    yf, yb = _rwkv_scan(r, kk, v, lw, kd, bb)
    rw_out = _rwkv_post(yf.reshape(n, -1), yb.reshape(n, -1), bv.reshape(n, -1), g.reshape(n, -1), wts, bm)
    merged = _merge(h0b, s5_out, rw_out, wts, bm)
    h1, logits = _out_proj(merged, h0, wts, _row_block(t_pad, 512))
    valid = (jnp.arange(n, dtype=jnp.int32) % t_pad) < t_valid
    row_tok, pos, w, blk_exp, n_used, n_blocks = _route(logits, wts, valid)
    xg = _moe_gather(h1, row_tok, n_used, n_blocks)
    eo = _moe_experts(xg, blk_exp, n_used, wts)
    h2 = _moe_combine(eo, pos, w, h1, wts, _row_block(t_pad, 256))
    return h2.reshape(bsz, t_pad, d)[:, N_META:t_valid]


def kernel(x_prompt, x_sample, meta, ln_in_g, ln_in_b, w_in, shift_mu, s5_B_re, s5_B_im, s5_A_re, s5_A_im, s5_log_dt, s5_C_re, s5_C_im, s5_D, s5_glu_w, s5_glu_b, rw_w0, rw_w2, rw_a0, rw_a2, rw_g2, rw_k_k, rw_k_a, rw_r_k, rw_lnx_g, rw_lnx_b, proj_s5, proj_rwkv, gate_b, w_out, ln1_g, ln1_b, router_coarse, router_coarse_b, router_fine, router_fine_b, exp_w_gate, exp_w_up, exp_w_down, ln2_g, ln2_b):
    p = {
        'meta': meta, 'ln_in_g': ln_in_g, 'ln_in_b': ln_in_b, 'w_in': w_in, 'shift_mu': shift_mu,
        's5_B_re': s5_B_re, 's5_B_im': s5_B_im, 's5_A_re': s5_A_re, 's5_A_im': s5_A_im,
        's5_log_dt': s5_log_dt, 's5_C_re': s5_C_re, 's5_C_im': s5_C_im, 's5_D': s5_D,
        's5_glu_w': s5_glu_w, 's5_glu_b': s5_glu_b,
        'rw_w0': rw_w0, 'rw_w2': rw_w2, 'rw_a0': rw_a0, 'rw_a2': rw_a2, 'rw_g2': rw_g2,
        'rw_k_k': rw_k_k, 'rw_k_a': rw_k_a, 'rw_r_k': rw_r_k, 'rw_lnx_g': rw_lnx_g, 'rw_lnx_b': rw_lnx_b,
        'proj_s5': proj_s5, 'proj_rwkv': proj_rwkv, 'gate_b': gate_b, 'w_out': w_out,
        'ln1_g': ln1_g, 'ln1_b': ln1_b,
        'router_coarse': router_coarse, 'router_coarse_b': router_coarse_b,
        'router_fine': router_fine, 'router_fine_b': router_fine_b,
        'exp_w_gate': exp_w_gate, 'exp_w_up': exp_w_up, 'exp_w_down': exp_w_down,
        'ln2_g': ln2_g, 'ln2_b': ln2_b,
    }
    wts = _prepare_weights(p)
    return (_encode(x_prompt, p, wts), _encode(x_sample, p, wts))
```

```python
import functools
import math

import jax
import jax.numpy as jnp
from jax import lax
from jax.experimental import pallas as pl
from jax.experimental.pallas import tpu as pltpu

F32 = jnp.float32
BF16 = jnp.bfloat16

D_MODEL = 2048
N_META = 16
S5_WIDTH = 1024
S5_GROUP = 16
S5_GROUPS = 64
S5_STATE = 64
S5_CHUNK = 16
RW_WIDTH = 1024
RW_HEAD = 64
RW_HEADS = 16
RW_DECAY_LORA = 64
RW_ICLR_LORA = 64
RW_GATE_LORA = 160
RW_GATE_PAD = 256
RW_COLS = 3 * RW_WIDTH + 2 * RW_DECAY_LORA + 2 * RW_ICLR_LORA + RW_GATE_PAD
RW_CHUNK = 64
RW_HEADS_PER_STEP = 2
MOE_GROUPS = 4
EXPERTS_PER_GROUP = 8
N_EXPERTS = 32
D_EXPERT = 512
MOE_ROWS = 256
ROUTER_PAD = 128
DEPTH = 1
ALPHA = (2 * DEPTH) ** 0.25
LN_EPS = 1e-5
GN_EPS = 64e-5
SEQ_ALIGN = 64
VMEM_LIMIT = 56 * 1024 * 1024


def _cparams(*sem):
    return pltpu.CompilerParams(dimension_semantics=sem, vmem_limit_bytes=VMEM_LIMIT)


def _row_block(t_pad, cap, mult=8):
    best = mult
    for d in range(mult, cap + 1, mult):
        if t_pad % d == 0:
            best = d
    return best


def _dot(a, b):
    return jnp.dot(a, b, preferred_element_type=F32)


def _dot_nt(a, b):
    return lax.dot_general(a, b, (((1,), (1,)), ((), ())), preferred_element_type=F32)


def _dot_tn(a, b):
    return lax.dot_general(a, b, (((0,), (0,)), ((), ())), preferred_element_type=F32)


def _split(x):
    hi = x.astype(BF16)
    lo = (x - hi.astype(F32)).astype(BF16)
    return hi, lo


def _layernorm(x, g, b):
    mu = jnp.mean(x, axis=-1, keepdims=True)
    xc = x - mu
    var = jnp.mean(xc * xc, axis=-1, keepdims=True)
    return xc * lax.rsqrt(var + LN_EPS) * g + b


def _ln_in_kernel(x_ref, g_ref, b_ref, of_ref, ob_ref, *, t_valid, bm):
    y = _layernorm(x_ref[0], g_ref[...], b_ref[...])
    pos = pl.program_id(1) * bm + lax.broadcasted_iota(jnp.int32, (bm, 1), 0)
    y = jnp.where(pos < t_valid, y, 0.0)
    of_ref[0] = y
    ob_ref[0] = y.astype(BF16)


def _ln_in(x3, g, b, t_valid):
    bsz, t_pad, d = x3.shape
    bm = _row_block(t_pad, 1024)
    row = pl.BlockSpec((1, bm, d), lambda i, j: (i, j, 0))
    vec = pl.BlockSpec((1, d), lambda i, j: (0, 0))
    return pl.pallas_call(
        functools.partial(_ln_in_kernel, t_valid=t_valid, bm=bm),
        grid=(bsz, t_pad // bm),
        in_specs=[row, vec, vec],
        out_specs=[row, row],
        out_shape=[jax.ShapeDtypeStruct(x3.shape, F32), jax.ShapeDtypeStruct(x3.shape, BF16)],
        compiler_params=_cparams("parallel", "parallel"),
        name="ln_in",
    )(x3, g.reshape(1, d), b.reshape(1, d))


def _mm_kernel(x_ref, w_ref, o_ref):
    o_ref[...] = _dot(x_ref[...], w_ref[...]).astype(o_ref.dtype)


def _mm(x, w, bm, bn, out_dtype, name):
    n, k = x.shape
    m = w.shape[1]
    return pl.pallas_call(
        _mm_kernel,
        grid=(m // bn, n // bm),
        in_specs=[pl.BlockSpec((bm, k), lambda j, i: (i, 0)),
                  pl.BlockSpec((k, bn), lambda j, i: (0, j))],
        out_specs=pl.BlockSpec((bm, bn), lambda j, i: (i, j)),
        out_shape=jax.ShapeDtypeStruct((n, m), out_dtype),
        compiler_params=_cparams("parallel", "parallel"),
        name=name,
    )(x, w)


def _s5_matrices(b_re, b_im, a_re, a_im, log_dt, c_re, c_im):
    L = S5_CHUNK
    dt = jnp.exp(log_dt)[..., None]
    mag = jnp.exp(a_re * dt)
    abr = mag * jnp.cos(a_im * dt)
    abi = mag * jnp.sin(a_im * dt)
    den = a_re * a_re + a_im * a_im
    nr = abr - 1.0
    cr = (nr * a_re + abi * a_im) / den
    ci = (abi * a_re - nr * a_im) / den
    bbr = cr[..., None] * b_re - ci[..., None] * b_im
    bbi = cr[..., None] * b_im + ci[..., None] * b_re
    tau = jnp.arange(L + 1, dtype=F32)[:, None, None, None]
    pmag = jnp.exp(tau * a_re * dt)
    pr = pmag * jnp.cos(tau * a_im * dt)
    pi = pmag * jnp.sin(tau * a_im * dt)
    wr = pr[..., None] * bbr - pi[..., None] * bbi
    wi = pr[..., None] * bbi + pi[..., None] * bbr
    kern = (jnp.einsum('zgop,tzgpi->tzgoi', c_re, wr)
            - jnp.einsum('zgop,tzgpi->tzgoi', c_im, wi))
    s = jnp.arange(L)[:, None]
    t = jnp.arange(L)[None, :]
    lag = t - s
    kf = jnp.where((lag >= 0)[..., None, None, None], kern[jnp.clip(lag, 0, L), 0], 0.0)
    kb = jnp.where((lag <= 0)[..., None, None, None], kern[jnp.clip(-lag, 0, L), 1], 0.0)
    toep = (kf + kb).transpose(2, 0, 4, 1, 3).reshape(S5_GROUPS, L * S5_GROUP, L * S5_GROUP)
    wf_r, wf_i = wr[::-1][1:, 0], wi[::-1][1:, 0]
    wb_r, wb_i = wr[:L, 1], wi[:L, 1]
    bmat = jnp.concatenate([wf_r, wb_r, wf_i, wb_i], axis=2)
    bmat = bmat.transpose(1, 0, 3, 2).reshape(S5_GROUPS, L * S5_GROUP, 4 * S5_STATE)
    pf_r, pf_i = pr[1:, 0], pi[1:, 0]
    pb_r, pb_i = pr[::-1][:L, 1], pi[::-1][:L, 1]
    c0r, c0i, c1r, c1i = c_re[0], c_im[0], c_re[1], c_im[1]

    def cpow(cre, cim, p_r, p_i):
        re = cre[None] * p_r[:, :, None, :] - cim[None] * p_i[:, :, None, :]
        im = cre[None] * p_i[:, :, None, :] + cim[None] * p_r[:, :, None, :]
        return re, -im

    f_re, f_im = cpow(c0r, c0i, pf_r, pf_i)
    g_re, g_im = cpow(c1r, c1i, pb_r, pb_i)
    cmat = jnp.concatenate([f_re, g_re, f_im, g_im], axis=3)
    cmat = cmat.transpose(1, 3, 0, 2).reshape(S5_GROUPS, 4 * S5_STATE, L * S5_GROUP)
    lam_re = jnp.concatenate([pr[L, 0], pr[L, 1]], axis=-1)[:, None, :]
    lam_im = jnp.concatenate([pi[L, 0], pi[L, 1]], axis=-1)[:, None, :]
    return bmat.astype(BF16), toep.astype(BF16), cmat.astype(BF16), lam_re, lam_im


def _s5_kernel(u_ref, bmat_ref, toep_ref, cmat_ref, lre_ref, lim_ref, y_ref,
               s_scr, xf_scr, xb_scr, *, n_chunks, bp):
    u = u_ref[0].astype(BF16)
    s_scr[...] = _dot(u, bmat_ref[0])
    a_re = lre_ref[0]
    a_im = lim_ref[0]
    is_fwd = lax.broadcasted_iota(jnp.int32, (bp, 128), 1) < S5_STATE

    def step(j, carry):
        xr, xi = carry
        rf = pl.multiple_of(j * bp, bp)
        rb = pl.multiple_of((n_chunks - 1 - j) * bp, bp)
        x_in = jnp.concatenate([xr, xi], axis=1)
        xf_scr[pl.ds(rf, bp), :] = x_in
        xb_scr[pl.ds(rb, bp), :] = x_in
        sf = s_scr[pl.ds(rf, bp), :]
        sb = s_scr[pl.ds(rb, bp), :]
        s_re = jnp.where(is_fwd, sf[:, :128], sb[:, :128])
        s_im = jnp.where(is_fwd, sf[:, 128:], sb[:, 128:])
        return (a_re * xr - a_im * xi + s_re, a_re * xi + a_im * xr + s_im)

    zero = jnp.zeros((bp, 128), F32)
    lax.fori_loop(0, n_chunks, step, (zero, zero))
    lane = lax.broadcasted_iota(jnp.int32, (1, 256), 1)
    fwd_lane = (lane % 128) < S5_STATE
    x_in = jnp.where(fwd_lane, xf_scr[...], xb_scr[...]).astype(BF16)
    y_ref[0] = _dot(u, toep_ref[0]) + _dot(x_in, cmat_ref[0])


def _s5_ssm(u, mats, bsz, t_pad):
    bmat, toep, cmat, lam_re, lam_im = mats
    L = S5_CHUNK
    n_chunks = t_pad // L
    bp = -(-bsz // 8) * 8
    ug = u.reshape(bsz, n_chunks, L, S5_GROUPS, S5_GROUP).transpose(3, 1, 0, 2, 4)
    ug = jnp.pad(ug, ((0, 0), (0, 0), (0, bp - bsz), (0, 0), (0, 0)))
    rows = n_chunks * bp
    ug = ug.reshape(S5_GROUPS, rows, L * S5_GROUP)
    blk = pl.BlockSpec((1, rows, 256), lambda g: (g, 0, 0))
    mat = pl.BlockSpec((1, 256, 256), lambda g: (g, 0, 0))
    vec = pl.BlockSpec((1, 1, 128), lambda g: (g, 0, 0))
    yg = pl.pallas_call(
        functools.partial(_s5_kernel, n_chunks=n_chunks, bp=bp),
        grid=(S5_GROUPS,),
        in_specs=[blk, mat, mat, mat, vec, vec],
        out_specs=blk,
        out_shape=jax.ShapeDtypeStruct(ug.shape, F32),
        scratch_shapes=[pltpu.VMEM((rows, 256), F32)] * 3,
        compiler_params=_cparams("parallel"),
        name="s5_ssm",
    )(ug, bmat, toep, cmat, lam_re, lam_im)
    yg = yg.reshape(S5_GROUPS, n_chunks, bp, L, S5_GROUP)[:, :, :bsz]
    return yg.transpose(2, 1, 3, 0, 4).reshape(bsz * t_pad, S5_WIDTH)


def _s5_post_kernel(y_ref, u_ref, d_ref, w_ref, b_ref, o_ref):
    y = y_ref[...] + u_ref[...] * d_ref[...]
    act = y * (0.5 * (1.0 + jnp.tanh(math.sqrt(2.0 / math.pi) * (y + 0.044715 * (y * y * y)))))
    z = _dot(act.astype(BF16), w_ref[...]) + b_ref[...]
    o_ref[...] = (act * jax.nn.sigmoid(z)).astype(o_ref.dtype)


def _s5_post(y, u, d_skip, glu_w, glu_b, bm):
    n = y.shape[0]
    row = pl.BlockSpec((bm, S5_WIDTH), lambda i: (i, 0))
    vec = pl.BlockSpec((1, S5_WIDTH), lambda i: (0, 0))
    return pl.pallas_call(
        _s5_post_kernel,
        grid=(n // bm,),
        in_specs=[row, row, vec, pl.BlockSpec((S5_WIDTH, S5_WIDTH), lambda i: (0, 0)), vec],
        out_specs=row,
        out_shape=jax.ShapeDtypeStruct((n, S5_WIDTH), BF16),
        compiler_params=_cparams("parallel"),
        name="s5_post",
    )(y, u, d_skip.reshape(1, -1), glu_w, glu_b.reshape(1, -1))


def _head_sum(x, e_ref, et_ref):
    hi, lo = _split(x)
    s = _dot(hi, e_ref[...]) + _dot(lo, e_ref[...])
    shi, slo = _split(s)
    return _dot(shi, et_ref[...]) + _dot(slo, et_ref[...])


def _rwkv_prep_kernel(cur_ref, prev_ref, next_ref, mu_ref, w2_ref, a2_ref, g2_ref, w0_ref, a0_ref,
                      kk_ref, ka_ref, rk_ref, e_ref, et_ref,
                      r_o, kk_o, v_o, g_o, bv_o, lw_o, kd_o, bb_o, *, t_valid, bm):
    j = pl.program_id(1)
    p = cur_ref[0]
    row = lax.broadcasted_iota(jnp.int32, (bm, 1), 0)
    prev_row = jnp.where(j > 0, prev_ref[0, 7:8, :], 0.0)
    next_row = jnp.where(j < pl.num_programs(1) - 1, next_ref[0, 0:1, :], 0.0)
    prev = jnp.where(row == 0, prev_row, pltpu.roll(p, 1, 0))
    nxt = jnp.where(row == bm - 1, next_row, pltpu.roll(p, bm - 1, 0))
    xs = p + mu_ref[0:1, :] * (prev - p) + mu_ref[1:2, :] * (nxt - p)
    w = RW_WIDTH
    r = xs[:, 0:w]
    k = xs[:, w:2 * w]
    v = xs[:, 2 * w:3 * w]
    lw = xs[:, 3 * w:3 * w + 128]
    la = xs[:, 3 * w + 128:3 * w + 256]
    lg = xs[:, 3 * w + 256:]
    w_log = _dot(jnp.tanh(lw).astype(BF16), w2_ref[...])
    a_lin = _dot(la.astype(BF16), a2_ref[...])
    g = _dot(jax.nn.sigmoid(lg).astype(BF16), g2_ref[...])
    kk = k * kk_ref[...]
    n2 = _head_sum(kk * kk, e_ref, et_ref)
    kk = kk / jnp.maximum(jnp.sqrt(n2), 1e-12)
    valid = (j * bm + row) < t_valid
    v = jnp.where(valid, v, 0.0)
    kd_sum = jnp.zeros_like(k)
    for z in range(2):
        wl = w_log[:, z * w:(z + 1) * w] + w0_ref[z:z + 1, :]
        lw_o[z, 0] = -math.exp(-0.5) * jax.nn.sigmoid(wl)
        a = jax.nn.sigmoid(a_lin[:, z * w:(z + 1) * w] + a0_ref[z:z + 1, :])
        kd = k * (1.0 + (a - 1.0) * ka_ref[...])
        kd_o[z, 0] = kd.astype(BF16)
        bb_o[z, 0] = (kk * a).astype(BF16)
        kd_sum = kd_sum + kd
    bonus = _head_sum(r * kd_sum * rk_ref[...], e_ref, et_ref)
    r_o[0] = r.astype(BF16)
    kk_o[0] = kk.astype(BF16)
    v_o[0] = v.astype(BF16)
    g_o[0] = g.astype(BF16)
    bv_o[0] = (bonus * v).astype(BF16)


def _rwkv_prep(rw3, wts, t_valid):
    bsz, t_pad, _ = rw3.shape
    bm = _row_block(t_pad, 320, 64)
    nb8 = bm // 8
    last8 = t_pad // 8 - 1
    w = RW_WIDTH
    cur = pl.BlockSpec((1, bm, RW_COLS), lambda b, j: (b, j, 0))
    prev = pl.BlockSpec((1, 8, RW_COLS), lambda b, j: (b, jnp.maximum(j * nb8 - 1, 0), 0))
    nxt = pl.BlockSpec((1, 8, RW_COLS), lambda b, j: (b, jnp.minimum((j + 1) * nb8, last8), 0))

    def full(a):
        return pl.BlockSpec(a.shape, lambda b, j: (0,) * a.ndim)

    shared = pl.BlockSpec((1, bm, w), lambda b, j: (b, j, 0))
    per_dir = pl.BlockSpec((2, 1, bm, w), lambda b, j: (0, b, j, 0))
    consts = [wts['mu'], wts['w2'], wts['a2'], wts['g2'], wts['w0'], wts['a0'],
              wts['k_k'], wts['k_a'], wts['r_k'], wts['head_e'], wts['head_et']]
    sds = jax.ShapeDtypeStruct
    return pl.pallas_call(
        functools.partial(_rwkv_prep_kernel, t_valid=t_valid, bm=bm),
        grid=(bsz, t_pad // bm),
        in_specs=[cur, prev, nxt] + [full(a) for a in consts],
        out_specs=[shared] * 5 + [per_dir] * 3,
        out_shape=[sds((bsz, t_pad, w), BF16)] * 5
        + [sds((2, bsz, t_pad, w), F32), sds((2, bsz, t_pad, w), BF16), sds((2, bsz, t_pad, w), BF16)],
        compiler_params=_cparams("parallel", "parallel"),
        name="rwkv_prep",
    )(rw3, rw3, rw3, *consts)


def _rwkv_chunk(fwd, r_ref, kk_ref, v_ref, lw_ref, kd_ref, bb_ref, y_ref, st_ref, z):
    L = RW_CHUNK
    hd = RW_HEAD
    gw = RW_HEADS_PER_STEP * hd
    n_groups = RW_WIDTH // gw
    row = lax.broadcasted_iota(jnp.int32, (L, L), 0)
    col = lax.broadcasted_iota(jnp.int32, (L, L), 1)
    tri = jnp.where((col <= row) if fwd else (col >= row), 1.0, 0.0).astype(BF16)
    grow = lax.broadcasted_iota(jnp.int32, (L, gw), 0)
    gcol = lax.broadcasted_iota(jnp.int32, (L, gw), 1) % L
    incl = (gcol <= grow) if fwd else (gcol >= grow)
    strict = (gcol < grow) if fwd else (gcol > grow)
    bd_mask = jnp.where(lax.broadcasted_iota(jnp.int32, (gw, gw), 0) // hd
                        == lax.broadcasted_iota(jnp.int32, (gw, gw), 1) // hd, 1.0, 0.0).astype(BF16)

    def bd(x):
        return jnp.concatenate([x] * RW_HEADS_PER_STEP, axis=0) * bd_mask

    def stack(x):
        return jnp.concatenate([x[:, h * hd:(h + 1) * hd] for h in range(RW_HEADS_PER_STEP)], axis=0)

    lw = lw_ref[0, 0]
    lw_hi, lw_lo = _split(lw)
    c = _dot(tri, lw_hi) + _dot(tri, lw_lo)
    e = c - lw
    c_tot = c[L - 1:L, :] if fwd else c[0:1, :]
    r = r_ref[0].astype(F32)
    kk = kk_ref[0].astype(F32)
    kd = kd_ref[0, 0].astype(F32)
    bb = bb_ref[0, 0].astype(F32)
    v = v_ref[0]
    q1 = (kk * jnp.exp(e)).astype(BF16)
    q2 = (r * jnp.exp(c)).astype(BF16)
    inv = jnp.exp(-c)
    k1 = (kd * inv).astype(BF16)
    k2 = (bb * inv).astype(BF16)
    rest = jnp.exp(c_tot - c)
    k1p = (kd * rest).astype(BF16)
    k2p = (bb * rest).astype(BF16)
    dec_tot = jnp.exp(c_tot)
    def group_chain(g):
        sl = slice(g * gw, (g + 1) * gw)
        s0 = st_ref[z, g]
        s0_hi, s0_lo = _split(s0)
        lhs = jnp.concatenate([q1[:, sl], q2[:, sl]], axis=0)
        rhs = jnp.concatenate([bd(k1[:, sl]), bd(k2[:, sl]), bd(s0_hi), bd(s0_lo)], axis=0)
        m1 = _dot_nt(lhs, rhs)
        yield
        a_kd = jnp.where(strict, m1[:L, 0:gw], 0.0)
        a_b = jnp.where(strict, m1[:L, gw:2 * gw], 0.0)
        q1s = m1[:L, 2 * gw:3 * gw] + m1[:L, 3 * gw:]
        b_kd = jnp.where(incl, m1[L:, 0:gw], 0.0)
        b_b = jnp.where(incl, m1[L:, gw:2 * gw], 0.0)
        q2s = m1[L:, 2 * gw:3 * gw] + m1[L:, 3 * gw:]
        vg = v[:, sl]
        v_bd = bd(vg)
        x = q1s + _dot(a_kd.astype(BF16), v_bd)
        yield
        m = -a_b
        levels = L.bit_length() - 1
        for lvl in range(levels):
            mb = m.astype(BF16)
            if lvl < levels - 1:
                rr = _dot(mb, jnp.concatenate([bd(x.astype(BF16)), bd(mb)], axis=1))
                x = x + rr[:, :gw]
                m = rr[:, gw:]
            else:
                x = x + _dot(mb, bd(x.astype(BF16)))
            yield
        ub = x.astype(BF16)
        y = q2s + _dot(jnp.concatenate([b_kd, -b_b], axis=1).astype(BF16),
                       jnp.concatenate([v_bd, bd(ub)], axis=0))
        y_ref[0, :, sl] = y
        yield
        st_ref[z, g] = s0 * dec_tot[:, sl] + _dot_tn(
            jnp.concatenate([stack(vg), stack(ub)], axis=0),
            jnp.concatenate([bd(k1p[:, sl]), -bd(k2p[:, sl])], axis=0))

    return [group_chain(g) for g in range(n_groups)]


def _rwkv_scan_kernel(rf, kkf, vf, lwf, kdf, bbf, rb, kkb, vb, lwb, kdb, bbb, yf_ref, yb_ref, st_ref):
    @pl.when(pl.program_id(1) == 0)
    def _():
        st_ref[...] = jnp.zeros_like(st_ref)

    chains = (_rwkv_chunk(True, rf, kkf, vf, lwf, kdf, bbf, yf_ref, st_ref, 0)
              + _rwkv_chunk(False, rb, kkb, vb, lwb, kdb, bbb, yb_ref, st_ref, 1))
    while chains:
        alive = []
        for chain in chains:
            try:
                next(chain)
                alive.append(chain)
            except StopIteration:
                pass
        chains = alive


def _rwkv_scan(r, kk, v, lw, kd, bb):
    bsz, t_pad, w = r.shape
    L = RW_CHUNK
    nc = t_pad // L
    gw = RW_HEADS_PER_STEP * RW_HEAD
    grid = (bsz, nc)

    def shared(fwd):
        if fwd:
            return pl.BlockSpec((1, L, w), lambda b, j: (b, j, 0))
        return pl.BlockSpec((1, L, w), lambda b, j: (b, nc - 1 - j, 0))

    def per_dir(fwd):
        if fwd:
            return pl.BlockSpec((1, 1, L, w), lambda b, j: (0, b, j, 0))
        return pl.BlockSpec((1, 1, L, w), lambda b, j: (1, b, nc - 1 - j, 0))

    in_specs = []
    for fwd in (True, False):
        in_specs += [shared(fwd), shared(fwd), shared(fwd), per_dir(fwd), per_dir(fwd), per_dir(fwd)]
    return pl.pallas_call(
        _rwkv_scan_kernel,
        grid=grid,
        in_specs=in_specs,
        out_specs=[shared(True), shared(False)],
        out_shape=[jax.ShapeDtypeStruct((bsz, t_pad, w), F32)] * 2,
        scratch_shapes=[pltpu.VMEM((2, w // gw, RW_HEAD, gw), F32)],
        compiler_params=_cparams("parallel", "arbitrary"),
        name="rwkv_scan",
    )(r, kk, v, lw, kd, bb, r, kk, v, lw, kd, bb)


def _rwkv_post_kernel(yf_ref, yb_ref, bv_ref, g_ref, lg_ref, lb_ref, e_ref, et_ref, o_ref):
    y = yf_ref[...] + yb_ref[...]
    mean = _head_sum(y, e_ref, et_ref) * (1.0 / RW_HEAD)
    yc = y - mean
    var = _head_sum(yc * yc, e_ref, et_ref) * (1.0 / RW_HEAD)
    y = yc * lax.rsqrt(var + GN_EPS) * lg_ref[...] + lb_ref[...]
    o_ref[...] = ((y + bv_ref[...].astype(F32)) * g_ref[...].astype(F32)).astype(o_ref.dtype)


def _rwkv_post(yf, yb, bv, g, wts, bm):
    n, w = yf.shape
    row = pl.BlockSpec((bm, w), lambda i: (i, 0))
    vec = pl.BlockSpec((1, w), lambda i: (0, 0))
    e, et = wts['head_e'], wts['head_et']
    return pl.pallas_call(
        _rwkv_post_kernel,
        grid=(n // bm,),
        in_specs=[row, row, row, row, vec, vec,
                  pl.BlockSpec(e.shape, lambda i: (0, 0)), pl.BlockSpec(et.shape, lambda i: (0, 0))],
        out_specs=row,
        out_shape=jax.ShapeDtypeStruct((n, w), BF16),
        compiler_params=_cparams("parallel"),
        name="rwkv_post",
    )(yf, yb, bv, g, wts['lnx_g'], wts['lnx_b'], e, et)


def _merge_kernel(h_ref, s5_ref, rw_ref, wg0_ref, wg1_ref, gb_ref, p0_ref, p1_ref, o_ref):
    h = h_ref[...]
    g0 = jax.nn.sigmoid(_dot(h, wg0_ref[...]) + gb_ref[0:1, :])
    g1 = jax.nn.sigmoid(_dot(h, wg1_ref[...]) + gb_ref[1:2, :])
    merged = g0 * _dot(s5_ref[...], p0_ref[...]) + g1 * _dot(rw_ref[...], p1_ref[...])
    o_ref[...] = merged.astype(o_ref.dtype)


def _merge(h0b, s5_out, rw_out, wts, bm):
    n, d = h0b.shape
    bn = 1024
    nj = d // bn
    return pl.pallas_call(
        _merge_kernel,
        grid=(nj, n // bm),
        in_specs=[pl.BlockSpec((bm, d), lambda j, i: (i, 0)),
                  pl.BlockSpec((bm, S5_WIDTH), lambda j, i: (i, 0)),
                  pl.BlockSpec((bm, RW_WIDTH), lambda j, i: (i, 0)),
                  pl.BlockSpec((d, bn), lambda j, i: (0, j)),
                  pl.BlockSpec((d, bn), lambda j, i: (0, nj + j)),
                  pl.BlockSpec((2, bn), lambda j, i: (0, j)),
                  pl.BlockSpec((S5_WIDTH, bn), lambda j, i: (0, j)),
                  pl.BlockSpec((RW_WIDTH, bn), lambda j, i: (0, j))],
        out_specs=pl.BlockSpec((bm, bn), lambda j, i: (i, j)),
        out_shape=jax.ShapeDtypeStruct((n, d), BF16),
        compiler_params=_cparams("parallel", "parallel"),
        name="merge",
    )(h0b, s5_out, rw_out, wts['w_gate'], wts['w_gate'], wts['gate_b'], wts['proj_s5'], wts['proj_rwkv'])


def _out_kernel(m_ref, h_ref, w_ref, g_ref, b_ref, rh_ref, rl_ref, o_ref, lg_ref):
    x = ALPHA * h_ref[...] + _dot(m_ref[...], w_ref[...])
    h1 = _layernorm(x, g_ref[...], b_ref[...])
    o_ref[...] = h1
    hi, lo = _split(h1)
    lg_ref[...] = _dot(hi, rh_ref[...]) + _dot(lo, rh_ref[...]) + _dot(hi, rl_ref[...])


def _out_proj(merged, h0, wts, bm):
    n, d = h0.shape
    row = pl.BlockSpec((bm, d), lambda i: (i, 0))
    vec = pl.BlockSpec((1, d), lambda i: (0, 0))
    rt = pl.BlockSpec((d, ROUTER_PAD), lambda i: (0, 0))
    return pl.pallas_call(
        _out_kernel,
        grid=(n // bm,),
        in_specs=[row, row, pl.BlockSpec((d, d), lambda i: (0, 0)), vec, vec, rt, rt],
        out_specs=[row, pl.BlockSpec((bm, ROUTER_PAD), lambda i: (i, 0))],
        out_shape=[jax.ShapeDtypeStruct((n, d), F32), jax.ShapeDtypeStruct((n, ROUTER_PAD), F32)],
        compiler_params=_cparams("parallel"),
        name="out_proj",
    )(merged, h0, wts['w_out'], wts['ln1_g'], wts['ln1_b'], wts['router_hi'], wts['router_lo'])


def _route(logits, wts, valid):
    lc = logits[:, :MOE_GROUPS] + wts['router_coarse_b']
    grp = jnp.argmax(lc, axis=-1)
    gate_c = jnp.take_along_axis(jax.nn.softmax(lc, axis=-1), grp[:, None], axis=1)[:, 0]
    lf = (logits[:, MOE_GROUPS:MOE_GROUPS + N_EXPERTS] + wts['router_fine_b'])
    lf = lf.reshape(-1, MOE_GROUPS, EXPERTS_PER_GROUP)
    lf = jnp.take_along_axis(lf, grp[:, None, None], axis=1)[:, 0]
    top_v, top_i = lax.top_k(lf, 2)
    w = gate_c[:, None] * jax.nn.softmax(top_v, axis=-1)
    expert = (grp[:, None] * EXPERTS_PER_GROUP + top_i).astype(jnp.int32)
    expert = jnp.where(valid[:, None], expert, N_EXPERTS)
    w = jnp.where(valid[:, None], w, 0.0)
    n_tok = logits.shape[0]
    n_asg = 2 * n_tok
    e_flat = expert.reshape(-1)
    order = jnp.argsort(e_flat).astype(jnp.int32)
    e_s = e_flat[order]
    counts = jnp.zeros((N_EXPERTS + 1,), jnp.int32).at[e_flat].add(1)[:N_EXPERTS]
    padded = (counts + MOE_ROWS - 1) // MOE_ROWS * MOE_ROWS
    start = jnp.cumsum(counts) - counts
    pend = jnp.cumsum(padded)
    pstart = pend - padded
    n_blocks = -(-n_asg // MOE_ROWS) + N_EXPERTS
    n_rows = n_blocks * MOE_ROWS
    e_c = jnp.minimum(e_s, N_EXPERTS - 1)
    dest = pstart[e_c] + (jnp.arange(n_asg, dtype=jnp.int32) - start[e_c])
    dest = jnp.where(e_s < N_EXPERTS, dest, n_rows)
    row_tok = jnp.zeros((n_rows,), jnp.int32).at[dest].set(order // 2, mode='drop')
    pos = jnp.zeros((n_asg,), jnp.int32).at[order].set(jnp.minimum(dest, n_rows - 1))
    n_used = (pend[-1] // MOE_ROWS).astype(jnp.int32)
    blk = jnp.arange(n_blocks, dtype=jnp.int32)
    blk_exp = jnp.searchsorted(pend, jnp.minimum(blk, n_used - 1) * MOE_ROWS, side='right')
    blk_exp = jnp.minimum(blk_exp, N_EXPERTS - 1).astype(jnp.int32)
    return row_tok, pos, w, blk_exp, n_used.reshape(1), n_blocks


def _gather_kernel(nused_ref, idx_ref, x_hbm, o_ref, buf, sem):
    i = pl.program_id(0)

    @pl.when(i < nused_ref[0])
    def _():
        def issue(r, carry):
            pltpu.make_async_copy(x_hbm.at[pl.ds(idx_ref[0, 0, r], 1)], buf.at[pl.ds(r, 1)], sem).start()
            return carry

        lax.fori_loop(0, MOE_ROWS, issue, 0, unroll=8)

        def wait(r, carry):
            pltpu.make_async_copy(x_hbm.at[pl.ds(0, 1)], buf.at[pl.ds(r, 1)], sem).wait()
            return carry

        lax.fori_loop(0, MOE_ROWS, wait, 0, unroll=8)
        o_ref[...] = buf[...].astype(o_ref.dtype)

    @pl.when(i >= nused_ref[0])
    def _():
        o_ref[...] = jnp.zeros_like(o_ref)


def _moe_gather(h1, row_tok, n_used, n_blocks):
    d = h1.shape[1]
    return pl.pallas_call(
        _gather_kernel,
        grid_spec=pltpu.PrefetchScalarGridSpec(
            num_scalar_prefetch=1,
            grid=(n_blocks,),
            in_specs=[pl.BlockSpec((1, 1, MOE_ROWS), lambda i, nu: (i, 0, 0), memory_space=pltpu.SMEM),
                      pl.BlockSpec(memory_space=pl.ANY)],
            out_specs=pl.BlockSpec((MOE_ROWS, d), lambda i, nu: (i, 0)),
            scratch_shapes=[pltpu.VMEM((MOE_ROWS, d), F32), pltpu.SemaphoreType.DMA(())],
        ),
        out_shape=jax.ShapeDtypeStruct((n_blocks * MOE_ROWS, d), BF16),
        compiler_params=_cparams("arbitrary"),
        name="moe_gather",
    )(n_used, row_tok.reshape(n_blocks, 1, MOE_ROWS), h1)


def _expert_kernel(nused_ref, bexp_ref, x_ref, wg_ref, wu_ref, wd_ref, o_ref):
    @pl.when(pl.program_id(0) < nused_ref[0])
    def _():
        x = x_ref[...]
        hb = jax.nn.silu(_dot(x, wg_ref[0])) * _dot(x, wu_ref[0])
        o_ref[...] = _dot(hb.astype(BF16), wd_ref[0])

    @pl.when(pl.program_id(0) >= nused_ref[0])
    def _():
        o_ref[...] = jnp.zeros_like(o_ref)


def _moe_experts(xg, blk_exp, n_used, wts):
    n_rows, d = xg.shape
    n_blocks = n_rows // MOE_ROWS
    return pl.pallas_call(
        _expert_kernel,
        grid_spec=pltpu.PrefetchScalarGridSpec(
            num_scalar_prefetch=2,
            grid=(n_blocks,),
            in_specs=[pl.BlockSpec((MOE_ROWS, d), lambda i, nu, be: (i, 0)),
                      pl.BlockSpec((1, d, D_EXPERT), lambda i, nu, be: (be[i], 0, 0)),
                      pl.BlockSpec((1, d, D_EXPERT), lambda i, nu, be: (be[i], 0, 0)),
                      pl.BlockSpec((1, D_EXPERT, d), lambda i, nu, be: (be[i], 0, 0))],
            out_specs=pl.BlockSpec((MOE_ROWS, d), lambda i, nu, be: (i, 0)),
        ),
        out_shape=jax.ShapeDtypeStruct((n_rows, d), F32),
        compiler_params=_cparams("arbitrary"),
        name="moe_experts",
    )(n_used, blk_exp, xg, wts['exp_w_gate'], wts['exp_w_up'], wts['exp_w_down'])


def _combine_kernel(pos_ref, eo_hbm, h_ref, w_ref, g_ref, b_ref, o_ref, buf0, buf1, sem, *, bm):
    def issue(r, carry):
        pltpu.make_async_copy(eo_hbm.at[pl.ds(pos_ref[0, 0, 2 * r], 1)], buf0.at[pl.ds(r, 1)], sem).start()
        pltpu.make_async_copy(eo_hbm.at[pl.ds(pos_ref[0, 0, 2 * r + 1], 1)], buf1.at[pl.ds(r, 1)], sem).start()
        return carry

    lax.fori_loop(0, bm, issue, 0, unroll=8)

    def wait(r, carry):
        pltpu.make_async_copy(eo_hbm.at[pl.ds(0, 1)], buf0.at[pl.ds(r, 1)], sem).wait()
        pltpu.make_async_copy(eo_hbm.at[pl.ds(0, 1)], buf1.at[pl.ds(r, 1)], sem).wait()
        return carry

    lax.fori_loop(0, bm, wait, 0, unroll=8)
    w = w_ref[...]
    moe = w[:, 0:1] * buf0[...] + w[:, 1:2] * buf1[...]
    o_ref[...] = _layernorm(ALPHA * h_ref[...] + moe, g_ref[...], b_ref[...])


def _moe_combine(eo, pos, w, h1, wts, bm):
    n, d = h1.shape
    row = pl.BlockSpec((bm, d), lambda i: (i, 0))
    vec = pl.BlockSpec((1, d), lambda i: (0, 0))
    return pl.pallas_call(
        functools.partial(_combine_kernel, bm=bm),
        grid=(n // bm,),
        in_specs=[pl.BlockSpec((1, 1, 2 * bm), lambda i: (i, 0, 0), memory_space=pltpu.SMEM),
                  pl.BlockSpec(memory_space=pl.ANY),
                  row, pl.BlockSpec((bm, 2), lambda i: (i, 0)), vec, vec],
        out_specs=row,
        out_shape=jax.ShapeDtypeStruct((n, d), F32),
        scratch_shapes=[pltpu.VMEM((bm, d), F32), pltpu.VMEM((bm, d), F32), pltpu.SemaphoreType.DMA(())],
        compiler_params=_cparams("arbitrary"),
        name="moe_combine",
    )(pos.reshape(n // bm, 1, 2 * bm), eo, h1, w, wts['ln2_g'], wts['ln2_b'])


def _prepare_weights(p):
    l = 0
    w_in = p['w_in'][l]
    c0 = S5_WIDTH
    c1 = c0 + 3 * RW_WIDTH + 2 * RW_DECAY_LORA + 2 * RW_ICLR_LORA + RW_GATE_LORA
    gpad = RW_GATE_PAD - RW_GATE_LORA
    wts = {}
    wts['w_u'] = w_in[:, :c0].astype(BF16)
    wts['w_rw'] = jnp.pad(w_in[:, c0:c1], ((0, 0), (0, gpad))).astype(BF16)
    wts['w_gate'] = w_in[:, c1:].astype(BF16)
    wts['mu'] = jnp.pad(p['shift_mu'][l], ((0, 0), (0, gpad)))
    z = jnp.zeros((RW_DECAY_LORA, RW_WIDTH), F32)
    wts['w2'] = jnp.block([[p['rw_w2'][l, 0], z], [z, p['rw_w2'][l, 1]]]).astype(BF16)
    wts['a2'] = jnp.block([[p['rw_a2'][l, 0], z], [z, p['rw_a2'][l, 1]]]).astype(BF16)
    wts['g2'] = jnp.pad(p['rw_g2'][l], ((0, gpad), (0, 0))).astype(BF16)
    wts['w0'] = p['rw_w0'][l]
    wts['a0'] = p['rw_a0'][l]
    wts['k_k'] = p['rw_k_k'][l].reshape(1, -1)
    wts['k_a'] = p['rw_k_a'][l].reshape(1, -1)
    wts['r_k'] = p['rw_r_k'][l].reshape(1, -1)
    wts['lnx_g'] = p['rw_lnx_g'][l].reshape(1, -1)
    wts['lnx_b'] = p['rw_lnx_b'][l].reshape(1, -1)
    head = jnp.arange(RW_WIDTH) // RW_HEAD
    e = (head[:, None] == jnp.arange(RW_HEADS)[None, :]).astype(BF16)
    wts['head_e'] = e
    wts['head_et'] = e.T
    wts['s5'] = _s5_matrices(p['s5_B_re'][l], p['s5_B_im'][l], p['s5_A_re'][l], p['s5_A_im'][l],
                             p['s5_log_dt'][l], p['s5_C_re'][l], p['s5_C_im'][l])
    wts['s5_D'] = p['s5_D'][l]
    wts['glu_w'] = p['s5_glu_w'][l].astype(BF16)
    wts['glu_b'] = p['s5_glu_b'][l]
    wts['proj_s5'] = p['proj_s5'][l].astype(BF16)
    wts['proj_rwkv'] = p['proj_rwkv'][l].astype(BF16)
    wts['gate_b'] = p['gate_b'][l]
    wts['w_out'] = p['w_out'][l].astype(BF16)
    wts['ln1_g'] = p['ln1_g'][l].reshape(1, -1)
    wts['ln1_b'] = p['ln1_b'][l].reshape(1, -1)
    router = jnp.concatenate([p['router_coarse'][l], p['router_fine'][l]], axis=1)
    router = jnp.pad(router, ((0, 0), (0, ROUTER_PAD - router.shape[1])))
    wts['router_hi'], wts['router_lo'] = _split(router)
    wts['router_coarse_b'] = p['router_coarse_b'][l]
    wts['router_fine_b'] = p['router_fine_b'][l]
    wts['exp_w_gate'] = p['exp_w_gate'][l].astype(BF16)
    wts['exp_w_up'] = p['exp_w_up'][l].astype(BF16)
    wts['exp_w_down'] = p['exp_w_down'][l].astype(BF16)
    wts['ln2_g'] = p['ln2_g'][l].reshape(1, -1)
    wts['ln2_b'] = p['ln2_b'][l].reshape(1, -1)
    return wts


def _encode(x, p, wts):
    bsz, t, d = x.shape
    t_valid = t + N_META
    t_pad = -(-t_valid // SEQ_ALIGN) * SEQ_ALIGN
    n = bsz * t_pad
    meta = jnp.broadcast_to(p['meta'][None].astype(x.dtype), (bsz, N_META, d))
    h_in = jnp.concatenate([meta, x, jnp.zeros((bsz, t_pad - t_valid, d), x.dtype)], axis=1)
    h0, h0b = _ln_in(h_in, p['ln_in_g'], p['ln_in_b'], t_valid)
    h0 = h0.reshape(n, d)
    h0b = h0b.reshape(n, d)
    bm = _row_block(t_pad, 1024)
    u = _mm(h0b, wts['w_u'], bm, S5_WIDTH, F32, "proj_s5_in")
    rw = _mm(h0b, wts['w_rw'], bm, RW_COLS // 4, F32, "proj_rwkv_in")
    y_ssm = _s5_ssm(u, wts['s5'], bsz, t_pad)
    s5_out = _s5_post(y_ssm, u, wts['s5_D'], wts['glu_w'], wts['glu_b'], bm)
    r, kk, v, g, bv, lw, kd, bb = _rwkv_prep(rw.reshape(bsz, t_pad, RW_COLS), wts, t_valid)
    yf, yb = _rwkv_scan(r, kk, v, lw, kd, bb)
    rw_out = _rwkv_post(yf.reshape(n, -1), yb.reshape(n, -1), bv.reshape(n, -1), g.reshape(n, -1), wts, bm)
    merged = _merge(h0b, s5_out, rw_out, wts, bm)
    h1, logits = _out_proj(merged, h0, wts, _row_block(t_pad, 512))
    valid = (jnp.arange(n, dtype=jnp.int32) % t_pad) < t_valid
    row_tok, pos, w, blk_exp, n_used, n_blocks = _route(logits, wts, valid)
    xg = _moe_gather(h1, row_tok, n_used, n_blocks)
    eo = _moe_experts(xg, blk_exp, n_used, wts)
    h2 = _moe_combine(eo, pos, w, h1, wts, _row_block(t_pad, 256))
    return h2.reshape(bsz, t_pad, d)[:, N_META:t_valid]


def kernel(x_prompt, x_sample, meta, ln_in_g, ln_in_b, w_in, shift_mu, s5_B_re, s5_B_im, s5_A_re, s5_A_im, s5_log_dt, s5_C_re, s5_C_im, s5_D, s5_glu_w, s5_glu_b, rw_w0, rw_w2, rw_a0, rw_a2, rw_g2, rw_k_k, rw_k_a, rw_r_k, rw_lnx_g, rw_lnx_b, proj_s5, proj_rwkv, gate_b, w_out, ln1_g, ln1_b, router_coarse, router_coarse_b, router_fine, router_fine_b, exp_w_gate, exp_w_up, exp_w_down, ln2_g, ln2_b):
    p = {
        'meta': meta, 'ln_in_g': ln_in_g, 'ln_in_b': ln_in_b, 'w_in': w_in, 'shift_mu': shift_mu,
        's5_B_re': s5_B_re, 's5_B_im': s5_B_im, 's5_A_re': s5_A_re, 's5_A_im': s5_A_im,
        's5_log_dt': s5_log_dt, 's5_C_re': s5_C_re, 's5_C_im': s5_C_im, 's5_D': s5_D,
        's5_glu_w': s5_glu_w, 's5_glu_b': s5_glu_b,
        'rw_w0': rw_w0, 'rw_w2': rw_w2, 'rw_a0': rw_a0, 'rw_a2': rw_a2, 'rw_g2': rw_g2,
        'rw_k_k': rw_k_k, 'rw_k_a': rw_k_a, 'rw_r_k': rw_r_k, 'rw_lnx_g': rw_lnx_g, 'rw_lnx_b': rw_lnx_b,
        'proj_s5': proj_s5, 'proj_rwkv': proj_rwkv, 'gate_b': gate_b, 'w_out': w_out,
        'ln1_g': ln1_g, 'ln1_b': ln1_b,
        'router_coarse': router_coarse, 'router_coarse_b': router_coarse_b,
        'router_fine': router_fine, 'router_fine_b': router_fine_b,
        'exp_w_gate': exp_w_gate, 'exp_w_up': exp_w_up, 'exp_w_down': exp_w_down,
        'ln2_g': ln2_g, 'ln2_b': ln2_b,
    }
    wts = _prepare_weights(p)
    return (_encode(x_prompt, p, wts), _encode(x_sample, p, wts))
```

```python
import functools
import math

import jax
import jax.numpy as jnp
from jax import lax
from jax.experimental import pallas as pl
from jax.experimental.pallas import tpu as pltpu

F32 = jnp.float32
BF16 = jnp.bfloat16

D_MODEL = 2048
N_META = 16
S5_WIDTH = 1024
S5_GROUP = 16
S5_GROUPS = 64
S5_STATE = 64
S5_CHUNK = 16
RW_WIDTH = 1024
RW_HEAD = 64
RW_HEADS = 16
RW_DECAY_LORA = 64
RW_ICLR_LORA = 64
RW_GATE_LORA = 160
RW_GATE_PAD = 256
RW_COLS = 3 * RW_WIDTH + 2 * RW_DECAY_LORA + 2 * RW_ICLR_LORA + RW_GATE_PAD
RW_CHUNK = 64
RW_HEADS_PER_STEP = 2
MOE_GROUPS = 4
EXPERTS_PER_GROUP = 8
N_EXPERTS = 32
D_EXPERT = 512
MOE_ROWS = 256
ROUTER_PAD = 128
DEPTH = 1
ALPHA = (2 * DEPTH) ** 0.25
LN_EPS = 1e-5
GN_EPS = 64e-5
SEQ_ALIGN = 64
SEQ_FRONT = 64
VMEM_LIMIT = 56 * 1024 * 1024


def _cparams(*sem):
    return pltpu.CompilerParams(dimension_semantics=sem, vmem_limit_bytes=VMEM_LIMIT)


def _row_block(t_pad, cap, mult=8):
    best = mult
    for d in range(mult, cap + 1, mult):
        if t_pad % d == 0:
            best = d
    return best


def _dot(a, b):
    return jnp.dot(a, b, preferred_element_type=F32)


def _dot_nt(a, b):
    return lax.dot_general(a, b, (((1,), (1,)), ((), ())), preferred_element_type=F32)


def _dot_tn(a, b):
    return lax.dot_general(a, b, (((0,), (0,)), ((), ())), preferred_element_type=F32)


def _split(x):
    hi = x.astype(BF16)
    lo = (x - hi.astype(F32)).astype(BF16)
    return hi, lo


def _layernorm(x, g, b):
    mu = jnp.mean(x, axis=-1, keepdims=True)
    xc = x - mu
    var = jnp.mean(xc * xc, axis=-1, keepdims=True)
    return xc * lax.rsqrt(var + LN_EPS) * g + b


def _ln_in_kernel(x_ref, m_ref, g_ref, b_ref, of_ref, ob_ref, *, t):
    j = pl.program_id(1)
    blk = SEQ_FRONT

    @pl.when(j == 0)
    def _():
        ym = _layernorm(m_ref[...], g_ref[...], b_ref[...])
        y = jnp.concatenate([jnp.zeros((blk - N_META, ym.shape[1]), F32), ym], axis=0)
        of_ref[0] = y
        ob_ref[0] = y.astype(BF16)

    @pl.when(j > 0)
    def _():
        y = _layernorm(x_ref[0], g_ref[...], b_ref[...])
        pos = (j - 1) * blk + lax.broadcasted_iota(jnp.int32, (blk, 1), 0)
        y = jnp.where(pos < t, y, 0.0)
        of_ref[0] = y
        ob_ref[0] = y.astype(BF16)


def _ln_in(x, meta, g, b, t_pad):
    bsz, t, d = x.shape
    blk = SEQ_FRONT
    row = pl.BlockSpec((1, blk, d), lambda i, j: (i, j, 0))
    vec = pl.BlockSpec((1, d), lambda i, j: (0, 0))
    return pl.pallas_call(
        functools.partial(_ln_in_kernel, t=t),
        grid=(bsz, t_pad // blk),
        in_specs=[pl.BlockSpec((1, blk, d), lambda i, j: (i, jnp.maximum(j - 1, 0), 0)),
                  pl.BlockSpec((N_META, d), lambda i, j: (0, 0)), vec, vec],
        out_specs=[row, row],
        out_shape=[jax.ShapeDtypeStruct((bsz, t_pad, d), F32), jax.ShapeDtypeStruct((bsz, t_pad, d), BF16)],
        compiler_params=_cparams("parallel", "parallel"),
        name="ln_in",
    )(x, meta, g.reshape(1, d), b.reshape(1, d))


def _mm_kernel(x_ref, w_ref, o_ref):
    o_ref[...] = _dot(x_ref[...], w_ref[...]).astype(o_ref.dtype)


def _mm(x, w, bm, bn, out_dtype, name):
    n, k = x.shape
    m = w.shape[1]
    return pl.pallas_call(
        _mm_kernel,
        grid=(m // bn, n // bm),
        in_specs=[pl.BlockSpec((bm, k), lambda j, i: (i, 0)),
                  pl.BlockSpec((k, bn), lambda j, i: (0, j))],
        out_specs=pl.BlockSpec((bm, bn), lambda j, i: (i, j)),
        out_shape=jax.ShapeDtypeStruct((n, m), out_dtype),
        compiler_params=_cparams("parallel", "parallel"),
        name=name,
    )(x, w)


def _s5_matrices(b_re, b_im, a_re, a_im, log_dt, c_re, c_im):
    L = S5_CHUNK
    dt = jnp.exp(log_dt)[..., None]
    mag = jnp.exp(a_re * dt)
    abr = mag * jnp.cos(a_im * dt)
    abi = mag * jnp.sin(a_im * dt)
    den = a_re * a_re + a_im * a_im
    nr = abr - 1.0
    cr = (nr * a_re + abi * a_im) / den
    ci = (abi * a_re - nr * a_im) / den
    bbr = cr[..., None] * b_re - ci[..., None] * b_im
    bbi = cr[..., None] * b_im + ci[..., None] * b_re
    tau = jnp.arange(L + 1, dtype=F32)[:, None, None, None]
    pmag = jnp.exp(tau * a_re * dt)
    pr = pmag * jnp.cos(tau * a_im * dt)
    pi = pmag * jnp.sin(tau * a_im * dt)
    wr = pr[..., None] * bbr - pi[..., None] * bbi
    wi = pr[..., None] * bbi + pi[..., None] * bbr
    kern = (jnp.einsum('zgop,tzgpi->tzgoi', c_re, wr)
            - jnp.einsum('zgop,tzgpi->tzgoi', c_im, wi))
    s = jnp.arange(L)[:, None]
    t = jnp.arange(L)[None, :]
    lag = t - s
    kf = jnp.where((lag >= 0)[..., None, None, None], kern[jnp.clip(lag, 0, L), 0], 0.0)
    kb = jnp.where((lag <= 0)[..., None, None, None], kern[jnp.clip(-lag, 0, L), 1], 0.0)
    toep = (kf + kb).transpose(2, 0, 4, 1, 3).reshape(S5_GROUPS, L * S5_GROUP, L * S5_GROUP)
    wf_r, wf_i = wr[::-1][1:, 0], wi[::-1][1:, 0]
    wb_r, wb_i = wr[:L, 1], wi[:L, 1]
    bmat = jnp.concatenate([wf_r, wb_r, wf_i, wb_i], axis=2)
    bmat = bmat.transpose(1, 0, 3, 2).reshape(S5_GROUPS, L * S5_GROUP, 4 * S5_STATE)
    pf_r, pf_i = pr[1:, 0], pi[1:, 0]
    pb_r, pb_i = pr[::-1][:L, 1], pi[::-1][:L, 1]
    c0r, c0i, c1r, c1i = c_re[0], c_im[0], c_re[1], c_im[1]

    def cpow(cre, cim, p_r, p_i):
        re = cre[None] * p_r[:, :, None, :] - cim[None] * p_i[:, :, None, :]
        im = cre[None] * p_i[:, :, None, :] + cim[None] * p_r[:, :, None, :]
        return re, -im

    f_re, f_im = cpow(c0r, c0i, pf_r, pf_i)
    g_re, g_im = cpow(c1r, c1i, pb_r, pb_i)
    cmat = jnp.concatenate([f_re, g_re, f_im, g_im], axis=3)
    cmat = cmat.transpose(1, 3, 0, 2).reshape(S5_GROUPS, 4 * S5_STATE, L * S5_GROUP)
    lam_re = jnp.concatenate([pr[L, 0], pr[L, 1]], axis=-1)[:, None, :]
    lam_im = jnp.concatenate([pi[L, 0], pi[L, 1]], axis=-1)[:, None, :]
    return bmat.astype(BF16), toep.astype(BF16), cmat.astype(BF16), lam_re, lam_im


def _s5_kernel(u_ref, bmat_ref, toep_ref, cmat_ref, lre_ref, lim_ref, y_ref,
               s_scr, xf_scr, xb_scr, *, n_chunks, bp):
    u = u_ref[0].astype(BF16)
    s_scr[...] = _dot(u, bmat_ref[0])
    a_re = lre_ref[0]
    a_im = lim_ref[0]
    is_fwd = lax.broadcasted_iota(jnp.int32, (bp, 128), 1) < S5_STATE

    def step(j, carry):
        xr, xi = carry
        rf = pl.multiple_of(j * bp, bp)
        rb = pl.multiple_of((n_chunks - 1 - j) * bp, bp)
        x_in = jnp.concatenate([xr, xi], axis=1)
        xf_scr[pl.ds(rf, bp), :] = x_in
        xb_scr[pl.ds(rb, bp), :] = x_in
        sf = s_scr[pl.ds(rf, bp), :]
        sb = s_scr[pl.ds(rb, bp), :]
        s_re = jnp.where(is_fwd, sf[:, :128], sb[:, :128])
        s_im = jnp.where(is_fwd, sf[:, 128:], sb[:, 128:])
        return (a_re * xr - a_im * xi + s_re, a_re * xi + a_im * xr + s_im)

    zero = jnp.zeros((bp, 128), F32)
    lax.fori_loop(0, n_chunks, step, (zero, zero))
    lane = lax.broadcasted_iota(jnp.int32, (1, 256), 1)
    fwd_lane = (lane % 128) < S5_STATE
    x_in = jnp.where(fwd_lane, xf_scr[...], xb_scr[...]).astype(BF16)
    y_ref[0] = _dot(u, toep_ref[0]) + _dot(x_in, cmat_ref[0])


def _s5_ssm(u, mats, bsz, t_pad):
    bmat, toep, cmat, lam_re, lam_im = mats
    L = S5_CHUNK
    n_chunks = t_pad // L
    bp = -(-bsz // 8) * 8
    ug = u.reshape(bsz, n_chunks, L, S5_GROUPS, S5_GROUP).transpose(3, 1, 0, 2, 4)
    ug = jnp.pad(ug, ((0, 0), (0, 0), (0, bp - bsz), (0, 0), (0, 0)))
    rows = n_chunks * bp
    ug = ug.reshape(S5_GROUPS, rows, L * S5_GROUP)
    blk = pl.BlockSpec((1, rows, 256), lambda g: (g, 0, 0))
    mat = pl.BlockSpec((1, 256, 256), lambda g: (g, 0, 0))
    vec = pl.BlockSpec((1, 1, 128), lambda g: (g, 0, 0))
    yg = pl.pallas_call(
        functools.partial(_s5_kernel, n_chunks=n_chunks, bp=bp),
        grid=(S5_GROUPS,),
        in_specs=[blk, mat, mat, mat, vec, vec],
        out_specs=blk,
        out_shape=jax.ShapeDtypeStruct(ug.shape, F32),
        scratch_shapes=[pltpu.VMEM((rows, 256), F32)] * 3,
        compiler_params=_cparams("parallel"),
        name="s5_ssm",
    )(ug, bmat, toep, cmat, lam_re, lam_im)
    yg = yg.reshape(S5_GROUPS, n_chunks, bp, L, S5_GROUP)[:, :, :bsz]
    return yg.transpose(2, 1, 3, 0, 4).reshape(bsz * t_pad, S5_WIDTH)


def _s5_post_kernel(y_ref, u_ref, d_ref, w_ref, b_ref, o_ref):
    y = y_ref[...] + u_ref[...] * d_ref[...]
    act = y * (0.5 * (1.0 + jnp.tanh(math.sqrt(2.0 / math.pi) * (y + 0.044715 * (y * y * y)))))
    z = _dot(act.astype(BF16), w_ref[...]) + b_ref[...]
    o_ref[...] = (act * jax.nn.sigmoid(z)).astype(o_ref.dtype)


def _s5_post(y, u, d_skip, glu_w, glu_b, bm):
    n = y.shape[0]
    row = pl.BlockSpec((bm, S5_WIDTH), lambda i: (i, 0))
    vec = pl.BlockSpec((1, S5_WIDTH), lambda i: (0, 0))
    return pl.pallas_call(
        _s5_post_kernel,
        grid=(n // bm,),
        in_specs=[row, row, vec, pl.BlockSpec((S5_WIDTH, S5_WIDTH), lambda i: (0, 0)), vec],
        out_specs=row,
        out_shape=jax.ShapeDtypeStruct((n, S5_WIDTH), BF16),
        compiler_params=_cparams("parallel"),
        name="s5_post",
    )(y, u, d_skip.reshape(1, -1), glu_w, glu_b.reshape(1, -1))


def _head_sum(x, e_ref, et_ref):
    hi, lo = _split(x)
    s = _dot(hi, e_ref[...]) + _dot(lo, e_ref[...])
    shi, slo = _split(s)
    return _dot(shi, et_ref[...]) + _dot(slo, et_ref[...])


def _rwkv_prep_kernel(cur_ref, prev_ref, next_ref, mu_ref, w2_ref, a2_ref, g2_ref, w0_ref, a0_ref,
                      kk_ref, ka_ref, rk_ref, e_ref, et_ref,
                      r_o, kk_o, v_o, g_o, bv_o, lw_o, kd_o, bb_o, *, t_lo, t_hi, bm):
    j = pl.program_id(1)
    p = cur_ref[0]
    row = lax.broadcasted_iota(jnp.int32, (bm, 1), 0)
    prev_row = jnp.where(j > 0, prev_ref[0, 7:8, :], 0.0)
    next_row = jnp.where(j < pl.num_programs(1) - 1, next_ref[0, 0:1, :], 0.0)
    prev = jnp.where(row == 0, prev_row, pltpu.roll(p, 1, 0))
    nxt = jnp.where(row == bm - 1, next_row, pltpu.roll(p, bm - 1, 0))
    xs = p + mu_ref[0:1, :] * (prev - p) + mu_ref[1:2, :] * (nxt - p)
    w = RW_WIDTH
    r = xs[:, 0:w]
    k = xs[:, w:2 * w]
    v = xs[:, 2 * w:3 * w]
    lw = xs[:, 3 * w:3 * w + 128]
    la = xs[:, 3 * w + 128:3 * w + 256]
    lg = xs[:, 3 * w + 256:]
    w_log = _dot(jnp.tanh(lw).astype(BF16), w2_ref[...])
    a_lin = _dot(la.astype(BF16), a2_ref[...])
    g = _dot(jax.nn.sigmoid(lg).astype(BF16), g2_ref[...])
    kk = k * kk_ref[...]
    n2 = _head_sum(kk * kk, e_ref, et_ref)
    kk = kk / jnp.maximum(jnp.sqrt(n2), 1e-12)
    pos = j * bm + row
    valid = (pos >= t_lo) & (pos < t_hi)
    v = jnp.where(valid, v, 0.0)
    kd_sum = jnp.zeros_like(k)
    for z in range(2):
        wl = w_log[:, z * w:(z + 1) * w] + w0_ref[z:z + 1, :]
        lw_o[z, 0] = -math.exp(-0.5) * jax.nn.sigmoid(wl)
        a = jax.nn.sigmoid(a_lin[:, z * w:(z + 1) * w] + a0_ref[z:z + 1, :])
        kd = k * (1.0 + (a - 1.0) * ka_ref[...])
        kd_o[z, 0] = kd.astype(BF16)
        bb_o[z, 0] = (kk * a).astype(BF16)
        kd_sum = kd_sum + kd
    bonus = _head_sum(r * kd_sum * rk_ref[...], e_ref, et_ref)
    r_o[0] = r.astype(BF16)
    kk_o[0] = kk.astype(BF16)
    v_o[0] = v.astype(BF16)
    g_o[0] = g.astype(BF16)
    bv_o[0] = (bonus * v).astype(BF16)


def _rwkv_prep(rw3, wts, t_lo, t_hi):
    bsz, t_pad, _ = rw3.shape
    bm = _row_block(t_pad, 320, 64)
    nb8 = bm // 8
    last8 = t_pad // 8 - 1
    w = RW_WIDTH
    cur = pl.BlockSpec((1, bm, RW_COLS), lambda b, j: (b, j, 0))
    prev = pl.BlockSpec((1, 8, RW_COLS), lambda b, j: (b, jnp.maximum(j * nb8 - 1, 0), 0))
    nxt = pl.BlockSpec((1, 8, RW_COLS), lambda b, j: (b, jnp.minimum((j + 1) * nb8, last8), 0))

    def full(a):
        return pl.BlockSpec(a.shape, lambda b, j: (0,) * a.ndim)

    shared = pl.BlockSpec((1, bm, w), lambda b, j: (b, j, 0))
    per_dir = pl.BlockSpec((2, 1, bm, w), lambda b, j: (0, b, j, 0))
    consts = [wts['mu'], wts['w2'], wts['a2'], wts['g2'], wts['w0'], wts['a0'],
              wts['k_k'], wts['k_a'], wts['r_k'], wts['head_e'], wts['head_et']]
    sds = jax.ShapeDtypeStruct
    return pl.pallas_call(
        functools.partial(_rwkv_prep_kernel, t_lo=t_lo, t_hi=t_hi, bm=bm),
        grid=(bsz, t_pad // bm),
        in_specs=[cur, prev, nxt] + [full(a) for a in consts],
        out_specs=[shared] * 5 + [per_dir] * 3,
        out_shape=[sds((bsz, t_pad, w), BF16)] * 5
        + [sds((2, bsz, t_pad, w), F32), sds((2, bsz, t_pad, w), BF16), sds((2, bsz, t_pad, w), BF16)],
        compiler_params=_cparams("parallel", "parallel"),
        name="rwkv_prep",
    )(rw3, rw3, rw3, *consts)


def _rwkv_chunk(fwd, r_ref, kk_ref, v_ref, lw_ref, kd_ref, bb_ref, y_ref, st_ref, z):
    L = RW_CHUNK
    hd = RW_HEAD
    gw = RW_HEADS_PER_STEP * hd
    n_groups = RW_WIDTH // gw
    row = lax.broadcasted_iota(jnp.int32, (L, L), 0)
    col = lax.broadcasted_iota(jnp.int32, (L, L), 1)
    tri = jnp.where((col <= row) if fwd else (col >= row), 1.0, 0.0).astype(BF16)
    grow = lax.broadcasted_iota(jnp.int32, (L, gw), 0)
    gcol = lax.broadcasted_iota(jnp.int32, (L, gw), 1) % L
    incl = (gcol <= grow) if fwd else (gcol >= grow)
    strict = (gcol < grow) if fwd else (gcol > grow)
    bd_mask = jnp.where(lax.broadcasted_iota(jnp.int32, (gw, gw), 0) // hd
                        == lax.broadcasted_iota(jnp.int32, (gw, gw), 1) // hd, 1.0, 0.0).astype(BF16)

    def bd(x):
        return jnp.concatenate([x] * RW_HEADS_PER_STEP, axis=0) * bd_mask

    def stack(x):
        return jnp.concatenate([x[:, h * hd:(h + 1) * hd] for h in range(RW_HEADS_PER_STEP)], axis=0)

    lw = lw_ref[0, 0]
    lw_hi, lw_lo = _split(lw)
    c = _dot(tri, lw_hi) + _dot(tri, lw_lo)
    e = c - lw
    c_tot = c[L - 1:L, :] if fwd else c[0:1, :]
    r = r_ref[0].astype(F32)
    kk = kk_ref[0].astype(F32)
    kd = kd_ref[0, 0].astype(F32)
    bb = bb_ref[0, 0].astype(F32)
    v = v_ref[0]
    q1 = (kk * jnp.exp(e)).astype(BF16)
    q2 = (r * jnp.exp(c)).astype(BF16)
    inv = jnp.exp(-c)
    k1 = (kd * inv).astype(BF16)
    k2 = (bb * inv).astype(BF16)
    rest = jnp.exp(c_tot - c)
    k1p = (kd * rest).astype(BF16)
    k2p = (bb * rest).astype(BF16)
    dec_tot = jnp.exp(c_tot)
    def group_chain(g):
        sl = slice(g * gw, (g + 1) * gw)
        s0 = st_ref[z, g]
        s0_hi, s0_lo = _split(s0)
        lhs = jnp.concatenate([q1[:, sl], q2[:, sl]], axis=0)
        rhs = jnp.concatenate([bd(k1[:, sl]), bd(k2[:, sl]), bd(s0_hi), bd(s0_lo)], axis=0)
        m1 = _dot_nt(lhs, rhs)
        yield
        a_kd = jnp.where(strict, m1[:L, 0:gw], 0.0)
        a_b = jnp.where(strict, m1[:L, gw:2 * gw], 0.0)
        q1s = m1[:L, 2 * gw:3 * gw] + m1[:L, 3 * gw:]
        b_kd = jnp.where(incl, m1[L:, 0:gw], 0.0)
        b_b = jnp.where(incl, m1[L:, gw:2 * gw], 0.0)
        q2s = m1[L:, 2 * gw:3 * gw] + m1[L:, 3 * gw:]
        vg = v[:, sl]
        v_bd = bd(vg)
        x = q1s + _dot(a_kd.astype(BF16), v_bd)
        yield
        m = -a_b
        levels = L.bit_length() - 1
        for lvl in range(levels):
            mb = m.astype(BF16)
            if lvl < levels - 1:
                rr = _dot(mb, jnp.concatenate([bd(x.astype(BF16)), bd(mb)], axis=1))
                x = x + rr[:, :gw]
                m = rr[:, gw:]
            else:
                x = x + _dot(mb, bd(x.astype(BF16)))
            yield
        ub = x.astype(BF16)
        y = q2s + _dot(jnp.concatenate([b_kd, -b_b], axis=1).astype(BF16),
                       jnp.concatenate([v_bd, bd(ub)], axis=0))
        y_ref[0, :, sl] = y
        yield
        st_ref[z, g] = s0 * dec_tot[:, sl] + _dot_tn(
            jnp.concatenate([stack(vg), stack(ub)], axis=0),
            jnp.concatenate([bd(k1p[:, sl]), -bd(k2p[:, sl])], axis=0))

    return [group_chain(g) for g in range(n_groups)]


def _rwkv_scan_kernel(rf, kkf, vf, lwf, kdf, bbf, rb, kkb, vb, lwb, kdb, bbb, yf_ref, yb_ref, st_ref):
    @pl.when(pl.program_id(1) == 0)
    def _():
        st_ref[...] = jnp.zeros_like(st_ref)

    chains = (_rwkv_chunk(True, rf, kkf, vf, lwf, kdf, bbf, yf_ref, st_ref, 0)
              + _rwkv_chunk(False, rb, kkb, vb, lwb, kdb, bbb, yb_ref, st_ref, 1))
    while chains:
        alive = []
        for chain in chains:
            try:
                next(chain)
                alive.append(chain)
            except StopIteration:
                pass
        chains = alive


def _rwkv_scan(r, kk, v, lw, kd, bb):
    bsz, t_pad, w = r.shape
    L = RW_CHUNK
    nc = t_pad // L
    gw = RW_HEADS_PER_STEP * RW_HEAD
    grid = (bsz, nc)

    def shared(fwd):
        if fwd:
            return pl.BlockSpec((1, L, w), lambda b, j: (b, j, 0))
        return pl.BlockSpec((1, L, w), lambda b, j: (b, nc - 1 - j, 0))

    def per_dir(fwd):
        if fwd:
            return pl.BlockSpec((1, 1, L, w), lambda b, j: (0, b, j, 0))
        return pl.BlockSpec((1, 1, L, w), lambda b, j: (1, b, nc - 1 - j, 0))

    in_specs = []
    for fwd in (True, False):
        in_specs += [shared(fwd), shared(fwd), shared(fwd), per_dir(fwd), per_dir(fwd), per_dir(fwd)]
    return pl.pallas_call(
        _rwkv_scan_kernel,
        grid=grid,
        in_specs=in_specs,
        out_specs=[shared(True), shared(False)],
        out_shape=[jax.ShapeDtypeStruct((bsz, t_pad, w), F32)] * 2,
        scratch_shapes=[pltpu.VMEM((2, w // gw, RW_HEAD, gw), F32)],
        compiler_params=_cparams("parallel", "arbitrary"),
        name="rwkv_scan",
    )(r, kk, v, lw, kd, bb, r, kk, v, lw, kd, bb)


def _rwkv_post_kernel(yf_ref, yb_ref, bv_ref, g_ref, lg_ref, lb_ref, e_ref, et_ref, o_ref):
    y = yf_ref[...] + yb_ref[...]
    mean = _head_sum(y, e_ref, et_ref) * (1.0 / RW_HEAD)
    yc = y - mean
    var = _head_sum(yc * yc, e_ref, et_ref) * (1.0 / RW_HEAD)
    y = yc * lax.rsqrt(var + GN_EPS) * lg_ref[...] + lb_ref[...]
    o_ref[...] = ((y + bv_ref[...].astype(F32)) * g_ref[...].astype(F32)).astype(o_ref.dtype)


def _rwkv_post(yf, yb, bv, g, wts, bm):
    n, w = yf.shape
    row = pl.BlockSpec((bm, w), lambda i: (i, 0))
    vec = pl.BlockSpec((1, w), lambda i: (0, 0))
    e, et = wts['head_e'], wts['head_et']
    return pl.pallas_call(
        _rwkv_post_kernel,
        grid=(n // bm,),
        in_specs=[row, row, row, row, vec, vec,
                  pl.BlockSpec(e.shape, lambda i: (0, 0)), pl.BlockSpec(et.shape, lambda i: (0, 0))],
        out_specs=row,
        out_shape=jax.ShapeDtypeStruct((n, w), BF16),
        compiler_params=_cparams("parallel"),
        name="rwkv_post",
    )(yf, yb, bv, g, wts['lnx_g'], wts['lnx_b'], e, et)


def _merge_kernel(h_ref, s5_ref, rw_ref, wg0_ref, wg1_ref, gb_ref, p0_ref, p1_ref, o_ref):
    h = h_ref[...]
    g0 = jax.nn.sigmoid(_dot(h, wg0_ref[...]) + gb_ref[0:1, :])
    g1 = jax.nn.sigmoid(_dot(h, wg1_ref[...]) + gb_ref[1:2, :])
    merged = g0 * _dot(s5_ref[...], p0_ref[...]) + g1 * _dot(rw_ref[...], p1_ref[...])
    o_ref[...] = merged.astype(o_ref.dtype)


def _merge(h0b, s5_out, rw_out, wts, bm):
    n, d = h0b.shape
    bn = 1024
    nj = d // bn
    return pl.pallas_call(
        _merge_kernel,
        grid=(nj, n // bm),
        in_specs=[pl.BlockSpec((bm, d), lambda j, i: (i, 0)),
                  pl.BlockSpec((bm, S5_WIDTH), lambda j, i: (i, 0)),
                  pl.BlockSpec((bm, RW_WIDTH), lambda j, i: (i, 0)),
                  pl.BlockSpec((d, bn), lambda j, i: (0, j)),
                  pl.BlockSpec((d, bn), lambda j, i: (0, nj + j)),
                  pl.BlockSpec((2, bn), lambda j, i: (0, j)),
                  pl.BlockSpec((S5_WIDTH, bn), lambda j, i: (0, j)),
                  pl.BlockSpec((RW_WIDTH, bn), lambda j, i: (0, j))],
        out_specs=pl.BlockSpec((bm, bn), lambda j, i: (i, j)),
        out_shape=jax.ShapeDtypeStruct((n, d), BF16),
        compiler_params=_cparams("parallel", "parallel"),
        name="merge",
    )(h0b, s5_out, rw_out, wts['w_gate'], wts['w_gate'], wts['gate_b'], wts['proj_s5'], wts['proj_rwkv'])


def _out_kernel(m_ref, h_ref, w_ref, g_ref, b_ref, rh_ref, rl_ref, o_ref, lg_ref):
    x = ALPHA * h_ref[...] + _dot(m_ref[...], w_ref[...])
    h1 = _layernorm(x, g_ref[...], b_ref[...])
    o_ref[...] = h1
    hi, lo = _split(h1)
    lg_ref[...] = _dot(hi, rh_ref[...]) + _dot(lo, rh_ref[...]) + _dot(hi, rl_ref[...])


def _out_proj(merged, h0, wts, bm):
    n, d = h0.shape
    row = pl.BlockSpec((bm, d), lambda i: (i, 0))
    vec = pl.BlockSpec((1, d), lambda i: (0, 0))
    rt = pl.BlockSpec((d, ROUTER_PAD), lambda i: (0, 0))
    return pl.pallas_call(
        _out_kernel,
        grid=(n // bm,),
        in_specs=[row, row, pl.BlockSpec((d, d), lambda i: (0, 0)), vec, vec, rt, rt],
        out_specs=[row, pl.BlockSpec((bm, ROUTER_PAD), lambda i: (i, 0))],
        out_shape=[jax.ShapeDtypeStruct((n, d), F32), jax.ShapeDtypeStruct((n, ROUTER_PAD), F32)],
        compiler_params=_cparams("parallel"),
        name="out_proj",
    )(merged, h0, wts['w_out'], wts['ln1_g'], wts['ln1_b'], wts['router_hi'], wts['router_lo'])


def _route(logits, wts, valid):
    i32 = jnp.int32
    lc = logits[:, :MOE_GROUPS] + wts['router_coarse_b']
    grp = jnp.argmax(lc, axis=-1).astype(i32)
    gate_c = jnp.max(jax.nn.softmax(lc, axis=-1), axis=-1)
    lf = (logits[:, MOE_GROUPS:MOE_GROUPS + N_EXPERTS] + wts['router_fine_b'])
    lf = lf.reshape(-1, MOE_GROUPS, EXPERTS_PER_GROUP)
    sel = grp[:, None, None] == jnp.arange(MOE_GROUPS, dtype=i32)[None, :, None]
    lf = jnp.sum(jnp.where(sel, lf, 0.0), axis=1)
    lane = jnp.arange(EXPERTS_PER_GROUP, dtype=i32)[None, :]
    i1 = jnp.argmax(lf, axis=-1).astype(i32)
    v1 = jnp.max(lf, axis=-1)
    rest = jnp.where(lane == i1[:, None], -jnp.inf, lf)
    i2 = jnp.argmax(rest, axis=-1).astype(i32)
    v2 = jnp.max(rest, axis=-1)
    top_v = jnp.stack([v1, v2], axis=-1)
    top_i = jnp.stack([i1, i2], axis=-1)
    w = gate_c[:, None] * jax.nn.softmax(top_v, axis=-1)
    expert = grp[:, None] * EXPERTS_PER_GROUP + top_i
    expert = jnp.where(valid[:, None], expert, N_EXPERTS)
    w = jnp.where(valid[:, None], w, 0.0)
    n_tok = logits.shape[0]
    n_asg = 2 * n_tok
    e_flat = expert.reshape(-1)
    order = jnp.argsort(e_flat).astype(i32)
    inv = jnp.argsort(order).astype(i32)
    bounds = jnp.sum(e_flat[None, :] < jnp.arange(N_EXPERTS + 1, dtype=i32)[:, None], axis=1, dtype=i32)
    start = bounds[:N_EXPERTS]
    counts = bounds[1:] - start
    padded = (counts + MOE_ROWS - 1) // MOE_ROWS * MOE_ROWS
    pend = jnp.cumsum(padded)
    pstart = pend - padded
    n_blocks = -(-n_asg // MOE_ROWS) + N_EXPERTS
    n_rows = n_blocks * MOE_ROWS
    n_used = pend[-1] // MOE_ROWS
    blk = jnp.minimum(jnp.arange(n_blocks, dtype=i32), n_used - 1)
    blk_exp = jnp.sum(pend[None, :] <= (blk * MOE_ROWS)[:, None], axis=1, dtype=i32)
    blk_exp = jnp.minimum(blk_exp, N_EXPERTS - 1)
    e_c = jnp.minimum(e_flat, N_EXPERTS - 1)
    pos = jnp.where(e_flat < N_EXPERTS, pstart[e_c] + (inv - start[e_c]), 0)
    d = jnp.arange(n_rows, dtype=i32)
    e_d = blk_exp[d // MOE_ROWS]
    k = d - pstart[e_d]
    src = order[jnp.clip(start[e_d] + k, 0, n_asg - 1)] // 2
    row_tok = jnp.where((k < counts[e_d]) & (d < pend[-1]), src, 0)
    return row_tok, pos, w, blk_exp, n_used.reshape(1).astype(i32), n_blocks


def _row_gather_start(src_hbm, idx_ref, n_rows, dst, sem, stride=1, offset=0):
    def issue(r, carry):
        row = idx_ref[0, 0, stride * r + offset]
        pltpu.make_async_copy(src_hbm.at[pl.ds(row, 1)], dst.at[pl.ds(r, 1)], sem).start()
        return carry

    lax.fori_loop(0, n_rows, issue, 0, unroll=8)


def _row_gather_wait(src_hbm, n_rows, dst, sem):
    def wait(r, carry):
        pltpu.make_async_copy(src_hbm.at[pl.ds(0, 1)], dst.at[pl.ds(r, 1)], sem).wait()
        return carry

    lax.fori_loop(0, n_rows, wait, 0, unroll=8)


def _expert_kernel(nused_ref, bexp_ref, idx_ref, nidx_ref, x_hbm, wg_ref, wu_ref, wd_ref, o_ref, buf, sem):
    i = pl.program_id(0)
    n_used = nused_ref[0]
    slot = i % 2

    @pl.when((i == 0) & (n_used > 0))
    def _():
        _row_gather_start(x_hbm, idx_ref, MOE_ROWS, buf.at[0], sem.at[0])

    @pl.when(i + 1 < n_used)
    def _():
        _row_gather_start(x_hbm, nidx_ref, MOE_ROWS, buf.at[1 - slot], sem.at[1 - slot])

    @pl.when(i < n_used)
    def _():
        _row_gather_wait(x_hbm, MOE_ROWS, buf.at[slot], sem.at[slot])
        x = buf[slot].astype(BF16)
        hb = jax.nn.silu(_dot(x, wg_ref[0])) * _dot(x, wu_ref[0])
        o_ref[...] = _dot(hb.astype(BF16), wd_ref[0])

    @pl.when(i >= n_used)
    def _():
        o_ref[...] = jnp.zeros_like(o_ref)


def _moe_experts(h1, row_tok, blk_exp, n_used, n_blocks, wts):
    d = h1.shape[1]
    idx = row_tok.reshape(n_blocks, 1, MOE_ROWS)
    last = n_blocks - 1
    smem = pltpu.SMEM
    return pl.pallas_call(
        _expert_kernel,
        grid_spec=pltpu.PrefetchScalarGridSpec(
            num_scalar_prefetch=2,
            grid=(n_blocks,),
            in_specs=[pl.BlockSpec((1, 1, MOE_ROWS), lambda i, nu, be: (i, 0, 0), memory_space=smem),
                      pl.BlockSpec((1, 1, MOE_ROWS), lambda i, nu, be: (jnp.minimum(i + 1, last), 0, 0),
                                   memory_space=smem),
                      pl.BlockSpec(memory_space=pl.ANY),
                      pl.BlockSpec((1, d, D_EXPERT), lambda i, nu, be: (be[i], 0, 0)),
                      pl.BlockSpec((1, d, D_EXPERT), lambda i, nu, be: (be[i], 0, 0)),
                      pl.BlockSpec((1, D_EXPERT, d), lambda i, nu, be: (be[i], 0, 0))],
            out_specs=pl.BlockSpec((MOE_ROWS, d), lambda i, nu, be: (i, 0)),
            scratch_shapes=[pltpu.VMEM((2, MOE_ROWS, d), F32), pltpu.SemaphoreType.DMA((2,))],
        ),
        out_shape=jax.ShapeDtypeStruct((n_blocks * MOE_ROWS, d), F32),
        compiler_params=_cparams("arbitrary"),
        name="moe_experts",
    )(n_used, blk_exp, idx, idx, h1, wts['exp_w_gate'], wts['exp_w_up'], wts['exp_w_down'])


def _combine_kernel(pos_ref, npos_ref, eo_hbm, h_ref, w_ref, g_ref, b_ref, o_ref, buf, sem, *, n_steps, bm):
    i = pl.program_id(0)
    slot = i % 2

    def start(p_ref, s):
        for k in range(2):
            _row_gather_start(eo_hbm, p_ref, bm, buf.at[s, k], sem.at[s], stride=2, offset=k)

    @pl.when(i == 0)
    def _():
        start(pos_ref, 0)

    @pl.when(i + 1 < n_steps)
    def _():
        start(npos_ref, 1 - slot)

    for k in range(2):
        _row_gather_wait(eo_hbm, bm, buf.at[slot, k], sem.at[slot])
    w = w_ref[...]
    moe = w[:, 0:1] * buf[slot, 0] + w[:, 1:2] * buf[slot, 1]
    o_ref[0] = _layernorm(ALPHA * h_ref[...] + moe, g_ref[...], b_ref[...])


def _moe_combine(eo, pos, w, h1, wts, bsz, t, t_pad):
    n, d = h1.shape
    bm = SEQ_FRONT
    per_seq = -(-t // bm)
    n_steps = bsz * per_seq
    blocks_per_seq = t_pad // bm

    def src(i):
        return (i // per_seq) * blocks_per_seq + 1 + i % per_seq

    def nxt(i):
        return src(jnp.minimum(i + 1, n_steps - 1))

    vec = pl.BlockSpec((1, d), lambda i: (0, 0))
    smem = pltpu.SMEM
    pos3 = pos.reshape(n // bm, 1, 2 * bm)
    return pl.pallas_call(
        functools.partial(_combine_kernel, n_steps=n_steps, bm=bm),
        grid=(n_steps,),
        in_specs=[pl.BlockSpec((1, 1, 2 * bm), lambda i: (src(i), 0, 0), memory_space=smem),
                  pl.BlockSpec((1, 1, 2 * bm), lambda i: (nxt(i), 0, 0), memory_space=smem),
                  pl.BlockSpec(memory_space=pl.ANY),
                  pl.BlockSpec((bm, d), lambda i: (src(i), 0)),
                  pl.BlockSpec((bm, 2), lambda i: (src(i), 0)), vec, vec],
        out_specs=pl.BlockSpec((1, bm, d), lambda i: (i // per_seq, i % per_seq, 0)),
        out_shape=jax.ShapeDtypeStruct((bsz, t, d), F32),
        scratch_shapes=[pltpu.VMEM((2, 2, bm, d), F32), pltpu.SemaphoreType.DMA((2,))],
        compiler_params=_cparams("arbitrary"),
        name="moe_combine",
    )(pos3, pos3, eo, h1, w, wts['ln2_g'], wts['ln2_b'])


def _prepare_weights(p):
    l = 0
    w_in = p['w_in'][l]
    c0 = S5_WIDTH
    c1 = c0 + 3 * RW_WIDTH + 2 * RW_DECAY_LORA + 2 * RW_ICLR_LORA + RW_GATE_LORA
    gpad = RW_GATE_PAD - RW_GATE_LORA
    wts = {}
    wts['w_u'] = w_in[:, :c0].astype(BF16)
    wts['w_rw'] = jnp.pad(w_in[:, c0:c1], ((0, 0), (0, gpad))).astype(BF16)
    wts['w_gate'] = w_in[:, c1:].astype(BF16)
    wts['mu'] = jnp.pad(p['shift_mu'][l], ((0, 0), (0, gpad)))
    z = jnp.zeros((RW_DECAY_LORA, RW_WIDTH), F32)
    wts['w2'] = jnp.block([[p['rw_w2'][l, 0], z], [z, p['rw_w2'][l, 1]]]).astype(BF16)
    wts['a2'] = jnp.block([[p['rw_a2'][l, 0], z], [z, p['rw_a2'][l, 1]]]).astype(BF16)
    wts['g2'] = jnp.pad(p['rw_g2'][l], ((0, gpad), (0, 0))).astype(BF16)
    wts['w0'] = p['rw_w0'][l]
    wts['a0'] = p['rw_a0'][l]
    wts['k_k'] = p['rw_k_k'][l].reshape(1, -1)
    wts['k_a'] = p['rw_k_a'][l].reshape(1, -1)
    wts['r_k'] = p['rw_r_k'][l].reshape(1, -1)
    wts['lnx_g'] = p['rw_lnx_g'][l].reshape(1, -1)
    wts['lnx_b'] = p['rw_lnx_b'][l].reshape(1, -1)
    head = jnp.arange(RW_WIDTH) // RW_HEAD
    e = (head[:, None] == jnp.arange(RW_HEADS)[None, :]).astype(BF16)
    wts['head_e'] = e
    wts['head_et'] = e.T
    wts['s5'] = _s5_matrices(p['s5_B_re'][l], p['s5_B_im'][l], p['s5_A_re'][l], p['s5_A_im'][l],
                             p['s5_log_dt'][l], p['s5_C_re'][l], p['s5_C_im'][l])
    wts['s5_D'] = p['s5_D'][l]
    wts['glu_w'] = p['s5_glu_w'][l].astype(BF16)
    wts['glu_b'] = p['s5_glu_b'][l]
    wts['proj_s5'] = p['proj_s5'][l].astype(BF16)
    wts['proj_rwkv'] = p['proj_rwkv'][l].astype(BF16)
    wts['gate_b'] = p['gate_b'][l]
    wts['w_out'] = p['w_out'][l].astype(BF16)
    wts['ln1_g'] = p['ln1_g'][l].reshape(1, -1)
    wts['ln1_b'] = p['ln1_b'][l].reshape(1, -1)
    router = jnp.concatenate([p['router_coarse'][l], p['router_fine'][l]], axis=1)
    router = jnp.pad(router, ((0, 0), (0, ROUTER_PAD - router.shape[1])))
    wts['router_hi'], wts['router_lo'] = _split(router)
    wts['router_coarse_b'] = p['router_coarse_b'][l]
    wts['router_fine_b'] = p['router_fine_b'][l]
    wts['exp_w_gate'] = p['exp_w_gate'][l].astype(BF16)
    wts['exp_w_up'] = p['exp_w_up'][l].astype(BF16)
    wts['exp_w_down'] = p['exp_w_down'][l].astype(BF16)
    wts['ln2_g'] = p['ln2_g'][l].reshape(1, -1)
    wts['ln2_b'] = p['ln2_b'][l].reshape(1, -1)
    return wts


def _encode(x, p, wts):
    bsz, t, d = x.shape
    t_lo = SEQ_FRONT - N_META
    t_hi = SEQ_FRONT + t
    t_pad = -(-t_hi // SEQ_ALIGN) * SEQ_ALIGN
    n = bsz * t_pad
    h0, h0b = _ln_in(x, p['meta'], p['ln_in_g'], p['ln_in_b'], t_pad)
    h0 = h0.reshape(n, d)
    h0b = h0b.reshape(n, d)
    bm = _row_block(t_pad, 1024)
    u = _mm(h0b, wts['w_u'], bm, S5_WIDTH, F32, "proj_s5_in")
    rw = _mm(h0b, wts['w_rw'], bm, RW_COLS // 4, F32, "proj_rwkv_in")
    y_ssm = _s5_ssm(u, wts['s5'], bsz, t_pad)
    s5_out = _s5_post(y_ssm, u, wts['s5_D'], wts['glu_w'], wts['glu_b'], bm)
    r, kk, v, g, bv, lw, kd, bb = _rwkv_prep(rw.reshape(bsz, t_pad, RW_COLS), wts, t_lo, t_hi)
    yf, yb = _rwkv_scan(r, kk, v, lw, kd, bb)
    rw_out = _rwkv_post(yf.reshape(n, -1), yb.reshape(n, -1), bv.reshape(n, -1), g.reshape(n, -1), wts, bm)
    merged = _merge(h0b, s5_out, rw_out, wts, bm)
    h1, logits = _out_proj(merged, h0, wts, _row_block(t_pad, 512))
    seq_pos = jnp.arange(n, dtype=jnp.int32) % t_pad
    valid = (seq_pos >= t_lo) & (seq_pos < t_hi)
    row_tok, pos, w, blk_exp, n_used, n_blocks = _route(logits, wts, valid)
    eo = _moe_experts(h1, row_tok, blk_exp, n_used, n_blocks, wts)
    return _moe_combine(eo, pos, w, h1, wts, bsz, t, t_pad)


def kernel(x_prompt, x_sample, meta, ln_in_g, ln_in_b, w_in, shift_mu, s5_B_re, s5_B_im, s5_A_re, s5_A_im, s5_log_dt, s5_C_re, s5_C_im, s5_D, s5_glu_w, s5_glu_b, rw_w0, rw_w2, rw_a0, rw_a2, rw_g2, rw_k_k, rw_k_a, rw_r_k, rw_lnx_g, rw_lnx_b, proj_s5, proj_rwkv, gate_b, w_out, ln1_g, ln1_b, router_coarse, router_coarse_b, router_fine, router_fine_b, exp_w_gate, exp_w_up, exp_w_down, ln2_g, ln2_b):
    p = {
        'meta': meta, 'ln_in_g': ln_in_g, 'ln_in_b': ln_in_b, 'w_in': w_in, 'shift_mu': shift_mu,
        's5_B_re': s5_B_re, 's5_B_im': s5_B_im, 's5_A_re': s5_A_re, 's5_A_im': s5_A_im,
        's5_log_dt': s5_log_dt, 's5_C_re': s5_C_re, 's5_C_im': s5_C_im, 's5_D': s5_D,
        's5_glu_w': s5_glu_w, 's5_glu_b': s5_glu_b,
        'rw_w0': rw_w0, 'rw_w2': rw_w2, 'rw_a0': rw_a0, 'rw_a2': rw_a2, 'rw_g2': rw_g2,
        'rw_k_k': rw_k_k, 'rw_k_a': rw_k_a, 'rw_r_k': rw_r_k, 'rw_lnx_g': rw_lnx_g, 'rw_lnx_b': rw_lnx_b,
        'proj_s5': proj_s5, 'proj_rwkv': proj_rwkv, 'gate_b': gate_b, 'w_out': w_out,
        'ln1_g': ln1_g, 'ln1_b': ln1_b,
        'router_coarse': router_coarse, 'router_coarse_b': router_coarse_b,
        'router_fine': router_fine, 'router_fine_b': router_fine_b,
        'exp_w_gate': exp_w_gate, 'exp_w_up': exp_w_up, 'exp_w_down': exp_w_down,
        'ln2_g': ln2_g, 'ln2_b': ln2_b,
    }
    wts = _prepare_weights(p)
    return (_encode(x_prompt, p, wts), _encode(x_sample, p, wts))
```

```python
import functools
import math

import jax
import jax.numpy as jnp
from jax import lax
from jax.experimental import pallas as pl
from jax.experimental.pallas import tpu as pltpu

F32 = jnp.float32
BF16 = jnp.bfloat16

D_MODEL = 2048
N_META = 16
S5_WIDTH = 1024
S5_GROUP = 16
S5_GROUPS = 64
S5_STATE = 64
S5_GB = 8
S5_CHUNK = 16
RW_WIDTH = 1024
RW_HEAD = 64
RW_HEADS = 16
RW_DECAY_LORA = 64
RW_ICLR_LORA = 64
RW_GATE_LORA = 160
RW_GATE_PAD = 256
RW_COLS = 3 * RW_WIDTH + 2 * RW_DECAY_LORA + 2 * RW_ICLR_LORA + RW_GATE_PAD
RW_CHUNK = 64
RW_HEADS_PER_STEP = 2
MOE_GROUPS = 4
EXPERTS_PER_GROUP = 8
N_EXPERTS = 32
D_EXPERT = 512
MOE_ROWS = 256
ROUTER_PAD = 128
DEPTH = 1
ALPHA = (2 * DEPTH) ** 0.25
LN_EPS = 1e-5
GN_EPS = 64e-5
SEQ_ALIGN = 64
SEQ_FRONT = 64
VMEM_LIMIT = 56 * 1024 * 1024


def _cparams(*sem):
    return pltpu.CompilerParams(dimension_semantics=sem, vmem_limit_bytes=VMEM_LIMIT)


def _row_block(t_pad, cap, mult=8):
    best = mult
    for d in range(mult, cap + 1, mult):
        if t_pad % d == 0:
            best = d
    return best


def _dot(a, b):
    return jnp.dot(a, b, preferred_element_type=F32)


def _dot_nt(a, b):
    return lax.dot_general(a, b, (((1,), (1,)), ((), ())), preferred_element_type=F32)


def _dot_tn(a, b):
    return lax.dot_general(a, b, (((0,), (0,)), ((), ())), preferred_element_type=F32)


def _split(x):
    hi = x.astype(BF16)
    lo = (x - hi.astype(F32)).astype(BF16)
    return hi, lo


def _layernorm(x, g, b):
    mu = jnp.mean(x, axis=-1, keepdims=True)
    xc = x - mu
    var = jnp.mean(xc * xc, axis=-1, keepdims=True)
    return xc * lax.rsqrt(var + LN_EPS) * g + b


def _ln_in_kernel(x_ref, m_ref, g_ref, b_ref, of_ref, ob_ref, *, t):
    j = pl.program_id(1)
    blk = SEQ_FRONT

    @pl.when(j == 0)
    def _():
        ym = _layernorm(m_ref[...], g_ref[...], b_ref[...])
        y = jnp.concatenate([jnp.zeros((blk - N_META, ym.shape[1]), F32), ym], axis=0)
        of_ref[0] = y
        ob_ref[0] = y.astype(BF16)

    @pl.when(j > 0)
    def _():
        y = _layernorm(x_ref[0], g_ref[...], b_ref[...])
        pos = (j - 1) * blk + lax.broadcasted_iota(jnp.int32, (blk, 1), 0)
        y = jnp.where(pos < t, y, 0.0)
        of_ref[0] = y
        ob_ref[0] = y.astype(BF16)


def _ln_in(x, meta, g, b, t_pad):
    bsz, t, d = x.shape
    blk = SEQ_FRONT
    row = pl.BlockSpec((1, blk, d), lambda i, j: (i, j, 0))
    vec = pl.BlockSpec((1, d), lambda i, j: (0, 0))
    return pl.pallas_call(
        functools.partial(_ln_in_kernel, t=t),
        grid=(bsz, t_pad // blk),
        in_specs=[pl.BlockSpec((1, blk, d), lambda i, j: (i, jnp.maximum(j - 1, 0), 0)),
                  pl.BlockSpec((N_META, d), lambda i, j: (0, 0)), vec, vec],
        out_specs=[row, row],
        out_shape=[jax.ShapeDtypeStruct((bsz, t_pad, d), F32), jax.ShapeDtypeStruct((bsz, t_pad, d), BF16)],
        compiler_params=_cparams("parallel", "parallel"),
        name="ln_in",
    )(x, meta, g.reshape(1, d), b.reshape(1, d))


def _mm_kernel(x_ref, w_ref, o_ref):
    o_ref[...] = _dot(x_ref[...], w_ref[...]).astype(o_ref.dtype)


def _mm(x, w, bm, bn, out_dtype, name):
    n, k = x.shape
    m = w.shape[1]
    return pl.pallas_call(
        _mm_kernel,
        grid=(m // bn, n // bm),
        in_specs=[pl.BlockSpec((bm, k), lambda j, i: (i, 0)),
                  pl.BlockSpec((k, bn), lambda j, i: (0, j))],
        out_specs=pl.BlockSpec((bm, bn), lambda j, i: (i, j)),
        out_shape=jax.ShapeDtypeStruct((n, m), out_dtype),
        compiler_params=_cparams("parallel", "parallel"),
        name=name,
    )(x, w)


def _s5_matrices(b_re, b_im, a_re, a_im, log_dt, c_re, c_im):
    L = S5_CHUNK
    dt = jnp.exp(log_dt)[..., None]
    mag = jnp.exp(a_re * dt)
    abr = mag * jnp.cos(a_im * dt)
    abi = mag * jnp.sin(a_im * dt)
    den = a_re * a_re + a_im * a_im
    nr = abr - 1.0
    cr = (nr * a_re + abi * a_im) / den
    ci = (abi * a_re - nr * a_im) / den
    bbr = cr[..., None] * b_re - ci[..., None] * b_im
    bbi = cr[..., None] * b_im + ci[..., None] * b_re
    tau = jnp.arange(L + 1, dtype=F32)[:, None, None, None]
    pmag = jnp.exp(tau * a_re * dt)
    pr = pmag * jnp.cos(tau * a_im * dt)
    pi = pmag * jnp.sin(tau * a_im * dt)
    wr = pr[..., None] * bbr - pi[..., None] * bbi
    wi = pr[..., None] * bbi + pi[..., None] * bbr
    kern = (jnp.einsum('zgop,tzgpi->tzgoi', c_re, wr)
            - jnp.einsum('zgop,tzgpi->tzgoi', c_im, wi))
    s = jnp.arange(L)[:, None]
    t = jnp.arange(L)[None, :]
    lag = t - s
    kf = jnp.where((lag >= 0)[..., None, None, None], kern[jnp.clip(lag, 0, L), 0], 0.0)
    kb = jnp.where((lag <= 0)[..., None, None, None], kern[jnp.clip(-lag, 0, L), 1], 0.0)
    toep = (kf + kb).transpose(2, 0, 4, 1, 3).reshape(S5_GROUPS, L * S5_GROUP, L * S5_GROUP)
    wf_r, wf_i = wr[::-1][1:, 0], wi[::-1][1:, 0]
    wb_r, wb_i = wr[:L, 1], wi[:L, 1]
    bmat = jnp.concatenate([wf_r, wb_r, wf_i, wb_i], axis=2)
    bmat = bmat.transpose(1, 0, 3, 2).reshape(S5_GROUPS, L * S5_GROUP, 4 * S5_STATE)
    pf_r, pf_i = pr[1:, 0], pi[1:, 0]
    pb_r, pb_i = pr[::-1][:L, 1], pi[::-1][:L, 1]
    c0r, c0i, c1r, c1i = c_re[0], c_im[0], c_re[1], c_im[1]

    def cpow(cre, cim, p_r, p_i):
        re = cre[None] * p_r[:, :, None, :] - cim[None] * p_i[:, :, None, :]
        im = cre[None] * p_i[:, :, None, :] + cim[None] * p_r[:, :, None, :]
        return re, -im

    f_re, f_im = cpow(c0r, c0i, pf_r, pf_i)
    g_re, g_im = cpow(c1r, c1i, pb_r, pb_i)
    cmat = jnp.concatenate([f_re, g_re, f_im, g_im], axis=3)
    cmat = cmat.transpose(1, 3, 0, 2).reshape(S5_GROUPS, 4 * S5_STATE, L * S5_GROUP)
    lam_re = jnp.concatenate([pr[L, 0], pr[L, 1]], axis=-1)[:, None, :]
    lam_im = jnp.concatenate([pi[L, 0], pi[L, 1]], axis=-1)[:, None, :]
    return bmat, toep, cmat, lam_re, lam_im


def _s5_block_operators(bmat, toep, cmat, lam_re, lam_im):
    nb, gb, L, c, p = S5_GROUPS // S5_GB, S5_GB, S5_CHUNK, S5_GROUP, S5_STATE
    eye = jnp.eye(gb, dtype=F32)
    quarters = jnp.array([0, 2, 1, 3])
    bm = bmat.reshape(nb, gb, L, c, 4, p)[:, :, :, :, quarters]
    wb = jnp.einsum('gjsckp,jJ->gsjckJp', bm, eye).reshape(nb, L * gb * c, 4 * gb * p)
    tp = toep.reshape(nb, gb, L, c, L, c)
    wt = jnp.einsum('gjscto,jJ->gsjctJo', tp, eye).reshape(nb, L * gb * c, L * gb * c)
    cm = cmat.reshape(nb, gb, 4, p, L, c)[:, :, quarters]
    wc = jnp.einsum('gjkpto,jJ->gkjptJo', cm, eye).reshape(nb, 4 * gb * p, L * gb * c)
    ar_f = lam_re[:, 0, :p].reshape(nb, gb * p)
    ar_b = lam_re[:, 0, p:].reshape(nb, gb * p)
    ai_f = lam_im[:, 0, :p].reshape(nb, gb * p)
    ai_b = lam_im[:, 0, p:].reshape(nb, gb * p)
    zero = jnp.zeros_like(ar_f)
    a1 = jnp.stack([ar_f, ar_f, ar_b, ar_b, zero, zero, zero, zero], axis=1)
    a2 = jnp.stack([-ai_f, ai_f, -ai_b, ai_b, zero, zero, zero, zero], axis=1)
    return wb.astype(BF16), wt.astype(BF16), wc.astype(BF16), a1, a2


def _s5_kernel(u_ref, wb_ref, wt_ref, wc_ref, a1_ref, a2_ref, y_ref, s_scr, xf_scr, xb_scr, *, n_chunks):
    L = S5_CHUNK
    C = n_chunks
    sw = S5_GB * S5_STATE
    u8 = jnp.concatenate([u_ref[0, pl.ds(s, C, stride=L), :] for s in range(L)], axis=1).astype(BF16)
    s_all = _dot(u8, wb_ref[0])
    for k in range(4):
        s_scr[:, k, :] = s_all[:, k * sw:(k + 1) * sw]
    s_scr[:, 4:8, :] = jnp.zeros((C, 4, sw), F32)
    a1 = a1_ref[0]
    a2 = a2_ref[0]
    row = lax.broadcasted_iota(jnp.int32, (8, sw), 0)
    even = (row % 2) == 0
    is_fwd = row < 2

    def step(c, x):
        rc = C - 1 - c
        xf_scr[c] = x
        xb_scr[rc] = x
        s = jnp.where(is_fwd, s_scr[c], s_scr[rc])
        swapped = jnp.where(even, pltpu.roll(x, 7, 0), pltpu.roll(x, 1, 0))
        return a1 * x + a2 * swapped + s

    lax.fori_loop(0, C, step, jnp.zeros((8, sw), F32))
    x_in = jnp.concatenate([xf_scr[:, 0, :], xf_scr[:, 1, :], xb_scr[:, 2, :], xb_scr[:, 3, :]],
                           axis=1).astype(BF16)
    y8 = _dot(u8, wt_ref[0]) + _dot(x_in, wc_ref[0])
    for t in range(L):
        y_ref[0, pl.ds(t, C, stride=L), :] = y8[:, t * 128:(t + 1) * 128]


def _s5_ssm(u3, ops):
    wb, wt, wc, a1, a2 = ops
    bsz, t_pad, width = u3.shape
    n_chunks = t_pad // S5_CHUNK
    nb = S5_GROUPS // S5_GB
    lanes = S5_GB * S5_GROUP
    sw = S5_GB * S5_STATE
    blk = pl.BlockSpec((1, t_pad, lanes), lambda g, b: (b, 0, g))
    mat = pl.BlockSpec((1,) + wb.shape[1:], lambda g, b: (g, 0, 0), pipeline_mode=pl.Buffered(1))
    vec = pl.BlockSpec((1, 8, sw), lambda g, b: (g, 0, 0))
    return pl.pallas_call(
        functools.partial(_s5_kernel, n_chunks=n_chunks),
        grid=(nb, bsz),
        in_specs=[blk, mat, mat, mat, vec, vec],
        out_specs=blk,
        out_shape=jax.ShapeDtypeStruct(u3.shape, F32),
        scratch_shapes=[pltpu.VMEM((n_chunks, 8, sw), F32)] * 3,
        compiler_params=_cparams("arbitrary", "arbitrary"),
        name="s5_ssm",
    )(u3, wb, wt, wc, a1, a2)


def _s5_post_kernel(y_ref, u_ref, d_ref, w_ref, b_ref, o_ref):
    y = y_ref[...] + u_ref[...] * d_ref[...]
    act = y * (0.5 * (1.0 + jnp.tanh(math.sqrt(2.0 / math.pi) * (y + 0.044715 * (y * y * y)))))
    z = _dot(act.astype(BF16), w_ref[...]) + b_ref[...]
    o_ref[...] = (act * jax.nn.sigmoid(z)).astype(o_ref.dtype)


def _s5_post(y, u, d_skip, glu_w, glu_b, bm):
    n = y.shape[0]
    row = pl.BlockSpec((bm, S5_WIDTH), lambda i: (i, 0))
    vec = pl.BlockSpec((1, S5_WIDTH), lambda i: (0, 0))
    return pl.pallas_call(
        _s5_post_kernel,
        grid=(n // bm,),
        in_specs=[row, row, vec, pl.BlockSpec((S5_WIDTH, S5_WIDTH), lambda i: (0, 0)), vec],
        out_specs=row,
        out_shape=jax.ShapeDtypeStruct((n, S5_WIDTH), BF16),
        compiler_params=_cparams("parallel"),
        name="s5_post",
    )(y, u, d_skip.reshape(1, -1), glu_w, glu_b.reshape(1, -1))


def _head_sum(x, e_ref, et_ref):
    hi, lo = _split(x)
    s = _dot(hi, e_ref[...]) + _dot(lo, e_ref[...])
    shi, slo = _split(s)
    return _dot(shi, et_ref[...]) + _dot(slo, et_ref[...])


def _rwkv_prep_kernel(cur_ref, prev_ref, next_ref, mu_ref, w2_ref, a2_ref, g2_ref, w0_ref, a0_ref,
                      kk_ref, ka_ref, rk_ref, e_ref, et_ref,
                      r_o, kk_o, v_o, g_o, bv_o, lw_o, kd_o, bb_o, *, t_lo, t_hi, bm):
    j = pl.program_id(1)
    p = cur_ref[0]
    row = lax.broadcasted_iota(jnp.int32, (bm, 1), 0)
    prev_row = jnp.where(j > 0, prev_ref[0, 7:8, :], 0.0)
    next_row = jnp.where(j < pl.num_programs(1) - 1, next_ref[0, 0:1, :], 0.0)
    prev = jnp.where(row == 0, prev_row, pltpu.roll(p, 1, 0))
    nxt = jnp.where(row == bm - 1, next_row, pltpu.roll(p, bm - 1, 0))
    xs = p + mu_ref[0:1, :] * (prev - p) + mu_ref[1:2, :] * (nxt - p)
    w = RW_WIDTH
    r = xs[:, 0:w]
    k = xs[:, w:2 * w]
    v = xs[:, 2 * w:3 * w]
    lw = xs[:, 3 * w:3 * w + 128]
    la = xs[:, 3 * w + 128:3 * w + 256]
    lg = xs[:, 3 * w + 256:]
    w_log = _dot(jnp.tanh(lw).astype(BF16), w2_ref[...])
    a_lin = _dot(la.astype(BF16), a2_ref[...])
    g = _dot(jax.nn.sigmoid(lg).astype(BF16), g2_ref[...])
    kk = k * kk_ref[...]
    n2 = _head_sum(kk * kk, e_ref, et_ref)
    kk = kk / jnp.maximum(jnp.sqrt(n2), 1e-12)
    pos = j * bm + row
    valid = (pos >= t_lo) & (pos < t_hi)
    v = jnp.where(valid, v, 0.0)
    kd_sum = jnp.zeros_like(k)
    for z in range(2):
        wl = w_log[:, z * w:(z + 1) * w] + w0_ref[z:z + 1, :]
        lw_o[z, 0] = -math.exp(-0.5) * jax.nn.sigmoid(wl)
        a = jax.nn.sigmoid(a_lin[:, z * w:(z + 1) * w] + a0_ref[z:z + 1, :])
        kd = k * (1.0 + (a - 1.0) * ka_ref[...])
        kd_o[z, 0] = kd.astype(BF16)
        bb_o[z, 0] = (kk * a).astype(BF16)
        kd_sum = kd_sum + kd
    bonus = _head_sum(r * kd_sum * rk_ref[...], e_ref, et_ref)
    r_o[0] = r.astype(BF16)
    kk_o[0] = kk.astype(BF16)
    v_o[0] = v.astype(BF16)
    g_o[0] = g.astype(BF16)
    bv_o[0] = (bonus * v).astype(BF16)


def _rwkv_prep(rw3, wts, t_lo, t_hi):
    bsz, t_pad, _ = rw3.shape
    bm = _row_block(t_pad, 320, 64)
    nb8 = bm // 8
    last8 = t_pad // 8 - 1
    w = RW_WIDTH
    cur = pl.BlockSpec((1, bm, RW_COLS), lambda b, j: (b, j, 0))
    prev = pl.BlockSpec((1, 8, RW_COLS), lambda b, j: (b, jnp.maximum(j * nb8 - 1, 0), 0))
    nxt = pl.BlockSpec((1, 8, RW_COLS), lambda b, j: (b, jnp.minimum((j + 1) * nb8, last8), 0))

    def full(a):
        return pl.BlockSpec(a.shape, lambda b, j: (0,) * a.ndim)

    shared = pl.BlockSpec((1, bm, w), lambda b, j: (b, j, 0))
    per_dir = pl.BlockSpec((2, 1, bm, w), lambda b, j: (0, b, j, 0))
    consts = [wts['mu'], wts['w2'], wts['a2'], wts['g2'], wts['w0'], wts['a0'],
              wts['k_k'], wts['k_a'], wts['r_k'], wts['head_e'], wts['head_et']]
    sds = jax.ShapeDtypeStruct
    return pl.pallas_call(
        functools.partial(_rwkv_prep_kernel, t_lo=t_lo, t_hi=t_hi, bm=bm),
        grid=(bsz, t_pad // bm),
        in_specs=[cur, prev, nxt] + [full(a) for a in consts],
        out_specs=[shared] * 5 + [per_dir] * 3,
        out_shape=[sds((bsz, t_pad, w), BF16)] * 5
        + [sds((2, bsz, t_pad, w), F32), sds((2, bsz, t_pad, w), BF16), sds((2, bsz, t_pad, w), BF16)],
        compiler_params=_cparams("parallel", "parallel"),
        name="rwkv_prep",
    )(rw3, rw3, rw3, *consts)


def _rwkv_chunk(fwd, r_ref, kk_ref, v_ref, lw_ref, kd_ref, bb_ref, y_ref, st_ref, z):
    L = RW_CHUNK
    hd = RW_HEAD
    gw = RW_HEADS_PER_STEP * hd
    n_groups = RW_WIDTH // gw
    row = lax.broadcasted_iota(jnp.int32, (L, L), 0)
    col = lax.broadcasted_iota(jnp.int32, (L, L), 1)
    tri = jnp.where((col <= row) if fwd else (col >= row), 1.0, 0.0).astype(BF16)
    grow = lax.broadcasted_iota(jnp.int32, (L, gw), 0)
    gcol = lax.broadcasted_iota(jnp.int32, (L, gw), 1) % L
    incl = (gcol <= grow) if fwd else (gcol >= grow)
    strict = (gcol < grow) if fwd else (gcol > grow)
    bd_mask = jnp.where(lax.broadcasted_iota(jnp.int32, (gw, gw), 0) // hd
                        == lax.broadcasted_iota(jnp.int32, (gw, gw), 1) // hd, 1.0, 0.0).astype(BF16)

    def bd(x):
        return jnp.concatenate([x] * RW_HEADS_PER_STEP, axis=0) * bd_mask

    def stack(x):
        return jnp.concatenate([x[:, h * hd:(h + 1) * hd] for h in range(RW_HEADS_PER_STEP)], axis=0)

    lw = lw_ref[0, 0]
    lw_hi, lw_lo = _split(lw)
    c = _dot(tri, lw_hi) + _dot(tri, lw_lo)
    e = c - lw
    c_tot = c[L - 1:L, :] if fwd else c[0:1, :]
    r = r_ref[0].astype(F32)
    kk = kk_ref[0].astype(F32)
    kd = kd_ref[0, 0].astype(F32)
    bb = bb_ref[0, 0].astype(F32)
    v = v_ref[0]
    q1 = (kk * jnp.exp(e)).astype(BF16)
    q2 = (r * jnp.exp(c)).astype(BF16)
    inv = jnp.exp(-c)
    k1 = (kd * inv).astype(BF16)
    k2 = (bb * inv).astype(BF16)
    rest = jnp.exp(c_tot - c)
    k1p = (kd * rest).astype(BF16)
    k2p = (bb * rest).astype(BF16)
    dec_tot = jnp.exp(c_tot)
    def group_chain(g):
        sl = slice(g * gw, (g + 1) * gw)
        s0 = st_ref[z, g]
        s0_hi, s0_lo = _split(s0)
        lhs = jnp.concatenate([q1[:, sl], q2[:, sl]], axis=0)
        rhs = jnp.concatenate([bd(k1[:, sl]), bd(k2[:, sl]), bd(s0_hi), bd(s0_lo)], axis=0)
        m1 = _dot_nt(lhs, rhs)
        yield
        a_kd = jnp.where(strict, m1[:L, 0:gw], 0.0)
        a_b = jnp.where(strict, m1[:L, gw:2 * gw], 0.0)
        q1s = m1[:L, 2 * gw:3 * gw] + m1[:L, 3 * gw:]
        b_kd = jnp.where(incl, m1[L:, 0:gw], 0.0)
        b_b = jnp.where(incl, m1[L:, gw:2 * gw], 0.0)
        q2s = m1[L:, 2 * gw:3 * gw] + m1[L:, 3 * gw:]
        vg = v[:, sl]
        v_bd = bd(vg)
        x = q1s + _dot(a_kd.astype(BF16), v_bd)
        yield
        m = -a_b
        levels = L.bit_length() - 1
        for lvl in range(levels):
            mb = m.astype(BF16)
            if lvl < levels - 1:
                rr = _dot(mb, jnp.concatenate([bd(x.astype(BF16)), bd(mb)], axis=1))
                x = x + rr[:, :gw]
                m = rr[:, gw:]
            else:
                x = x + _dot(mb, bd(x.astype(BF16)))
            yield
        ub = x.astype(BF16)
        y = q2s + _dot(jnp.concatenate([b_kd, -b_b], axis=1).astype(BF16),
                       jnp.concatenate([v_bd, bd(ub)], axis=0))
        y_ref[0, :, sl] = y
        yield
        st_ref[z, g] = s0 * dec_tot[:, sl] + _dot_tn(
            jnp.concatenate([stack(vg), stack(ub)], axis=0),
            jnp.concatenate([bd(k1p[:, sl]), -bd(k2p[:, sl])], axis=0))

    return [group_chain(g) for g in range(n_groups)]


def _rwkv_scan_kernel(rf, kkf, vf, lwf, kdf, bbf, rb, kkb, vb, lwb, kdb, bbb, yf_ref, yb_ref, st_ref):
    @pl.when(pl.program_id(1) == 0)
    def _():
        st_ref[...] = jnp.zeros_like(st_ref)

    chains = (_rwkv_chunk(True, rf, kkf, vf, lwf, kdf, bbf, yf_ref, st_ref, 0)
              + _rwkv_chunk(False, rb, kkb, vb, lwb, kdb, bbb, yb_ref, st_ref, 1))
    while chains:
        alive = []
        for chain in chains:
            try:
                next(chain)
                alive.append(chain)
            except StopIteration:
                pass
        chains = alive


def _rwkv_scan(r, kk, v, lw, kd, bb):
    bsz, t_pad, w = r.shape
    L = RW_CHUNK
    nc = t_pad // L
    gw = RW_HEADS_PER_STEP * RW_HEAD
    grid = (bsz, nc)

    def shared(fwd):
        if fwd:
            return pl.BlockSpec((1, L, w), lambda b, j: (b, j, 0))
        return pl.BlockSpec((1, L, w), lambda b, j: (b, nc - 1 - j, 0))

    def per_dir(fwd):
        if fwd:
            return pl.BlockSpec((1, 1, L, w), lambda b, j: (0, b, j, 0))
        return pl.BlockSpec((1, 1, L, w), lambda b, j: (1, b, nc - 1 - j, 0))

    in_specs = []
    for fwd in (True, False):
        in_specs += [shared(fwd), shared(fwd), shared(fwd), per_dir(fwd), per_dir(fwd), per_dir(fwd)]
    return pl.pallas_call(
        _rwkv_scan_kernel,
        grid=grid,
        in_specs=in_specs,
        out_specs=[shared(True), shared(False)],
        out_shape=[jax.ShapeDtypeStruct((bsz, t_pad, w), F32)] * 2,
        scratch_shapes=[pltpu.VMEM((2, w // gw, RW_HEAD, gw), F32)],
        compiler_params=_cparams("parallel", "arbitrary"),
        name="rwkv_scan",
    )(r, kk, v, lw, kd, bb, r, kk, v, lw, kd, bb)


def _rwkv_post_kernel(yf_ref, yb_ref, bv_ref, g_ref, lg_ref, lb_ref, e_ref, et_ref, o_ref):
    y = yf_ref[...] + yb_ref[...]
    mean = _head_sum(y, e_ref, et_ref) * (1.0 / RW_HEAD)
    yc = y - mean
    var = _head_sum(yc * yc, e_ref, et_ref) * (1.0 / RW_HEAD)
    y = yc * lax.rsqrt(var + GN_EPS) * lg_ref[...] + lb_ref[...]
    o_ref[...] = ((y + bv_ref[...].astype(F32)) * g_ref[...].astype(F32)).astype(o_ref.dtype)


def _rwkv_post(yf, yb, bv, g, wts, bm):
    n, w = yf.shape
    row = pl.BlockSpec((bm, w), lambda i: (i, 0))
    vec = pl.BlockSpec((1, w), lambda i: (0, 0))
    e, et = wts['head_e'], wts['head_et']
    return pl.pallas_call(
        _rwkv_post_kernel,
        grid=(n // bm,),
        in_specs=[row, row, row, row, vec, vec,
                  pl.BlockSpec(e.shape, lambda i: (0, 0)), pl.BlockSpec(et.shape, lambda i: (0, 0))],
        out_specs=row,
        out_shape=jax.ShapeDtypeStruct((n, w), BF16),
        compiler_params=_cparams("parallel"),
        name="rwkv_post",
    )(yf, yb, bv, g, wts['lnx_g'], wts['lnx_b'], e, et)


def _merge_kernel(h_ref, s5_ref, rw_ref, wg0_ref, wg1_ref, gb_ref, p0_ref, p1_ref, o_ref):
    h = h_ref[...]
    g0 = jax.nn.sigmoid(_dot(h, wg0_ref[...]) + gb_ref[0:1, :])
    g1 = jax.nn.sigmoid(_dot(h, wg1_ref[...]) + gb_ref[1:2, :])
    merged = g0 * _dot(s5_ref[...], p0_ref[...]) + g1 * _dot(rw_ref[...], p1_ref[...])
    o_ref[...] = merged.astype(o_ref.dtype)


def _merge(h0b, s5_out, rw_out, wts, bm):
    n, d = h0b.shape
    bn = 1024
    nj = d // bn
    return pl.pallas_call(
        _merge_kernel,
        grid=(nj, n // bm),
        in_specs=[pl.BlockSpec((bm, d), lambda j, i: (i, 0)),
                  pl.BlockSpec((bm, S5_WIDTH), lambda j, i: (i, 0)),
                  pl.BlockSpec((bm, RW_WIDTH), lambda j, i: (i, 0)),
                  pl.BlockSpec((d, bn), lambda j, i: (0, j)),
                  pl.BlockSpec((d, bn), lambda j, i: (0, nj + j)),
                  pl.BlockSpec((2, bn), lambda j, i: (0, j)),
                  pl.BlockSpec((S5_WIDTH, bn), lambda j, i: (0, j)),
                  pl.BlockSpec((RW_WIDTH, bn), lambda j, i: (0, j))],
        out_specs=pl.BlockSpec((bm, bn), lambda j, i: (i, j)),
        out_shape=jax.ShapeDtypeStruct((n, d), BF16),
        compiler_params=_cparams("parallel", "parallel"),
        name="merge",
    )(h0b, s5_out, rw_out, wts['w_gate'], wts['w_gate'], wts['gate_b'], wts['proj_s5'], wts['proj_rwkv'])


def _to_slabs(ref, x):
    for j in range(x.shape[1] // 128):
        ref[:, j, :] = x[:, j * 128:(j + 1) * 128]


def _from_slabs(ref_view, n_slabs):
    return jnp.concatenate([ref_view[:, j, :] for j in range(n_slabs)], axis=1)


def _out_kernel(m_ref, h_ref, w_ref, g_ref, b_ref, rh_ref, rl_ref, o_ref, os_ref, lg_ref):
    x = ALPHA * h_ref[...] + _dot(m_ref[...], w_ref[...])
    h1 = _layernorm(x, g_ref[...], b_ref[...])
    o_ref[...] = h1
    _to_slabs(os_ref, h1)
    hi, lo = _split(h1)
    lg_ref[...] = _dot(hi, rh_ref[...]) + _dot(lo, rh_ref[...]) + _dot(hi, rl_ref[...])


def _out_proj(merged, h0, wts, bm):
    n, d = h0.shape
    row = pl.BlockSpec((bm, d), lambda i: (i, 0))
    vec = pl.BlockSpec((1, d), lambda i: (0, 0))
    rt = pl.BlockSpec((d, ROUTER_PAD), lambda i: (0, 0))
    return pl.pallas_call(
        _out_kernel,
        grid=(n // bm,),
        in_specs=[row, row, pl.BlockSpec((d, d), lambda i: (0, 0)), vec, vec, rt, rt],
        out_specs=[row, pl.BlockSpec((bm, d // 128, 128), lambda i: (i, 0, 0)),
                   pl.BlockSpec((bm, ROUTER_PAD), lambda i: (i, 0))],
        out_shape=[jax.ShapeDtypeStruct((n, d), F32), jax.ShapeDtypeStruct((n, d // 128, 128), F32),
                   jax.ShapeDtypeStruct((n, ROUTER_PAD), F32)],
        compiler_params=_cparams("parallel"),
        name="out_proj",
    )(merged, h0, wts['w_out'], wts['ln1_g'], wts['ln1_b'], wts['router_hi'], wts['router_lo'])


def _route(logits, wts, valid):
    i32 = jnp.int32
    lc = logits[:, :MOE_GROUPS] + wts['router_coarse_b']
    grp = jnp.argmax(lc, axis=-1).astype(i32)
    gate_c = jnp.max(jax.nn.softmax(lc, axis=-1), axis=-1)
    lf = (logits[:, MOE_GROUPS:MOE_GROUPS + N_EXPERTS] + wts['router_fine_b'])
    lf = lf.reshape(-1, MOE_GROUPS, EXPERTS_PER_GROUP)
    sel = grp[:, None, None] == jnp.arange(MOE_GROUPS, dtype=i32)[None, :, None]
    lf = jnp.sum(jnp.where(sel, lf, 0.0), axis=1)
    lane = jnp.arange(EXPERTS_PER_GROUP, dtype=i32)[None, :]
    i1 = jnp.argmax(lf, axis=-1).astype(i32)
    v1 = jnp.max(lf, axis=-1)
    rest = jnp.where(lane == i1[:, None], -jnp.inf, lf)
    i2 = jnp.argmax(rest, axis=-1).astype(i32)
    v2 = jnp.max(rest, axis=-1)
    top_v = jnp.stack([v1, v2], axis=-1)
    top_i = jnp.stack([i1, i2], axis=-1)
    w = gate_c[:, None] * jax.nn.softmax(top_v, axis=-1)
    expert = grp[:, None] * EXPERTS_PER_GROUP + top_i
    expert = jnp.where(valid[:, None], expert, N_EXPERTS)
    w = jnp.where(valid[:, None], w, 0.0)
    n_tok = logits.shape[0]
    n_asg = 2 * n_tok
    e_flat = expert.reshape(-1)
    order = jnp.argsort(e_flat).astype(i32)
    inv = jnp.argsort(order).astype(i32)
    bounds = jnp.sum(e_flat[None, :] < jnp.arange(N_EXPERTS + 1, dtype=i32)[:, None], axis=1, dtype=i32)
    start = bounds[:N_EXPERTS]
    counts = bounds[1:] - start
    padded = (counts + MOE_ROWS - 1) // MOE_ROWS * MOE_ROWS
    pend = jnp.cumsum(padded)
    pstart = pend - padded
    n_blocks = -(-n_asg // MOE_ROWS) + N_EXPERTS
    n_rows = n_blocks * MOE_ROWS
    n_used = pend[-1] // MOE_ROWS
    blk = jnp.minimum(jnp.arange(n_blocks, dtype=i32), n_used - 1)
    blk_exp = jnp.sum(pend[None, :] <= (blk * MOE_ROWS)[:, None], axis=1, dtype=i32)
    blk_exp = jnp.minimum(blk_exp, N_EXPERTS - 1)
    experts = jnp.arange(N_EXPERTS, dtype=i32)

    def lookup(table, idx):
        return jnp.sum(jnp.where(idx[..., None] == experts, table, 0), axis=-1)

    pos = jnp.where(e_flat < N_EXPERTS, lookup(pstart - start, e_flat) + inv, 0)
    d = jnp.arange(n_rows, dtype=i32).reshape(n_blocks, MOE_ROWS)
    k = d - lookup(pstart, blk_exp)[:, None]
    src = jnp.clip(k + lookup(start, blk_exp)[:, None], 0, n_asg - 1)
    row_tok = jnp.where((k < lookup(counts, blk_exp)[:, None]) & (d < pend[-1]), order[src] // 2, 0)
    return row_tok, pos, w, blk_exp, n_used.reshape(1).astype(i32), n_blocks


def _row_gather_start(src_hbm, idx_ref, n_rows, dst, sem, stride=1, offset=0):
    def issue(r, carry):
        row = idx_ref[0, 0, stride * r + offset]
        pltpu.make_async_copy(src_hbm.at[pl.ds(row, 1)], dst.at[pl.ds(r, 1)], sem).start()
        return carry

    lax.fori_loop(0, n_rows, issue, 0, unroll=8)


def _row_gather_wait(src_hbm, n_rows, dst, sem):
    def wait(r, carry):
        pltpu.make_async_copy(src_hbm.at[pl.ds(0, 1)], dst.at[pl.ds(r, 1)], sem).wait()
        return carry

    lax.fori_loop(0, n_rows, wait, 0, unroll=8)


def _expert_kernel(nused_ref, bexp_ref, idx_ref, nidx_ref, x_hbm, wg_ref, wu_ref, wd_ref, o_ref, buf, sem):
    i = pl.program_id(0)
    n_used = nused_ref[0]
    slot = i % 2

    @pl.when((i == 0) & (n_used > 0))
    def _():
        _row_gather_start(x_hbm, idx_ref, MOE_ROWS, buf.at[0], sem.at[0])

    @pl.when(i + 1 < n_used)
    def _():
        _row_gather_start(x_hbm, nidx_ref, MOE_ROWS, buf.at[1 - slot], sem.at[1 - slot])

    @pl.when(i < n_used)
    def _():
        _row_gather_wait(x_hbm, MOE_ROWS, buf.at[slot], sem.at[slot])
        x = _from_slabs(buf.at[slot], buf.shape[2]).astype(BF16)
        hb = jax.nn.silu(_dot(x, wg_ref[0])) * _dot(x, wu_ref[0])
        _to_slabs(o_ref, _dot(hb.astype(BF16), wd_ref[0]))

    @pl.when(i >= n_used)
    def _():
        o_ref[...] = jnp.zeros_like(o_ref)


def _moe_experts(h1s, row_tok, blk_exp, n_used, n_blocks, wts):
    slabs = h1s.shape[1]
    d = slabs * 128
    idx = row_tok.reshape(n_blocks, 1, MOE_ROWS)
    last = n_blocks - 1
    smem = pltpu.SMEM
    return pl.pallas_call(
        _expert_kernel,
        grid_spec=pltpu.PrefetchScalarGridSpec(
            num_scalar_prefetch=2,
            grid=(n_blocks,),
            in_specs=[pl.BlockSpec((1, 1, MOE_ROWS), lambda i, nu, be: (i, 0, 0), memory_space=smem),
                      pl.BlockSpec((1, 1, MOE_ROWS), lambda i, nu, be: (jnp.minimum(i + 1, last), 0, 0),
                                   memory_space=smem),
                      pl.BlockSpec(memory_space=pl.ANY),
                      pl.BlockSpec((1, d, D_EXPERT), lambda i, nu, be: (be[i], 0, 0)),
                      pl.BlockSpec((1, d, D_EXPERT), lambda i, nu, be: (be[i], 0, 0)),
                      pl.BlockSpec((1, D_EXPERT, d), lambda i, nu, be: (be[i], 0, 0))],
            out_specs=pl.BlockSpec((MOE_ROWS, slabs, 128), lambda i, nu, be: (i, 0, 0)),
            scratch_shapes=[pltpu.VMEM((2, MOE_ROWS, slabs, 128), F32), pltpu.SemaphoreType.DMA((2,))],
        ),
        out_shape=jax.ShapeDtypeStruct((n_blocks * MOE_ROWS, slabs, 128), F32),
        compiler_params=_cparams("arbitrary"),
        name="moe_experts",
    )(n_used, blk_exp, idx, idx, h1s, wts['exp_w_gate'], wts['exp_w_up'], wts['exp_w_down'])


def _combine_kernel(pos_ref, npos_ref, eo_hbm, h_ref, w_ref, g_ref, b_ref, o_ref, buf, sem, *, n_steps, bm):
    i = pl.program_id(0)
    slot = i % 2

    def start(p_ref, s):
        for k in range(2):
            _row_gather_start(eo_hbm, p_ref, bm, buf.at[s, k], sem.at[s], stride=2, offset=k)

    @pl.when(i == 0)
    def _():
        start(pos_ref, 0)

    @pl.when(i + 1 < n_steps)
    def _():
        start(npos_ref, 1 - slot)

    for k in range(2):
        _row_gather_wait(eo_hbm, bm, buf.at[slot, k], sem.at[slot])
    w = w_ref[...]
    slabs = buf.shape[3]
    moe = (w[:, 0:1] * _from_slabs(buf.at[slot, 0], slabs)
           + w[:, 1:2] * _from_slabs(buf.at[slot, 1], slabs))
    o_ref[0] = _layernorm(ALPHA * h_ref[...] + moe, g_ref[...], b_ref[...])


def _moe_combine(eo, pos, w, h1, wts, bsz, t, t_pad):
    n, d = h1.shape
    bm = SEQ_FRONT
    per_seq = -(-t // bm)
    n_steps = bsz * per_seq
    blocks_per_seq = t_pad // bm

    def src(i):
        return (i // per_seq) * blocks_per_seq + 1 + i % per_seq

    def nxt(i):
        return src(jnp.minimum(i + 1, n_steps - 1))

    vec = pl.BlockSpec((1, d), lambda i: (0, 0))
    smem = pltpu.SMEM
    pos3 = pos.reshape(n // bm, 1, 2 * bm)
    return pl.pallas_call(
        functools.partial(_combine_kernel, n_steps=n_steps, bm=bm),
        grid=(n_steps,),
        in_specs=[pl.BlockSpec((1, 1, 2 * bm), lambda i: (src(i), 0, 0), memory_space=smem),
                  pl.BlockSpec((1, 1, 2 * bm), lambda i: (nxt(i), 0, 0), memory_space=smem),
                  pl.BlockSpec(memory_space=pl.ANY),
                  pl.BlockSpec((bm, d), lambda i: (src(i), 0)),
                  pl.BlockSpec((bm, 2), lambda i: (src(i), 0)), vec, vec],
        out_specs=pl.BlockSpec((1, bm, d), lambda i: (i // per_seq, i % per_seq, 0)),
        out_shape=jax.ShapeDtypeStruct((bsz, t, d), F32),
        scratch_shapes=[pltpu.VMEM((2, 2, bm) + eo.shape[1:], F32), pltpu.SemaphoreType.DMA((2,))],
        compiler_params=_cparams("arbitrary"),
        name="moe_combine",
    )(pos3, pos3, eo, h1, w, wts['ln2_g'], wts['ln2_b'])


def _prepare_weights(p):
    l = 0
    w_in = p['w_in'][l]
    c0 = S5_WIDTH
    c1 = c0 + 3 * RW_WIDTH + 2 * RW_DECAY_LORA + 2 * RW_ICLR_LORA + RW_GATE_LORA
    gpad = RW_GATE_PAD - RW_GATE_LORA
    wts = {}
    wts['w_u'] = w_in[:, :c0].astype(BF16)
    wts['w_rw'] = jnp.pad(w_in[:, c0:c1], ((0, 0), (0, gpad))).astype(BF16)
    wts['w_gate'] = w_in[:, c1:].astype(BF16)
    wts['mu'] = jnp.pad(p['shift_mu'][l], ((0, 0), (0, gpad)))
    z = jnp.zeros((RW_DECAY_LORA, RW_WIDTH), F32)
    wts['w2'] = jnp.block([[p['rw_w2'][l, 0], z], [z, p['rw_w2'][l, 1]]]).astype(BF16)
    wts['a2'] = jnp.block([[p['rw_a2'][l, 0], z], [z, p['rw_a2'][l, 1]]]).astype(BF16)
    wts['g2'] = jnp.pad(p['rw_g2'][l], ((0, gpad), (0, 0))).astype(BF16)
    wts['w0'] = p['rw_w0'][l]
    wts['a0'] = p['rw_a0'][l]
    wts['k_k'] = p['rw_k_k'][l].reshape(1, -1)
    wts['k_a'] = p['rw_k_a'][l].reshape(1, -1)
    wts['r_k'] = p['rw_r_k'][l].reshape(1, -1)
    wts['lnx_g'] = p['rw_lnx_g'][l].reshape(1, -1)
    wts['lnx_b'] = p['rw_lnx_b'][l].reshape(1, -1)
    head = jnp.arange(RW_WIDTH) // RW_HEAD
    e = (head[:, None] == jnp.arange(RW_HEADS)[None, :]).astype(BF16)
    wts['head_e'] = e
    wts['head_et'] = e.T
    wts['s5'] = _s5_block_operators(*_s5_matrices(
        p['s5_B_re'][l], p['s5_B_im'][l], p['s5_A_re'][l], p['s5_A_im'][l],
        p['s5_log_dt'][l], p['s5_C_re'][l], p['s5_C_im'][l]))
    wts['s5_D'] = p['s5_D'][l]
    wts['glu_w'] = p['s5_glu_w'][l].astype(BF16)
    wts['glu_b'] = p['s5_glu_b'][l]
    wts['proj_s5'] = p['proj_s5'][l].astype(BF16)
    wts['proj_rwkv'] = p['proj_rwkv'][l].astype(BF16)
    wts['gate_b'] = p['gate_b'][l]
    wts['w_out'] = p['w_out'][l].astype(BF16)
    wts['ln1_g'] = p['ln1_g'][l].reshape(1, -1)
    wts['ln1_b'] = p['ln1_b'][l].reshape(1, -1)
    router = jnp.concatenate([p['router_coarse'][l], p['router_fine'][l]], axis=1)
    router = jnp.pad(router, ((0, 0), (0, ROUTER_PAD - router.shape[1])))
    wts['router_hi'], wts['router_lo'] = _split(router)
    wts['router_coarse_b'] = p['router_coarse_b'][l]
    wts['router_fine_b'] = p['router_fine_b'][l]
    wts['exp_w_gate'] = p['exp_w_gate'][l].astype(BF16)
    wts['exp_w_up'] = p['exp_w_up'][l].astype(BF16)
    wts['exp_w_down'] = p['exp_w_down'][l].astype(BF16)
    wts['ln2_g'] = p['ln2_g'][l].reshape(1, -1)
    wts['ln2_b'] = p['ln2_b'][l].reshape(1, -1)
    return wts


def _encode(x, p, wts):
    bsz, t, d = x.shape
    t_lo = SEQ_FRONT - N_META
    t_hi = SEQ_FRONT + t
    t_pad = -(-t_hi // SEQ_ALIGN) * SEQ_ALIGN
    n = bsz * t_pad
    h0, h0b = _ln_in(x, p['meta'], p['ln_in_g'], p['ln_in_b'], t_pad)
    h0 = h0.reshape(n, d)
    h0b = h0b.reshape(n, d)
    bm = _row_block(t_pad, 1024)
    u = _mm(h0b, wts['w_u'], bm, S5_WIDTH, F32, "proj_s5_in")
    rw = _mm(h0b, wts['w_rw'], bm, RW_COLS // 4, F32, "proj_rwkv_in")
    y_ssm = _s5_ssm(u.reshape(bsz, t_pad, S5_WIDTH), wts['s5']).reshape(n, S5_WIDTH)
    s5_out = _s5_post(y_ssm, u, wts['s5_D'], wts['glu_w'], wts['glu_b'], bm)
    r, kk, v, g, bv, lw, kd, bb = _rwkv_prep(rw.reshape(bsz, t_pad, RW_COLS), wts, t_lo, t_hi)
    yf, yb = _rwkv_scan(r, kk, v, lw, kd, bb)
    rw_out = _rwkv_post(yf.reshape(n, -1), yb.reshape(n, -1), bv.reshape(n, -1), g.reshape(n, -1), wts, bm)
    merged = _merge(h0b, s5_out, rw_out, wts, bm)
    h1, h1s, logits = _out_proj(merged, h0, wts, _row_block(t_pad, 512))
    seq_pos = jnp.arange(n, dtype=jnp.int32) % t_pad
    valid = (seq_pos >= t_lo) & (seq_pos < t_hi)
    row_tok, pos, w, blk_exp, n_used, n_blocks = _route(logits, wts, valid)
    eo = _moe_experts(h1s, row_tok, blk_exp, n_used, n_blocks, wts)
    return _moe_combine(eo, pos, w, h1, wts, bsz, t, t_pad)


def kernel(x_prompt, x_sample, meta, ln_in_g, ln_in_b, w_in, shift_mu, s5_B_re, s5_B_im, s5_A_re, s5_A_im, s5_log_dt, s5_C_re, s5_C_im, s5_D, s5_glu_w, s5_glu_b, rw_w0, rw_w2, rw_a0, rw_a2, rw_g2, rw_k_k, rw_k_a, rw_r_k, rw_lnx_g, rw_lnx_b, proj_s5, proj_rwkv, gate_b, w_out, ln1_g, ln1_b, router_coarse, router_coarse_b, router_fine, router_fine_b, exp_w_gate, exp_w_up, exp_w_down, ln2_g, ln2_b):
    p = {
        'meta': meta, 'ln_in_g': ln_in_g, 'ln_in_b': ln_in_b, 'w_in': w_in, 'shift_mu': shift_mu,
        's5_B_re': s5_B_re, 's5_B_im': s5_B_im, 's5_A_re': s5_A_re, 's5_A_im': s5_A_im,
        's5_log_dt': s5_log_dt, 's5_C_re': s5_C_re, 's5_C_im': s5_C_im, 's5_D': s5_D,
        's5_glu_w': s5_glu_w, 's5_glu_b': s5_glu_b,
        'rw_w0': rw_w0, 'rw_w2': rw_w2, 'rw_a0': rw_a0, 'rw_a2': rw_a2, 'rw_g2': rw_g2,
        'rw_k_k': rw_k_k, 'rw_k_a': rw_k_a, 'rw_r_k': rw_r_k, 'rw_lnx_g': rw_lnx_g, 'rw_lnx_b': rw_lnx_b,
        'proj_s5': proj_s5, 'proj_rwkv': proj_rwkv, 'gate_b': gate_b, 'w_out': w_out,
        'ln1_g': ln1_g, 'ln1_b': ln1_b,
        'router_coarse': router_coarse, 'router_coarse_b': router_coarse_b,
        'router_fine': router_fine, 'router_fine_b': router_fine_b,
        'exp_w_gate': exp_w_gate, 'exp_w_up': exp_w_up, 'exp_w_down': exp_w_down,
        'ln2_g': ln2_g, 'ln2_b': ln2_b,
    }
    wts = _prepare_weights(p)
    return (_encode(x_prompt, p, wts), _encode(x_sample, p, wts))
```

```python
import functools
import math

import jax
import jax.numpy as jnp
from jax import lax
from jax.experimental import pallas as pl
from jax.experimental.pallas import tpu as pltpu

F32 = jnp.float32
BF16 = jnp.bfloat16

D_MODEL = 2048
N_META = 16
S5_WIDTH = 1024
S5_GROUP = 16
S5_GROUPS = 64
S5_STATE = 64
S5_GB = 8
S5_CHUNK = 16
RW_WIDTH = 1024
RW_HEAD = 64
RW_HEADS = 16
RW_DECAY_LORA = 64
RW_ICLR_LORA = 64
RW_GATE_LORA = 160
RW_GATE_PAD = 256
RW_COLS = 3 * RW_WIDTH + 2 * RW_DECAY_LORA + 2 * RW_ICLR_LORA + RW_GATE_PAD
RW_CHUNK = 64
RW_HEADS_PER_STEP = 2
MOE_GROUPS = 4
EXPERTS_PER_GROUP = 8
N_EXPERTS = 32
D_EXPERT = 512
MOE_ROWS = 256
ROUTER_PAD = 128
DEPTH = 1
ALPHA = (2 * DEPTH) ** 0.25
LN_EPS = 1e-5
GN_EPS = 64e-5
SEQ_ALIGN = 64
SEQ_FRONT = 64
VMEM_LIMIT = 56 * 1024 * 1024


def _cparams(*sem):
    return pltpu.CompilerParams(dimension_semantics=sem, vmem_limit_bytes=VMEM_LIMIT)


def _row_block(t_pad, cap, mult=8):
    best = mult
    for d in range(mult, cap + 1, mult):
        if t_pad % d == 0:
            best = d
    return best


def _dot(a, b):
    return jnp.dot(a, b, preferred_element_type=F32)


def _dot_nt(a, b):
    return lax.dot_general(a, b, (((1,), (1,)), ((), ())), preferred_element_type=F32)


def _dot_tn(a, b):
    return lax.dot_general(a, b, (((0,), (0,)), ((), ())), preferred_element_type=F32)


def _split(x):
    hi = x.astype(BF16)
    lo = (x - hi.astype(F32)).astype(BF16)
    return hi, lo


def _layernorm(x, g, b):
    mu = jnp.mean(x, axis=-1, keepdims=True)
    xc = x - mu
    var = jnp.mean(xc * xc, axis=-1, keepdims=True)
    return xc * lax.rsqrt(var + LN_EPS) * g + b


def _ln_in_kernel(x_ref, m_ref, g_ref, b_ref, of_ref, ob_ref, *, t):
    j = pl.program_id(1)
    blk = SEQ_FRONT

    @pl.when(j == 0)
    def _():
        ym = _layernorm(m_ref[...], g_ref[...], b_ref[...])
        y = jnp.concatenate([jnp.zeros((blk - N_META, ym.shape[1]), F32), ym], axis=0)
        of_ref[0] = y
        ob_ref[0] = y.astype(BF16)

    @pl.when(j > 0)
    def _():
        y = _layernorm(x_ref[0], g_ref[...], b_ref[...])
        pos = (j - 1) * blk + lax.broadcasted_iota(jnp.int32, (blk, 1), 0)
        y = jnp.where(pos < t, y, 0.0)
        of_ref[0] = y
        ob_ref[0] = y.astype(BF16)


def _ln_in(x, meta, g, b, t_pad):
    bsz, t, d = x.shape
    blk = SEQ_FRONT
    row = pl.BlockSpec((1, blk, d), lambda i, j: (i, j, 0))
    vec = pl.BlockSpec((1, d), lambda i, j: (0, 0))
    return pl.pallas_call(
        functools.partial(_ln_in_kernel, t=t),
        grid=(bsz, t_pad // blk),
        in_specs=[pl.BlockSpec((1, blk, d), lambda i, j: (i, jnp.maximum(j - 1, 0), 0)),
                  pl.BlockSpec((N_META, d), lambda i, j: (0, 0)), vec, vec],
        out_specs=[row, row],
        out_shape=[jax.ShapeDtypeStruct((bsz, t_pad, d), F32), jax.ShapeDtypeStruct((bsz, t_pad, d), BF16)],
        compiler_params=_cparams("parallel", "parallel"),
        name="ln_in",
    )(x, meta, g.reshape(1, d), b.reshape(1, d))


def _mm_kernel(x_ref, w_ref, o_ref):
    o_ref[...] = _dot(x_ref[...], w_ref[...]).astype(o_ref.dtype)


def _mm(x, w, bm, bn, out_dtype, name):
    n, k = x.shape
    m = w.shape[1]
    return pl.pallas_call(
        _mm_kernel,
        grid=(m // bn, n // bm),
        in_specs=[pl.BlockSpec((bm, k), lambda j, i: (i, 0)),
                  pl.BlockSpec((k, bn), lambda j, i: (0, j))],
        out_specs=pl.BlockSpec((bm, bn), lambda j, i: (i, j)),
        out_shape=jax.ShapeDtypeStruct((n, m), out_dtype),
        compiler_params=_cparams("parallel", "parallel"),
        name=name,
    )(x, w)


def _s5_matrices(b_re, b_im, a_re, a_im, log_dt, c_re, c_im):
    L = S5_CHUNK
    dt = jnp.exp(log_dt)[..., None]
    mag = jnp.exp(a_re * dt)
    abr = mag * jnp.cos(a_im * dt)
    abi = mag * jnp.sin(a_im * dt)
    den = a_re * a_re + a_im * a_im
    nr = abr - 1.0
    cr = (nr * a_re + abi * a_im) / den
    ci = (abi * a_re - nr * a_im) / den
    bbr = cr[..., None] * b_re - ci[..., None] * b_im
    bbi = cr[..., None] * b_im + ci[..., None] * b_re
    tau = jnp.arange(L + 1, dtype=F32)[:, None, None, None]
    pmag = jnp.exp(tau * a_re * dt)
    pr = pmag * jnp.cos(tau * a_im * dt)
    pi = pmag * jnp.sin(tau * a_im * dt)
    wr = pr[..., None] * bbr - pi[..., None] * bbi
    wi = pr[..., None] * bbi + pi[..., None] * bbr
    kern = (jnp.einsum('zgop,tzgpi->tzgoi', c_re, wr)
            - jnp.einsum('zgop,tzgpi->tzgoi', c_im, wi))
    s = jnp.arange(L)[:, None]
    t = jnp.arange(L)[None, :]
    lag = t - s
    kf = jnp.where((lag >= 0)[..., None, None, None], kern[jnp.clip(lag, 0, L), 0], 0.0)
    kb = jnp.where((lag <= 0)[..., None, None, None], kern[jnp.clip(-lag, 0, L), 1], 0.0)
    toep = (kf + kb).transpose(2, 0, 4, 1, 3).reshape(S5_GROUPS, L * S5_GROUP, L * S5_GROUP)
    wf_r, wf_i = wr[::-1][1:, 0], wi[::-1][1:, 0]
    wb_r, wb_i = wr[:L, 1], wi[:L, 1]
    bmat = jnp.concatenate([wf_r, wb_r, wf_i, wb_i], axis=2)
    bmat = bmat.transpose(1, 0, 3, 2).reshape(S5_GROUPS, L * S5_GROUP, 4 * S5_STATE)
    pf_r, pf_i = pr[1:, 0], pi[1:, 0]
    pb_r, pb_i = pr[::-1][:L, 1], pi[::-1][:L, 1]
    c0r, c0i, c1r, c1i = c_re[0], c_im[0], c_re[1], c_im[1]

    def cpow(cre, cim, p_r, p_i):
        re = cre[None] * p_r[:, :, None, :] - cim[None] * p_i[:, :, None, :]
        im = cre[None] * p_i[:, :, None, :] + cim[None] * p_r[:, :, None, :]
        return re, -im

    f_re, f_im = cpow(c0r, c0i, pf_r, pf_i)
    g_re, g_im = cpow(c1r, c1i, pb_r, pb_i)
    cmat = jnp.concatenate([f_re, g_re, f_im, g_im], axis=3)
    cmat = cmat.transpose(1, 3, 0, 2).reshape(S5_GROUPS, 4 * S5_STATE, L * S5_GROUP)
    lam_re = jnp.concatenate([pr[L, 0], pr[L, 1]], axis=-1)[:, None, :]
    lam_im = jnp.concatenate([pi[L, 0], pi[L, 1]], axis=-1)[:, None, :]
    return bmat, toep, cmat, lam_re, lam_im


def _s5_expand_kernel(src_ref, e_ref, o_ref, *, pieces):
    j = pl.program_id(1)
    x = _dot(src_ref[0].astype(BF16), e_ref[0]).astype(BF16)
    for src, rows, dst, step in pieces:
        o_ref[0, pl.ds(pl.multiple_of(dst + j * step, 16), rows), :] = x[src:src + rows, :]


def _s5_expand(per_group, expand, pieces):
    groups, rows, cols = per_group.shape
    gb, _, wide = expand.shape
    return pl.pallas_call(
        functools.partial(_s5_expand_kernel, pieces=pieces),
        grid=(groups // gb, gb),
        in_specs=[pl.BlockSpec((1, rows, cols), lambda g, j: (g * gb + j, 0, 0)),
                  pl.BlockSpec((1, cols, wide), lambda g, j: (j, 0, 0))],
        out_specs=pl.BlockSpec((1, gb * rows, wide), lambda g, j: (g, 0, 0)),
        out_shape=jax.ShapeDtypeStruct((groups // gb, gb * rows, wide), BF16),
        compiler_params=_cparams("arbitrary", "arbitrary"),
        name="s5_expand",
    )(per_group, expand)


def _s5_block_operators(bmat, toep, cmat, lam_re, lam_im):
    nb, gb, L, c, p = S5_GROUPS // S5_GB, S5_GB, S5_CHUNK, S5_GROUP, S5_STATE
    quarters = (0, 2, 1, 3)
    wide = L * gb * c
    r = jnp.arange(L * c)[None, :, None]
    col = jnp.arange(wide)[None, None, :]
    j = jnp.arange(gb)[:, None, None]
    e_tok = ((r // c == col // (gb * c)) & ((col // c) % gb == j) & (r % c == col % c)).astype(BF16)
    k_of_col = jnp.array(quarters)[col // (gb * p)]
    e_state = ((r // p == k_of_col) & ((col // p) % gb == j) & (r % p == col % p)).astype(BF16)
    tok_rows = tuple((s * c, c, s * gb * c, c) for s in range(L))
    state_rows = tuple((quarters[k] * p, p, k * gb * p, p) for k in range(4))
    wb = _s5_expand(bmat, e_state, tok_rows)
    wt = _s5_expand(toep, e_tok, tok_rows)
    wc = _s5_expand(cmat, e_tok, state_rows)
    ar_f = lam_re[:, 0, :p].reshape(nb, gb * p)
    ar_b = lam_re[:, 0, p:].reshape(nb, gb * p)
    ai_f = lam_im[:, 0, :p].reshape(nb, gb * p)
    ai_b = lam_im[:, 0, p:].reshape(nb, gb * p)
    zero = jnp.zeros_like(ar_f)
    a1 = jnp.stack([ar_f, ar_f, ar_b, ar_b, zero, zero, zero, zero], axis=1)
    a2 = jnp.stack([-ai_f, ai_f, -ai_b, ai_b, zero, zero, zero, zero], axis=1)
    return wb.astype(BF16), wt.astype(BF16), wc.astype(BF16), a1, a2


def _s5_kernel(u_ref, wb_ref, wt_ref, wc_ref, a1_ref, a2_ref, y_ref, s_scr, xf_scr, xb_scr, *, n_chunks):
    L = S5_CHUNK
    C = n_chunks
    sw = S5_GB * S5_STATE
    u8 = jnp.concatenate([u_ref[0, pl.ds(s, C, stride=L), :] for s in range(L)], axis=1).astype(BF16)
    s_all = _dot(u8, wb_ref[0])
    for k in range(4):
        s_scr[:, k, :] = s_all[:, k * sw:(k + 1) * sw]
    s_scr[:, 4:8, :] = jnp.zeros((C, 4, sw), F32)
    a1 = a1_ref[0]
    a2 = a2_ref[0]
    row = lax.broadcasted_iota(jnp.int32, (8, sw), 0)
    even = (row % 2) == 0
    is_fwd = row < 2

    def step(c, x):
        rc = C - 1 - c
        xf_scr[c] = x
        xb_scr[rc] = x
        s = jnp.where(is_fwd, s_scr[c], s_scr[rc])
        swapped = jnp.where(even, pltpu.roll(x, 7, 0), pltpu.roll(x, 1, 0))
        return a1 * x + a2 * swapped + s

    lax.fori_loop(0, C, step, jnp.zeros((8, sw), F32))
    x_in = jnp.concatenate([xf_scr[:, 0, :], xf_scr[:, 1, :], xb_scr[:, 2, :], xb_scr[:, 3, :]],
                           axis=1).astype(BF16)
    y8 = _dot(u8, wt_ref[0]) + _dot(x_in, wc_ref[0])
    for t in range(L):
        y_ref[0, pl.ds(t, C, stride=L), :] = y8[:, t * 128:(t + 1) * 128]


def _s5_ssm(u3, ops):
    wb, wt, wc, a1, a2 = ops
    bsz, t_pad, width = u3.shape
    n_chunks = t_pad // S5_CHUNK
    nb = S5_GROUPS // S5_GB
    lanes = S5_GB * S5_GROUP
    sw = S5_GB * S5_STATE
    blk = pl.BlockSpec((1, t_pad, lanes), lambda g, b: (b, 0, g))
    mat = pl.BlockSpec((1,) + wb.shape[1:], lambda g, b: (g, 0, 0), pipeline_mode=pl.Buffered(1))
    vec = pl.BlockSpec((1, 8, sw), lambda g, b: (g, 0, 0))
    return pl.pallas_call(
        functools.partial(_s5_kernel, n_chunks=n_chunks),
        grid=(nb, bsz),
        in_specs=[blk, mat, mat, mat, vec, vec],
        out_specs=blk,
        out_shape=jax.ShapeDtypeStruct(u3.shape, F32),
        scratch_shapes=[pltpu.VMEM((n_chunks, 8, sw), F32)] * 3,
        compiler_params=_cparams("arbitrary", "arbitrary"),
        name="s5_ssm",
    )(u3, wb, wt, wc, a1, a2)


def _s5_post_kernel(y_ref, u_ref, d_ref, w_ref, b_ref, o_ref):
    y = y_ref[...] + u_ref[...] * d_ref[...]
    act = y * (0.5 * (1.0 + jnp.tanh(math.sqrt(2.0 / math.pi) * (y + 0.044715 * (y * y * y)))))
    z = _dot(act.astype(BF16), w_ref[...]) + b_ref[...]
    o_ref[...] = (act * jax.nn.sigmoid(z)).astype(o_ref.dtype)


def _s5_post(y, u, d_skip, glu_w, glu_b, bm):
    n = y.shape[0]
    row = pl.BlockSpec((bm, S5_WIDTH), lambda i: (i, 0))
    vec = pl.BlockSpec((1, S5_WIDTH), lambda i: (0, 0))
    return pl.pallas_call(
        _s5_post_kernel,
        grid=(n // bm,),
        in_specs=[row, row, vec, pl.BlockSpec((S5_WIDTH, S5_WIDTH), lambda i: (0, 0)), vec],
        out_specs=row,
        out_shape=jax.ShapeDtypeStruct((n, S5_WIDTH), BF16),
        compiler_params=_cparams("parallel"),
        name="s5_post",
    )(y, u, d_skip.reshape(1, -1), glu_w, glu_b.reshape(1, -1))


def _head_sum(x, e_ref, et_ref):
    hi, lo = _split(x)
    s = _dot(hi, e_ref[...]) + _dot(lo, e_ref[...])
    shi, slo = _split(s)
    return _dot(shi, et_ref[...]) + _dot(slo, et_ref[...])


def _rwkv_prep_kernel(cur_ref, prev_ref, next_ref, mu_ref, w2_ref, a2_ref, g2_ref, w0_ref, a0_ref,
                      kk_ref, ka_ref, rk_ref, e_ref, et_ref,
                      r_o, kk_o, v_o, g_o, bv_o, lw_o, kd_o, bb_o, *, t_lo, t_hi, bm):
    j = pl.program_id(1)
    p = cur_ref[0]
    row = lax.broadcasted_iota(jnp.int32, (bm, 1), 0)
    prev_row = jnp.where(j > 0, prev_ref[0, 7:8, :], 0.0)
    next_row = jnp.where(j < pl.num_programs(1) - 1, next_ref[0, 0:1, :], 0.0)
    prev = jnp.where(row == 0, prev_row, pltpu.roll(p, 1, 0))
    nxt = jnp.where(row == bm - 1, next_row, pltpu.roll(p, bm - 1, 0))
    xs = p + mu_ref[0:1, :] * (prev - p) + mu_ref[1:2, :] * (nxt - p)
    w = RW_WIDTH
    r = xs[:, 0:w]
    k = xs[:, w:2 * w]
    v = xs[:, 2 * w:3 * w]
    lw = xs[:, 3 * w:3 * w + 128]
    la = xs[:, 3 * w + 128:3 * w + 256]
    lg = xs[:, 3 * w + 256:]
    w_log = _dot(jnp.tanh(lw).astype(BF16), w2_ref[...])
    a_lin = _dot(la.astype(BF16), a2_ref[...])
    g = _dot(jax.nn.sigmoid(lg).astype(BF16), g2_ref[...])
    kk = k * kk_ref[...]
    n2 = _head_sum(kk * kk, e_ref, et_ref)
    kk = kk / jnp.maximum(jnp.sqrt(n2), 1e-12)
    pos = j * bm + row
    valid = (pos >= t_lo) & (pos < t_hi)
    v = jnp.where(valid, v, 0.0)
    kd_sum = jnp.zeros_like(k)
    for z in range(2):
        wl = w_log[:, z * w:(z + 1) * w] + w0_ref[z:z + 1, :]
        lw_o[z, 0] = -math.exp(-0.5) * jax.nn.sigmoid(wl)
        a = jax.nn.sigmoid(a_lin[:, z * w:(z + 1) * w] + a0_ref[z:z + 1, :])
        kd = k * (1.0 + (a - 1.0) * ka_ref[...])
        kd_o[z, 0] = kd.astype(BF16)
        bb_o[z, 0] = (kk * a).astype(BF16)
        kd_sum = kd_sum + kd
    bonus = _head_sum(r * kd_sum * rk_ref[...], e_ref, et_ref)
    r_o[0] = r.astype(BF16)
    kk_o[0] = kk.astype(BF16)
    v_o[0] = v.astype(BF16)
    g_o[0] = g.astype(BF16)
    bv_o[0] = (bonus * v).astype(BF16)


def _rwkv_prep(rw3, wts, t_lo, t_hi):
    bsz, t_pad, _ = rw3.shape
    bm = _row_block(t_pad, 320, 64)
    nb8 = bm // 8
    last8 = t_pad // 8 - 1
    w = RW_WIDTH
    cur = pl.BlockSpec((1, bm, RW_COLS), lambda b, j: (b, j, 0))
    prev = pl.BlockSpec((1, 8, RW_COLS), lambda b, j: (b, jnp.maximum(j * nb8 - 1, 0), 0))
    nxt = pl.BlockSpec((1, 8, RW_COLS), lambda b, j: (b, jnp.minimum((j + 1) * nb8, last8), 0))

    def full(a):
        return pl.BlockSpec(a.shape, lambda b, j: (0,) * a.ndim)

    shared = pl.BlockSpec((1, bm, w), lambda b, j: (b, j, 0))
    per_dir = pl.BlockSpec((2, 1, bm, w), lambda b, j: (0, b, j, 0))
    consts = [wts['mu'], wts['w2'], wts['a2'], wts['g2'], wts['w0'], wts['a0'],
              wts['k_k'], wts['k_a'], wts['r_k'], wts['head_e'], wts['head_et']]
    sds = jax.ShapeDtypeStruct
    return pl.pallas_call(
        functools.partial(_rwkv_prep_kernel, t_lo=t_lo, t_hi=t_hi, bm=bm),
        grid=(bsz, t_pad // bm),
        in_specs=[cur, prev, nxt] + [full(a) for a in consts],
        out_specs=[shared] * 5 + [per_dir] * 3,
        out_shape=[sds((bsz, t_pad, w), BF16)] * 5
        + [sds((2, bsz, t_pad, w), F32), sds((2, bsz, t_pad, w), BF16), sds((2, bsz, t_pad, w), BF16)],
        compiler_params=_cparams("parallel", "parallel"),
        name="rwkv_prep",
    )(rw3, rw3, rw3, *consts)


def _rwkv_chunk(fwd, r_ref, kk_ref, v_ref, lw_ref, kd_ref, bb_ref, y_ref, st_ref, z):
    L = RW_CHUNK
    hd = RW_HEAD
    gw = RW_HEADS_PER_STEP * hd
    n_groups = RW_WIDTH // gw
    row = lax.broadcasted_iota(jnp.int32, (L, L), 0)
    col = lax.broadcasted_iota(jnp.int32, (L, L), 1)
    tri = jnp.where((col <= row) if fwd else (col >= row), 1.0, 0.0).astype(BF16)
    grow = lax.broadcasted_iota(jnp.int32, (L, gw), 0)
    gcol = lax.broadcasted_iota(jnp.int32, (L, gw), 1) % L
    incl = (gcol <= grow) if fwd else (gcol >= grow)
    strict = (gcol < grow) if fwd else (gcol > grow)
    bd_mask = jnp.where(lax.broadcasted_iota(jnp.int32, (gw, gw), 0) // hd
                        == lax.broadcasted_iota(jnp.int32, (gw, gw), 1) // hd, 1.0, 0.0).astype(BF16)

    def bd(x):
        return jnp.concatenate([x] * RW_HEADS_PER_STEP, axis=0) * bd_mask

    def stack(x):
        return jnp.concatenate([x[:, h * hd:(h + 1) * hd] for h in range(RW_HEADS_PER_STEP)], axis=0)

    lw = lw_ref[0, 0]
    lw_hi, lw_lo = _split(lw)
    c = _dot(tri, lw_hi) + _dot(tri, lw_lo)
    e = c - lw
    c_tot = c[L - 1:L, :] if fwd else c[0:1, :]
    r = r_ref[0].astype(F32)
    kk = kk_ref[0].astype(F32)
    kd = kd_ref[0, 0].astype(F32)
    bb = bb_ref[0, 0].astype(F32)
    v = v_ref[0]
    q1 = (kk * jnp.exp(e)).astype(BF16)
    q2 = (r * jnp.exp(c)).astype(BF16)
    inv = jnp.exp(-c)
    k1 = (kd * inv).astype(BF16)
    k2 = (bb * inv).astype(BF16)
    rest = jnp.exp(c_tot - c)
    k1p = (kd * rest).astype(BF16)
    k2p = (bb * rest).astype(BF16)
    dec_tot = jnp.exp(c_tot)
    def group_chain(g):
        sl = slice(g * gw, (g + 1) * gw)
        s0 = st_ref[z, g]
        s0_hi, s0_lo = _split(s0)
        lhs = jnp.concatenate([q1[:, sl], q2[:, sl]], axis=0)
        rhs = jnp.concatenate([bd(k1[:, sl]), bd(k2[:, sl]), bd(s0_hi), bd(s0_lo)], axis=0)
        m1 = _dot_nt(lhs, rhs)
        yield
        a_kd = jnp.where(strict, m1[:L, 0:gw], 0.0)
        a_b = jnp.where(strict, m1[:L, gw:2 * gw], 0.0)
        q1s = m1[:L, 2 * gw:3 * gw] + m1[:L, 3 * gw:]
        b_kd = jnp.where(incl, m1[L:, 0:gw], 0.0)
        b_b = jnp.where(incl, m1[L:, gw:2 * gw], 0.0)
        q2s = m1[L:, 2 * gw:3 * gw] + m1[L:, 3 * gw:]
        vg = v[:, sl]
        v_bd = bd(vg)
        x = q1s + _dot(a_kd.astype(BF16), v_bd)
        yield
        m = -a_b
        levels = L.bit_length() - 1
        for lvl in range(levels):
            mb = m.astype(BF16)
            if lvl < levels - 1:
                rr = _dot(mb, jnp.concatenate([bd(x.astype(BF16)), bd(mb)], axis=1))
                x = x + rr[:, :gw]
                m = rr[:, gw:]
            else:
                x = x + _dot(mb, bd(x.astype(BF16)))
            yield
        ub = x.astype(BF16)
        y = q2s + _dot(jnp.concatenate([b_kd, -b_b], axis=1).astype(BF16),
                       jnp.concatenate([v_bd, bd(ub)], axis=0))
        y_ref[0, :, sl] = y
        yield
        st_ref[z, g] = s0 * dec_tot[:, sl] + _dot_tn(
            jnp.concatenate([stack(vg), stack(ub)], axis=0),
            jnp.concatenate([bd(k1p[:, sl]), -bd(k2p[:, sl])], axis=0))

    return [group_chain(g) for g in range(n_groups)]


def _rwkv_scan_kernel(rf, kkf, vf, lwf, kdf, bbf, rb, kkb, vb, lwb, kdb, bbb, yf_ref, yb_ref, st_ref):
    @pl.when(pl.program_id(1) == 0)
    def _():
        st_ref[...] = jnp.zeros_like(st_ref)

    chains = (_rwkv_chunk(True, rf, kkf, vf, lwf, kdf, bbf, yf_ref, st_ref, 0)
              + _rwkv_chunk(False, rb, kkb, vb, lwb, kdb, bbb, yb_ref, st_ref, 1))
    while chains:
        alive = []
        for chain in chains:
            try:
                next(chain)
                alive.append(chain)
            except StopIteration:
                pass
        chains = alive


def _rwkv_scan(r, kk, v, lw, kd, bb):
    bsz, t_pad, w = r.shape
    L = RW_CHUNK
    nc = t_pad // L
    gw = RW_HEADS_PER_STEP * RW_HEAD
    grid = (bsz, nc)

    def shared(fwd):
        if fwd:
            return pl.BlockSpec((1, L, w), lambda b, j: (b, j, 0))
        return pl.BlockSpec((1, L, w), lambda b, j: (b, nc - 1 - j, 0))

    def per_dir(fwd):
        if fwd:
            return pl.BlockSpec((1, 1, L, w), lambda b, j: (0, b, j, 0))
        return pl.BlockSpec((1, 1, L, w), lambda b, j: (1, b, nc - 1 - j, 0))

    in_specs = []
    for fwd in (True, False):
        in_specs += [shared(fwd), shared(fwd), shared(fwd), per_dir(fwd), per_dir(fwd), per_dir(fwd)]
    return pl.pallas_call(
        _rwkv_scan_kernel,
        grid=grid,
        in_specs=in_specs,
        out_specs=[shared(True), shared(False)],
        out_shape=[jax.ShapeDtypeStruct((bsz, t_pad, w), F32)] * 2,
        scratch_shapes=[pltpu.VMEM((2, w // gw, RW_HEAD, gw), F32)],
        compiler_params=_cparams("parallel", "arbitrary"),
        name="rwkv_scan",
    )(r, kk, v, lw, kd, bb, r, kk, v, lw, kd, bb)


def _rwkv_post_kernel(yf_ref, yb_ref, bv_ref, g_ref, lg_ref, lb_ref, e_ref, et_ref, o_ref):
    y = yf_ref[...] + yb_ref[...]
    mean = _head_sum(y, e_ref, et_ref) * (1.0 / RW_HEAD)
    yc = y - mean
    var = _head_sum(yc * yc, e_ref, et_ref) * (1.0 / RW_HEAD)
    y = yc * lax.rsqrt(var + GN_EPS) * lg_ref[...] + lb_ref[...]
    o_ref[...] = ((y + bv_ref[...].astype(F32)) * g_ref[...].astype(F32)).astype(o_ref.dtype)


def _rwkv_post(yf, yb, bv, g, wts, bm):
    n, w = yf.shape
    row = pl.BlockSpec((bm, w), lambda i: (i, 0))
    vec = pl.BlockSpec((1, w), lambda i: (0, 0))
    e, et = wts['head_e'], wts['head_et']
    return pl.pallas_call(
        _rwkv_post_kernel,
        grid=(n // bm,),
        in_specs=[row, row, row, row, vec, vec,
                  pl.BlockSpec(e.shape, lambda i: (0, 0)), pl.BlockSpec(et.shape, lambda i: (0, 0))],
        out_specs=row,
        out_shape=jax.ShapeDtypeStruct((n, w), BF16),
        compiler_params=_cparams("parallel"),
        name="rwkv_post",
    )(yf, yb, bv, g, wts['lnx_g'], wts['lnx_b'], e, et)


def _merge_kernel(h_ref, s5_ref, rw_ref, wg0_ref, wg1_ref, gb_ref, p0_ref, p1_ref, o_ref):
    h = h_ref[...]
    g0 = jax.nn.sigmoid(_dot(h, wg0_ref[...]) + gb_ref[0:1, :])
    g1 = jax.nn.sigmoid(_dot(h, wg1_ref[...]) + gb_ref[1:2, :])
    merged = g0 * _dot(s5_ref[...], p0_ref[...]) + g1 * _dot(rw_ref[...], p1_ref[...])
    o_ref[...] = merged.astype(o_ref.dtype)


def _merge(h0b, s5_out, rw_out, wts, bm):
    n, d = h0b.shape
    bn = 1024
    nj = d // bn
    return pl.pallas_call(
        _merge_kernel,
        grid=(nj, n // bm),
        in_specs=[pl.BlockSpec((bm, d), lambda j, i: (i, 0)),
                  pl.BlockSpec((bm, S5_WIDTH), lambda j, i: (i, 0)),
                  pl.BlockSpec((bm, RW_WIDTH), lambda j, i: (i, 0)),
                  pl.BlockSpec((d, bn), lambda j, i: (0, j)),
                  pl.BlockSpec((d, bn), lambda j, i: (0, nj + j)),
                  pl.BlockSpec((2, bn), lambda j, i: (0, j)),
                  pl.BlockSpec((S5_WIDTH, bn), lambda j, i: (0, j)),
                  pl.BlockSpec((RW_WIDTH, bn), lambda j, i: (0, j))],
        out_specs=pl.BlockSpec((bm, bn), lambda j, i: (i, j)),
        out_shape=jax.ShapeDtypeStruct((n, d), BF16),
        compiler_params=_cparams("parallel", "parallel"),
        name="merge",
    )(h0b, s5_out, rw_out, wts['w_gate'], wts['w_gate'], wts['gate_b'], wts['proj_s5'], wts['proj_rwkv'])


def _out_kernel(m_ref, h_ref, w_ref, g_ref, b_ref, rh_ref, rl_ref, o_ref, lg_ref):
    bm = m_ref.shape[0]
    halves = [slice(0, bm // 2), slice(bm // 2, bm)]
    xs = [ALPHA * h_ref[rows, :] + _dot(m_ref[rows, :], w_ref[...]) for rows in halves]
    for rows, x in zip(halves, xs):
        h1 = _layernorm(x, g_ref[...], b_ref[...])
        o_ref[rows, :] = h1
        hi, lo = _split(h1)
        lg_ref[rows, :] = _dot(hi, rh_ref[...]) + _dot(lo, rh_ref[...]) + _dot(hi, rl_ref[...])


def _out_proj(merged, h0, wts, bm):
    n, d = h0.shape
    row = pl.BlockSpec((bm, d), lambda i: (i, 0))
    vec = pl.BlockSpec((1, d), lambda i: (0, 0))
    rt = pl.BlockSpec((d, ROUTER_PAD), lambda i: (0, 0))
    return pl.pallas_call(
        _out_kernel,
        grid=(n // bm,),
        in_specs=[row, row, pl.BlockSpec((d, d), lambda i: (0, 0)), vec, vec, rt, rt],
        out_specs=[row, pl.BlockSpec((bm, ROUTER_PAD), lambda i: (i, 0))],
        out_shape=[jax.ShapeDtypeStruct((n, d), F32), jax.ShapeDtypeStruct((n, ROUTER_PAD), F32)],
        compiler_params=_cparams("parallel"),
        name="out_proj",
    )(merged, h0, wts['w_out'], wts['ln1_g'], wts['ln1_b'], wts['router_hi'], wts['router_lo'])


def _route(logits, wts, valid):
    i32 = jnp.int32
    lc = logits[:, :MOE_GROUPS] + wts['router_coarse_b']
    grp = jnp.argmax(lc, axis=-1).astype(i32)
    gate_c = jnp.max(jax.nn.softmax(lc, axis=-1), axis=-1)
    lf = (logits[:, MOE_GROUPS:MOE_GROUPS + N_EXPERTS] + wts['router_fine_b'])
    lf = lf.reshape(-1, MOE_GROUPS, EXPERTS_PER_GROUP)
    sel = grp[:, None, None] == jnp.arange(MOE_GROUPS, dtype=i32)[None, :, None]
    lf = jnp.sum(jnp.where(sel, lf, 0.0), axis=1)
    lane = jnp.arange(EXPERTS_PER_GROUP, dtype=i32)[None, :]
    i1 = jnp.argmax(lf, axis=-1).astype(i32)
    v1 = jnp.max(lf, axis=-1)
    rest = jnp.where(lane == i1[:, None], -jnp.inf, lf)
    i2 = jnp.argmax(rest, axis=-1).astype(i32)
    v2 = jnp.max(rest, axis=-1)
    top_v = jnp.stack([v1, v2], axis=-1)
    top_i = jnp.stack([i1, i2], axis=-1)
    w = gate_c[:, None] * jax.nn.softmax(top_v, axis=-1)
    expert = grp[:, None] * EXPERTS_PER_GROUP + top_i
    expert = jnp.where(valid[:, None], expert, N_EXPERTS)
    w = jnp.where(valid[:, None], w, 0.0)
    n_tok = logits.shape[0]
    n_asg = 2 * n_tok
    e_flat = expert.reshape(-1)
    order = jnp.argsort(e_flat).astype(i32)
    inv = jnp.argsort(order).astype(i32)
    bounds = jnp.sum(e_flat[None, :] < jnp.arange(N_EXPERTS + 1, dtype=i32)[:, None], axis=1, dtype=i32)
    start = bounds[:N_EXPERTS]
    counts = bounds[1:] - start
    padded = (counts + MOE_ROWS - 1) // MOE_ROWS * MOE_ROWS
    ex = jnp.arange(N_EXPERTS, dtype=i32)
    pend = jnp.sum(jnp.where(ex[None, :] <= ex[:, None], padded[None, :], 0), axis=1)
    pstart = pend - padded
    n_blocks = -(-n_asg // MOE_ROWS) + N_EXPERTS
    n_rows = n_blocks * MOE_ROWS
    n_used = pend[-1] // MOE_ROWS
    blk = jnp.minimum(jnp.arange(n_blocks, dtype=i32), n_used - 1)
    blk_exp = jnp.sum(pend[None, :] <= (blk * MOE_ROWS)[:, None], axis=1, dtype=i32)
    blk_exp = jnp.minimum(blk_exp, N_EXPERTS - 1)
    experts = jnp.arange(N_EXPERTS, dtype=i32)

    def lookup(table, idx):
        return jnp.sum(jnp.where(idx[..., None] == experts, table, 0), axis=-1)

    pos = jnp.where(e_flat < N_EXPERTS, lookup(pstart - start, e_flat) + inv, 0)
    d = jnp.arange(n_rows, dtype=i32).reshape(n_blocks, MOE_ROWS)
    k = d - lookup(pstart, blk_exp)[:, None]
    src = jnp.clip(k + lookup(start, blk_exp)[:, None], 0, n_asg - 1)
    row_tok = jnp.where((k < lookup(counts, blk_exp)[:, None]) & (d < pend[-1]), order[src] // 2, 0)
    return row_tok, pos, w, blk_exp, n_used.reshape(1).astype(i32), n_blocks


def _row_gather_start(src_hbm, idx_ref, n_rows, dst, sem, stride=1, offset=0):
    def issue(r, carry):
        row = idx_ref[0, 0, stride * r + offset]
        pltpu.make_async_copy(src_hbm.at[pl.ds(row, 1)], dst.at[pl.ds(r, 1)], sem).start()
        return carry

    lax.fori_loop(0, n_rows, issue, 0, unroll=8)


def _row_gather_wait(src_hbm, n_rows, dst, sem):
    def wait(r, carry):
        pltpu.make_async_copy(src_hbm.at[pl.ds(0, 1)], dst.at[pl.ds(r, 1)], sem).wait()
        return carry

    lax.fori_loop(0, n_rows, wait, 0, unroll=8)


def _expert_kernel(nused_ref, bexp_ref, idx_ref, nidx_ref, x_hbm, wg_ref, wu_ref, wd_ref, o_ref, buf, sem):
    i = pl.program_id(0)
    n_used = nused_ref[0]
    slot = i % 2

    @pl.when((i == 0) & (n_used > 0))
    def _():
        _row_gather_start(x_hbm, idx_ref, MOE_ROWS, buf.at[0], sem.at[0])

    @pl.when(i + 1 < n_used)
    def _():
        _row_gather_start(x_hbm, nidx_ref, MOE_ROWS, buf.at[1 - slot], sem.at[1 - slot])

    @pl.when(i < n_used)
    def _():
        _row_gather_wait(x_hbm, MOE_ROWS, buf.at[slot], sem.at[slot])
        x = buf[slot].astype(BF16)
        hb = jax.nn.silu(_dot(x, wg_ref[0])) * _dot(x, wu_ref[0])
        o_ref[...] = _dot(hb.astype(BF16), wd_ref[0])

    @pl.when(i >= n_used)
    def _():
        o_ref[...] = jnp.zeros_like(o_ref)


def _moe_experts(h1, row_tok, blk_exp, n_used, n_blocks, wts):
    d = h1.shape[1]
    idx = row_tok.reshape(n_blocks, 1, MOE_ROWS)
    last = n_blocks - 1
    smem = pltpu.SMEM
    return pl.pallas_call(
        _expert_kernel,
        grid_spec=pltpu.PrefetchScalarGridSpec(
            num_scalar_prefetch=2,
            grid=(n_blocks,),
            in_specs=[pl.BlockSpec((1, 1, MOE_ROWS), lambda i, nu, be: (i, 0, 0), memory_space=smem),
                      pl.BlockSpec((1, 1, MOE_ROWS), lambda i, nu, be: (jnp.minimum(i + 1, last), 0, 0),
                                   memory_space=smem),
                      pl.BlockSpec(memory_space=pl.ANY),
                      pl.BlockSpec((1, d, D_EXPERT), lambda i, nu, be: (be[i], 0, 0)),
                      pl.BlockSpec((1, d, D_EXPERT), lambda i, nu, be: (be[i], 0, 0)),
                      pl.BlockSpec((1, D_EXPERT, d), lambda i, nu, be: (be[i], 0, 0))],
            out_specs=pl.BlockSpec((MOE_ROWS, d), lambda i, nu, be: (i, 0)),
            scratch_shapes=[pltpu.VMEM((2, MOE_ROWS, d), F32), pltpu.SemaphoreType.DMA((2,))],
        ),
        out_shape=jax.ShapeDtypeStruct((n_blocks * MOE_ROWS, d), F32),
        compiler_params=_cparams("arbitrary"),
        name="moe_experts",
    )(n_used, blk_exp, idx, idx, h1, wts['exp_w_gate'], wts['exp_w_up'], wts['exp_w_down'])


def _combine_kernel(pos_ref, npos_ref, eo_hbm, h_ref, w_ref, g_ref, b_ref, o_ref, buf, sem, *, n_steps, bm):
    i = pl.program_id(0)
    slot = i % 2

    def start(p_ref, s):
        for k in range(2):
            _row_gather_start(eo_hbm, p_ref, bm, buf.at[s, k], sem.at[s], stride=2, offset=k)

    @pl.when(i == 0)
    def _():
        start(pos_ref, 0)

    @pl.when(i + 1 < n_steps)
    def _():
        start(npos_ref, 1 - slot)

    for k in range(2):
        _row_gather_wait(eo_hbm, bm, buf.at[slot, k], sem.at[slot])
    w = w_ref[...]
    moe = w[:, 0:1] * buf[slot, 0] + w[:, 1:2] * buf[slot, 1]
    o_ref[0] = _layernorm(ALPHA * h_ref[...] + moe, g_ref[...], b_ref[...])


def _moe_combine(eo, pos, w, h1, wts, bsz, t, t_pad):
    n, d = h1.shape
    bm = SEQ_FRONT
    per_seq = -(-t // bm)
    n_steps = bsz * per_seq
    blocks_per_seq = t_pad // bm

    def src(i):
        return (i // per_seq) * blocks_per_seq + 1 + i % per_seq

    def nxt(i):
        return src(jnp.minimum(i + 1, n_steps - 1))

    vec = pl.BlockSpec((1, d), lambda i: (0, 0))
    smem = pltpu.SMEM
    pos3 = pos.reshape(n // bm, 1, 2 * bm)
    return pl.pallas_call(
        functools.partial(_combine_kernel, n_steps=n_steps, bm=bm),
        grid=(n_steps,),
        in_specs=[pl.BlockSpec((1, 1, 2 * bm), lambda i: (src(i), 0, 0), memory_space=smem),
                  pl.BlockSpec((1, 1, 2 * bm), lambda i: (nxt(i), 0, 0), memory_space=smem),
                  pl.BlockSpec(memory_space=pl.ANY),
                  pl.BlockSpec((bm, d), lambda i: (src(i), 0)),
                  pl.BlockSpec((bm, 2), lambda i: (src(i), 0)), vec, vec],
        out_specs=pl.BlockSpec((1, bm, d), lambda i: (i // per_seq, i % per_seq, 0)),
        out_shape=jax.ShapeDtypeStruct((bsz, t, d), F32),
        scratch_shapes=[pltpu.VMEM((2, 2, bm) + eo.shape[1:], F32), pltpu.SemaphoreType.DMA((2,))],
        compiler_params=_cparams("arbitrary"),
        name="moe_combine",
    )(pos3, pos3, eo, h1, w, wts['ln2_g'], wts['ln2_b'])


def _prepare_weights(p):
    l = 0
    w_in = p['w_in'][l]
    c0 = S5_WIDTH
    c1 = c0 + 3 * RW_WIDTH + 2 * RW_DECAY_LORA + 2 * RW_ICLR_LORA + RW_GATE_LORA
    gpad = RW_GATE_PAD - RW_GATE_LORA
    wts = {}
    wts['w_u'] = w_in[:, :c0].astype(BF16)
    wts['w_rw'] = jnp.pad(w_in[:, c0:c1], ((0, 0), (0, gpad))).astype(BF16)
    wts['w_gate'] = w_in[:, c1:].astype(BF16)
    wts['mu'] = jnp.pad(p['shift_mu'][l], ((0, 0), (0, gpad)))
    z = jnp.zeros((RW_DECAY_LORA, RW_WIDTH), F32)
    wts['w2'] = jnp.block([[p['rw_w2'][l, 0], z], [z, p['rw_w2'][l, 1]]]).astype(BF16)
    wts['a2'] = jnp.block([[p['rw_a2'][l, 0], z], [z, p['rw_a2'][l, 1]]]).astype(BF16)
    wts['g2'] = jnp.pad(p['rw_g2'][l], ((0, gpad), (0, 0))).astype(BF16)
    wts['w0'] = p['rw_w0'][l]
    wts['a0'] = p['rw_a0'][l]
    wts['k_k'] = p['rw_k_k'][l].reshape(1, -1)
    wts['k_a'] = p['rw_k_a'][l].reshape(1, -1)
    wts['r_k'] = p['rw_r_k'][l].reshape(1, -1)
    wts['lnx_g'] = p['rw_lnx_g'][l].reshape(1, -1)
    wts['lnx_b'] = p['rw_lnx_b'][l].reshape(1, -1)
    head = jnp.arange(RW_WIDTH) // RW_HEAD
    e = (head[:, None] == jnp.arange(RW_HEADS)[None, :]).astype(BF16)
    wts['head_e'] = e
    wts['head_et'] = e.T
    wts['s5'] = _s5_block_operators(*_s5_matrices(
        p['s5_B_re'][l], p['s5_B_im'][l], p['s5_A_re'][l], p['s5_A_im'][l],
        p['s5_log_dt'][l], p['s5_C_re'][l], p['s5_C_im'][l]))
    wts['s5_D'] = p['s5_D'][l]
    wts['glu_w'] = p['s5_glu_w'][l].astype(BF16)
    wts['glu_b'] = p['s5_glu_b'][l]
    wts['proj_s5'] = p['proj_s5'][l].astype(BF16)
    wts['proj_rwkv'] = p['proj_rwkv'][l].astype(BF16)
    wts['gate_b'] = p['gate_b'][l]
    wts['w_out'] = p['w_out'][l].astype(BF16)
    wts['ln1_g'] = p['ln1_g'][l].reshape(1, -1)
    wts['ln1_b'] = p['ln1_b'][l].reshape(1, -1)
    router = jnp.concatenate([p['router_coarse'][l], p['router_fine'][l]], axis=1)
    router = jnp.pad(router, ((0, 0), (0, ROUTER_PAD - router.shape[1])))
    wts['router_hi'], wts['router_lo'] = _split(router)
    wts['router_coarse_b'] = p['router_coarse_b'][l]
    wts['router_fine_b'] = p['router_fine_b'][l]
    wts['exp_w_gate'] = p['exp_w_gate'][l].astype(BF16)
    wts['exp_w_up'] = p['exp_w_up'][l].astype(BF16)
    wts['exp_w_down'] = p['exp_w_down'][l].astype(BF16)
    wts['ln2_g'] = p['ln2_g'][l].reshape(1, -1)
    wts['ln2_b'] = p['ln2_b'][l].reshape(1, -1)
    return wts


def _encode(x, p, wts):
    bsz, t, d = x.shape
    t_lo = SEQ_FRONT - N_META
    t_hi = SEQ_FRONT + t
    t_pad = -(-t_hi // SEQ_ALIGN) * SEQ_ALIGN
    n = bsz * t_pad
    h0, h0b = _ln_in(x, p['meta'], p['ln_in_g'], p['ln_in_b'], t_pad)
    h0 = h0.reshape(n, d)
    h0b = h0b.reshape(n, d)
    bm = _row_block(t_pad, 1024)
    u = _mm(h0b, wts['w_u'], bm, S5_WIDTH, F32, "proj_s5_in")
    rw = _mm(h0b, wts['w_rw'], bm, RW_COLS // 4, F32, "proj_rwkv_in")
    y_ssm = _s5_ssm(u.reshape(bsz, t_pad, S5_WIDTH), wts['s5']).reshape(n, S5_WIDTH)
    s5_out = _s5_post(y_ssm, u, wts['s5_D'], wts['glu_w'], wts['glu_b'], bm)
    r, kk, v, g, bv, lw, kd, bb = _rwkv_prep(rw.reshape(bsz, t_pad, RW_COLS), wts, t_lo, t_hi)
    yf, yb = _rwkv_scan(r, kk, v, lw, kd, bb)
    rw_out = _rwkv_post(yf.reshape(n, -1), yb.reshape(n, -1), bv.reshape(n, -1), g.reshape(n, -1), wts, bm)
    merged = _merge(h0b, s5_out, rw_out, wts, bm)
    h1, logits = _out_proj(merged, h0, wts, _row_block(t_pad, 512))
    seq_pos = jnp.arange(n, dtype=jnp.int32) % t_pad
    valid = (seq_pos >= t_lo) & (seq_pos < t_hi)
    row_tok, pos, w, blk_exp, n_used, n_blocks = _route(logits, wts, valid)
    eo = _moe_experts(h1, row_tok, blk_exp, n_used, n_blocks, wts)
    return _moe_combine(eo, pos, w, h1, wts, bsz, t, t_pad)


def kernel(x_prompt, x_sample, meta, ln_in_g, ln_in_b, w_in, shift_mu, s5_B_re, s5_B_im, s5_A_re, s5_A_im, s5_log_dt, s5_C_re, s5_C_im, s5_D, s5_glu_w, s5_glu_b, rw_w0, rw_w2, rw_a0, rw_a2, rw_g2, rw_k_k, rw_k_a, rw_r_k, rw_lnx_g, rw_lnx_b, proj_s5, proj_rwkv, gate_b, w_out, ln1_g, ln1_b, router_coarse, router_coarse_b, router_fine, router_fine_b, exp_w_gate, exp_w_up, exp_w_down, ln2_g, ln2_b):
    p = {
        'meta': meta, 'ln_in_g': ln_in_g, 'ln_in_b': ln_in_b, 'w_in': w_in, 'shift_mu': shift_mu,
        's5_B_re': s5_B_re, 's5_B_im': s5_B_im, 's5_A_re': s5_A_re, 's5_A_im': s5_A_im,
        's5_log_dt': s5_log_dt, 's5_C_re': s5_C_re, 's5_C_im': s5_C_im, 's5_D': s5_D,
        's5_glu_w': s5_glu_w, 's5_glu_b': s5_glu_b,
        'rw_w0': rw_w0, 'rw_w2': rw_w2, 'rw_a0': rw_a0, 'rw_a2': rw_a2, 'rw_g2': rw_g2,
        'rw_k_k': rw_k_k, 'rw_k_a': rw_k_a, 'rw_r_k': rw_r_k, 'rw_lnx_g': rw_lnx_g, 'rw_lnx_b': rw_lnx_b,
        'proj_s5': proj_s5, 'proj_rwkv': proj_rwkv, 'gate_b': gate_b, 'w_out': w_out,
        'ln1_g': ln1_g, 'ln1_b': ln1_b,
        'router_coarse': router_coarse, 'router_coarse_b': router_coarse_b,
        'router_fine': router_fine, 'router_fine_b': router_fine_b,
        'exp_w_gate': exp_w_gate, 'exp_w_up': exp_w_up, 'exp_w_down': exp_w_down,
        'ln2_g': ln2_g, 'ln2_b': ln2_b,
    }
    wts = _prepare_weights(p)
    return (_encode(x_prompt, p, wts), _encode(x_sample, p, wts))
```

```python
import functools
import math

import jax
import jax.numpy as jnp
from jax import lax
from jax.experimental import pallas as pl
from jax.experimental.pallas import tpu as pltpu

F32 = jnp.float32
BF16 = jnp.bfloat16

D_MODEL = 2048
N_META = 16
S5_WIDTH = 1024
S5_GROUP = 16
S5_GROUPS = 64
S5_STATE = 64
S5_GB = 8
S5_PAIR_MAX_CHUNKS = 160
S5_CHUNK = 16
RW_WIDTH = 1024
RW_HEAD = 64
RW_HEADS = 16
RW_DECAY_LORA = 64
RW_ICLR_LORA = 64
RW_GATE_LORA = 160
RW_GATE_PAD = 256
RW_COLS = 3 * RW_WIDTH + 2 * RW_DECAY_LORA + 2 * RW_ICLR_LORA + RW_GATE_PAD
RW_CHUNK = 64
RW_HEADS_PER_STEP = 2
MOE_GROUPS = 4
EXPERTS_PER_GROUP = 8
N_EXPERTS = 32
D_EXPERT = 512
MOE_ROWS = 256
ROUTER_PAD = 128
DEPTH = 1
ALPHA = (2 * DEPTH) ** 0.25
LN_EPS = 1e-5
GN_EPS = 64e-5
SEQ_TAIL = 64
VMEM_LIMIT = 56 * 1024 * 1024


def _cparams(*sem):
    return pltpu.CompilerParams(dimension_semantics=sem, vmem_limit_bytes=VMEM_LIMIT)


def _row_block(t_pad, cap, mult=8):
    best = mult
    for d in range(mult, cap + 1, mult):
        if t_pad % d == 0:
            best = d
    return best


def _dot(a, b):
    return jnp.dot(a, b, preferred_element_type=F32)


def _dot_nt(a, b):
    return lax.dot_general(a, b, (((1,), (1,)), ((), ())), preferred_element_type=F32)


def _dot_tn(a, b):
    return lax.dot_general(a, b, (((0,), (0,)), ((), ())), preferred_element_type=F32)


def _split(x):
    hi = x.astype(BF16)
    lo = (x - hi.astype(F32)).astype(BF16)
    return hi, lo


def _layernorm(x, g, b):
    mu = jnp.mean(x, axis=-1, keepdims=True)
    xc = x - mu
    var = jnp.mean(xc * xc, axis=-1, keepdims=True)
    return xc * lax.rsqrt(var + LN_EPS) * g + b


def _ln_in_kernel(x_ref, m_ref, g_ref, b_ref, of_ref, ob_ref, *, n_token_blocks):
    j = pl.program_id(1)
    bx, d = x_ref.shape[1], x_ref.shape[2]

    @pl.when(j < n_token_blocks)
    def _():
        y = _layernorm(x_ref[0], g_ref[...], b_ref[...])
        of_ref[0] = y
        ob_ref[0] = y.astype(BF16)

    @pl.when(j == n_token_blocks)
    def _():
        ym = _layernorm(m_ref[...], g_ref[...], b_ref[...])
        y = jnp.concatenate([jnp.zeros((SEQ_TAIL - N_META, d), F32), ym, jnp.zeros((bx - SEQ_TAIL, d), F32)],
                            axis=0)
        of_ref[0] = y
        ob_ref[0] = y.astype(BF16)


def _ln_in(x, meta, g, b):
    bsz, t, d = x.shape
    assert t % SEQ_TAIL == 0
    t_pad = t + SEQ_TAIL
    bx = _row_block(t, 512, SEQ_TAIL)
    nxb = t // bx
    row = pl.BlockSpec((1, bx, d), lambda i, j: (i, j, 0))
    vec = pl.BlockSpec((1, d), lambda i, j: (0, 0))
    return pl.pallas_call(
        functools.partial(_ln_in_kernel, n_token_blocks=nxb),
        grid=(bsz, nxb + 1),
        in_specs=[pl.BlockSpec((1, bx, d), lambda i, j: (i, jnp.minimum(j, nxb - 1), 0)),
                  pl.BlockSpec((N_META, d), lambda i, j: (0, 0)), vec, vec],
        out_specs=[row, row],
        out_shape=[jax.ShapeDtypeStruct((bsz, t_pad, d), F32), jax.ShapeDtypeStruct((bsz, t_pad, d), BF16)],
        compiler_params=_cparams("parallel", "parallel"),
        name="ln_in",
    )(x, meta, g.reshape(1, d), b.reshape(1, d))


def _mm_kernel(x_ref, w_ref, o_ref):
    o_ref[...] = _dot(x_ref[...], w_ref[...]).astype(o_ref.dtype)


def _mm(x, w, bm, bn, out_dtype, name):
    n, k = x.shape
    m = w.shape[1]
    return pl.pallas_call(
        _mm_kernel,
        grid=(m // bn, n // bm),
        in_specs=[pl.BlockSpec((bm, k), lambda j, i: (i, 0)),
                  pl.BlockSpec((k, bn), lambda j, i: (0, j))],
        out_specs=pl.BlockSpec((bm, bn), lambda j, i: (i, j)),
        out_shape=jax.ShapeDtypeStruct((n, m), out_dtype),
        compiler_params=_cparams("parallel", "parallel"),
        name=name,
    )(x, w)


def _s5_matrices(b_re, b_im, a_re, a_im, log_dt, c_re, c_im):
    L = S5_CHUNK
    dt = jnp.exp(log_dt)[..., None]
    mag = jnp.exp(a_re * dt)
    abr = mag * jnp.cos(a_im * dt)
    abi = mag * jnp.sin(a_im * dt)
    den = a_re * a_re + a_im * a_im
    nr = abr - 1.0
    cr = (nr * a_re + abi * a_im) / den
    ci = (abi * a_re - nr * a_im) / den
    bbr = cr[..., None] * b_re - ci[..., None] * b_im
    bbi = cr[..., None] * b_im + ci[..., None] * b_re
    tau = jnp.arange(L + 1, dtype=F32)[:, None, None, None]
    pmag = jnp.exp(tau * a_re * dt)
    pr = pmag * jnp.cos(tau * a_im * dt)
    pi = pmag * jnp.sin(tau * a_im * dt)
    wr = pr[..., None] * bbr - pi[..., None] * bbi
    wi = pr[..., None] * bbi + pi[..., None] * bbr
    kern = (jnp.einsum('zgop,tzgpi->tzgoi', c_re, wr)
            - jnp.einsum('zgop,tzgpi->tzgoi', c_im, wi))
    s = jnp.arange(L)[:, None]
    t = jnp.arange(L)[None, :]
    lag = t - s
    kf = jnp.where((lag >= 0)[..., None, None, None], kern[jnp.clip(lag, 0, L), 0], 0.0)
    kb = jnp.where((lag <= 0)[..., None, None, None], kern[jnp.clip(-lag, 0, L), 1], 0.0)
    toep = (kf + kb).transpose(2, 0, 4, 1, 3).reshape(S5_GROUPS, L * S5_GROUP, L * S5_GROUP)
    wf_r, wf_i = wr[::-1][1:, 0], wi[::-1][1:, 0]
    wb_r, wb_i = wr[:L, 1], wi[:L, 1]
    bmat = jnp.concatenate([wf_r, wb_r, wf_i, wb_i], axis=2)
    bmat = bmat.transpose(1, 0, 3, 2).reshape(S5_GROUPS, L * S5_GROUP, 4 * S5_STATE)
    pf_r, pf_i = pr[1:, 0], pi[1:, 0]
    pb_r, pb_i = pr[::-1][:L, 1], pi[::-1][:L, 1]
    c0r, c0i, c1r, c1i = c_re[0], c_im[0], c_re[1], c_im[1]

    def cpow(cre, cim, p_r, p_i):
        re = cre[None] * p_r[:, :, None, :] - cim[None] * p_i[:, :, None, :]
        im = cre[None] * p_i[:, :, None, :] + cim[None] * p_r[:, :, None, :]
        return re, -im

    f_re, f_im = cpow(c0r, c0i, pf_r, pf_i)
    g_re, g_im = cpow(c1r, c1i, pb_r, pb_i)
    cmat = jnp.concatenate([f_re, g_re, f_im, g_im], axis=3)
    cmat = cmat.transpose(1, 3, 0, 2).reshape(S5_GROUPS, 4 * S5_STATE, L * S5_GROUP)
    lam_re = jnp.concatenate([pr[L, 0], pr[L, 1]], axis=-1)[:, None, :]
    lam_im = jnp.concatenate([pi[L, 0], pi[L, 1]], axis=-1)[:, None, :]
    return bmat, toep, cmat, lam_re, lam_im


def _s5_expand_kernel(src_ref, e_ref, o_ref, *, pieces):
    for j in range(src_ref.shape[0]):
        x = _dot(src_ref[j].astype(BF16), e_ref[j]).astype(BF16)
        for src, rows, dst, step in pieces:
            o_ref[0, dst + j * step:dst + j * step + rows, :] = x[src:src + rows, :]


def _s5_expand(per_group, expand, pieces):
    groups, rows, cols = per_group.shape
    gb, _, wide = expand.shape
    return pl.pallas_call(
        functools.partial(_s5_expand_kernel, pieces=pieces),
        grid=(groups // gb,),
        in_specs=[pl.BlockSpec((gb, rows, cols), lambda g: (g, 0, 0)),
                  pl.BlockSpec((gb, cols, wide), lambda g: (0, 0, 0))],
        out_specs=pl.BlockSpec((1, gb * rows, wide), lambda g: (g, 0, 0)),
        out_shape=jax.ShapeDtypeStruct((groups // gb, gb * rows, wide), BF16),
        compiler_params=_cparams("parallel"),
        name="s5_expand",
    )(per_group, expand)


def _s5_block_operators(bmat, toep, cmat, lam_re, lam_im):
    nb, gb, L, c, p = S5_GROUPS // S5_GB, S5_GB, S5_CHUNK, S5_GROUP, S5_STATE
    quarters = (0, 2, 1, 3)
    wide = L * gb * c
    r = jnp.arange(L * c)[None, :, None]
    col = jnp.arange(wide)[None, None, :]
    j = jnp.arange(gb)[:, None, None]
    e_tok = ((r // c == col // (gb * c)) & ((col // c) % gb == j) & (r % c == col % c)).astype(BF16)
    k_of_col = jnp.array(quarters)[col // (gb * p)]
    e_state = ((r // p == k_of_col) & ((col // p) % gb == j) & (r % p == col % p)).astype(BF16)
    tok_rows = tuple((s * c, c, s * gb * c, c) for s in range(L))
    state_rows = tuple((quarters[k] * p, p, k * gb * p, p) for k in range(4))
    wb = _s5_expand(bmat, e_state, tok_rows)
    wt = _s5_expand(toep, e_tok, tok_rows)
    wc = _s5_expand(cmat, e_tok, state_rows)
    ar_f = lam_re[:, 0, :p].reshape(nb, gb * p)
    ar_b = lam_re[:, 0, p:].reshape(nb, gb * p)
    ai_f = lam_im[:, 0, :p].reshape(nb, gb * p)
    ai_b = lam_im[:, 0, p:].reshape(nb, gb * p)
    a1 = jnp.stack([ar_f, ar_f, ar_b, ar_b] * 2, axis=1)
    a2 = jnp.stack([-ai_f, ai_f, -ai_b, ai_b] * 2, axis=1)
    return wb, wt, wc, a1, a2


def _s5_kernel(u_ref, wb_ref, wt_ref, wc_ref, a1_ref, a2_ref, y_ref, s_scr, xf_scr, xb_scr, *, n_chunks):
    L = S5_CHUNK
    C = n_chunks
    nseq = u_ref.shape[0]
    cp = -(-C // 8) * 8
    sw = S5_GB * S5_STATE
    tail = SEQ_TAIL // L

    def rows_of(parts):
        if cp > C:
            pad = jnp.zeros((cp - C, parts[0].shape[1]), F32)
            parts = [x for part in parts for x in (part, pad)]
        return jnp.concatenate(parts, axis=0).astype(BF16)

    u8 = rows_of([jnp.concatenate([u_ref[q, pl.ds(s, C, stride=L), :] for s in range(L)], axis=1)
                  for q in range(nseq)])
    s_all = _dot(u8, wb_ref[0])
    for q in range(nseq):
        for k in range(4):
            s_scr[:, 4 * q + k, :] = s_all[q * cp:q * cp + C, k * sw:(k + 1) * sw]
    if nseq == 1:
        s_scr[:, 4:8, :] = jnp.zeros((C, 4, sw), F32)
    a1 = a1_ref[0]
    a2 = a2_ref[0]
    row = lax.broadcasted_iota(jnp.int32, (8, sw), 0)
    even = (row % 2) == 0
    is_fwd = (row % 4) < 2

    def step(c, x):
        cf = (c + C - tail) % C
        cb = (2 * C - 1 - c - tail) % C
        xf_scr[cf] = x
        xb_scr[cb] = x
        s = jnp.where(is_fwd, s_scr[cf], s_scr[cb])
        swapped = jnp.where(even, pltpu.roll(x, 7, 0), pltpu.roll(x, 1, 0))
        return a1 * x + a2 * swapped + s

    lax.fori_loop(0, C, step, jnp.zeros((8, sw), F32))
    x_in = rows_of([jnp.concatenate([xf_scr[:, 4 * q, :], xf_scr[:, 4 * q + 1, :],
                                     xb_scr[:, 4 * q + 2, :], xb_scr[:, 4 * q + 3, :]], axis=1)
                    for q in range(nseq)])
    y8 = _dot(u8, wt_ref[0]) + _dot(x_in, wc_ref[0])
    for q in range(nseq):
        for t in range(L):
            y_ref[q, pl.ds(t, C, stride=L), :] = y8[q * cp:q * cp + C, t * 128:(t + 1) * 128]


def _s5_ssm(u3, ops):
    wb, wt, wc, a1, a2 = ops
    bsz, t_pad, width = u3.shape
    n_chunks = t_pad // S5_CHUNK
    nb = S5_GROUPS // S5_GB
    lanes = S5_GB * S5_GROUP
    sw = S5_GB * S5_STATE
    nseq = 2 if (bsz % 2 == 0 and n_chunks <= S5_PAIR_MAX_CHUNKS) else 1
    blk = pl.BlockSpec((nseq, t_pad, lanes), lambda g, b: (b, 0, g))
    mat = pl.BlockSpec((1,) + wb.shape[1:], lambda g, b: (g, 0, 0), pipeline_mode=pl.Buffered(1))
    vec = pl.BlockSpec((1, 8, sw), lambda g, b: (g, 0, 0))
    return pl.pallas_call(
        functools.partial(_s5_kernel, n_chunks=n_chunks),
        grid=(nb, bsz // nseq),
        in_specs=[blk, mat, mat, mat, vec, vec],
        out_specs=blk,
        out_shape=jax.ShapeDtypeStruct(u3.shape, F32),
        scratch_shapes=[pltpu.VMEM((n_chunks, 8, sw), F32)] * 3,
        compiler_params=_cparams("arbitrary", "arbitrary"),
        name="s5_ssm",
    )(u3, wb, wt, wc, a1, a2)


def _s5_post_kernel(y_ref, u_ref, d_ref, w_ref, b_ref, o_ref):
    y = y_ref[...] + u_ref[...] * d_ref[...]
    act = y * (0.5 * (1.0 + jnp.tanh(math.sqrt(2.0 / math.pi) * (y + 0.044715 * (y * y * y)))))
    z = _dot(act.astype(BF16), w_ref[...]) + b_ref[...]
    o_ref[...] = (act * jax.nn.sigmoid(z)).astype(o_ref.dtype)


def _s5_post(y, u, d_skip, glu_w, glu_b, bm):
    n = y.shape[0]
    row = pl.BlockSpec((bm, S5_WIDTH), lambda i: (i, 0))
    vec = pl.BlockSpec((1, S5_WIDTH), lambda i: (0, 0))
    return pl.pallas_call(
        _s5_post_kernel,
        grid=(n // bm,),
        in_specs=[row, row, vec, pl.BlockSpec((S5_WIDTH, S5_WIDTH), lambda i: (0, 0)), vec],
        out_specs=row,
        out_shape=jax.ShapeDtypeStruct((n, S5_WIDTH), BF16),
        compiler_params=_cparams("parallel"),
        name="s5_post",
    )(y, u, d_skip.reshape(1, -1), glu_w, glu_b.reshape(1, -1))


def _head_sum(x, e_ref, et_ref):
    hi, lo = _split(x)
    s = _dot(hi, e_ref[...]) + _dot(lo, e_ref[...])
    shi, slo = _split(s)
    return _dot(shi, et_ref[...]) + _dot(slo, et_ref[...])


def _rwkv_prep_kernel(cur_ref, prev_ref, next_ref, mu_ref, w2_ref, a2_ref, g2_ref, w0_ref, a0_ref,
                      kk_ref, ka_ref, rk_ref, e_ref, et_ref,
                      r_o, kk_o, v_o, g_o, bv_o, lw_o, kd_o, bb_o, *, t, t_pad, bm):
    j = pl.program_id(1)
    p = cur_ref[0]
    row = lax.broadcasted_iota(jnp.int32, (bm, 1), 0)
    prev_row = prev_ref[0, 7:8, :]
    next_row = next_ref[0, 0:1, :]
    prev = jnp.where(row == 0, prev_row, pltpu.roll(p, 1, 0))
    nxt = jnp.where(row == bm - 1, next_row, pltpu.roll(p, bm - 1, 0))
    xs = p + mu_ref[0:1, :] * (prev - p) + mu_ref[1:2, :] * (nxt - p)
    w = RW_WIDTH
    r = xs[:, 0:w]
    k = xs[:, w:2 * w]
    v = xs[:, 2 * w:3 * w]
    lw = xs[:, 3 * w:3 * w + 128]
    la = xs[:, 3 * w + 128:3 * w + 256]
    lg = xs[:, 3 * w + 256:]
    w_log = _dot(jnp.tanh(lw).astype(BF16), w2_ref[...])
    a_lin = _dot(la.astype(BF16), a2_ref[...])
    g = _dot(jax.nn.sigmoid(lg).astype(BF16), g2_ref[...])
    kk = k * kk_ref[...]
    n2 = _head_sum(kk * kk, e_ref, et_ref)
    kk = kk / jnp.maximum(jnp.sqrt(n2), 1e-12)
    pos = j * bm + row
    valid = (pos < t) | (pos >= t_pad - N_META)
    v = jnp.where(valid, v, 0.0)
    kd_sum = jnp.zeros_like(k)
    for z in range(2):
        wl = w_log[:, z * w:(z + 1) * w] + w0_ref[z:z + 1, :]
        lw_o[z, 0] = -math.exp(-0.5) * jax.nn.sigmoid(wl)
        a = jax.nn.sigmoid(a_lin[:, z * w:(z + 1) * w] + a0_ref[z:z + 1, :])
        kd = k * (1.0 + (a - 1.0) * ka_ref[...])
        kd_o[z, 0] = kd.astype(BF16)
        bb_o[z, 0] = (kk * a).astype(BF16)
        kd_sum = kd_sum + kd
    bonus = _head_sum(r * kd_sum * rk_ref[...], e_ref, et_ref)
    r_o[0] = r.astype(BF16)
    kk_o[0] = kk.astype(BF16)
    v_o[0] = v.astype(BF16)
    g_o[0] = g.astype(BF16)
    bv_o[0] = (bonus * v).astype(BF16)


def _rwkv_prep(rw3, wts, t):
    bsz, t_pad, _ = rw3.shape
    bm = _row_block(t_pad, 320, 64)
    nb8 = bm // 8
    n8 = t_pad // 8
    w = RW_WIDTH
    cur = pl.BlockSpec((1, bm, RW_COLS), lambda b, j: (b, j, 0))
    prev = pl.BlockSpec((1, 8, RW_COLS), lambda b, j: (b, (j * nb8 + n8 - 1) % n8, 0))
    nxt = pl.BlockSpec((1, 8, RW_COLS), lambda b, j: (b, ((j + 1) * nb8) % n8, 0))

    def full(a):
        return pl.BlockSpec(a.shape, lambda b, j: (0,) * a.ndim)

    shared = pl.BlockSpec((1, bm, w), lambda b, j: (b, j, 0))
    per_dir = pl.BlockSpec((2, 1, bm, w), lambda b, j: (0, b, j, 0))
    consts = [wts['mu'], wts['w2'], wts['a2'], wts['g2'], wts['w0'], wts['a0'],
              wts['k_k'], wts['k_a'], wts['r_k'], wts['head_e'], wts['head_et']]
    sds = jax.ShapeDtypeStruct
    return pl.pallas_call(
        functools.partial(_rwkv_prep_kernel, t=t, t_pad=t_pad, bm=bm),
        grid=(bsz, t_pad // bm),
        in_specs=[cur, prev, nxt] + [full(a) for a in consts],
        out_specs=[shared] * 5 + [per_dir] * 3,
        out_shape=[sds((bsz, t_pad, w), BF16)] * 5
        + [sds((2, bsz, t_pad, w), F32), sds((2, bsz, t_pad, w), BF16), sds((2, bsz, t_pad, w), BF16)],
        compiler_params=_cparams("parallel", "parallel"),
        name="rwkv_prep",
    )(rw3, rw3, rw3, *consts)


def _rwkv_chunk(fwd, r_ref, kk_ref, v_ref, lw_ref, kd_ref, bb_ref, y_ref, st_ref, z):
    L = RW_CHUNK
    hd = RW_HEAD
    gw = RW_HEADS_PER_STEP * hd
    n_groups = RW_WIDTH // gw
    row = lax.broadcasted_iota(jnp.int32, (L, L), 0)
    col = lax.broadcasted_iota(jnp.int32, (L, L), 1)
    tri = jnp.where((col <= row) if fwd else (col >= row), 1.0, 0.0).astype(BF16)
    grow = lax.broadcasted_iota(jnp.int32, (L, gw), 0)
    gcol = lax.broadcasted_iota(jnp.int32, (L, gw), 1) % L
    incl = (gcol <= grow) if fwd else (gcol >= grow)
    strict = (gcol < grow) if fwd else (gcol > grow)
    bd_mask = jnp.where(lax.broadcasted_iota(jnp.int32, (gw, gw), 0) // hd
                        == lax.broadcasted_iota(jnp.int32, (gw, gw), 1) // hd, 1.0, 0.0).astype(BF16)

    def bd(x):
        return jnp.concatenate([x] * RW_HEADS_PER_STEP, axis=0) * bd_mask

    def stack(x):
        return jnp.concatenate([x[:, h * hd:(h + 1) * hd] for h in range(RW_HEADS_PER_STEP)], axis=0)

    lw = lw_ref[0, 0]
    lw_hi, lw_lo = _split(lw)
    c = _dot(tri, lw_hi) + _dot(tri, lw_lo)
    e = c - lw
    c_tot = c[L - 1:L, :] if fwd else c[0:1, :]
    r = r_ref[0].astype(F32)
    kk = kk_ref[0].astype(F32)
    kd = kd_ref[0, 0].astype(F32)
    bb = bb_ref[0, 0].astype(F32)
    v = v_ref[0]
    q1 = (kk * jnp.exp(e)).astype(BF16)
    q2 = (r * jnp.exp(c)).astype(BF16)
    inv = jnp.exp(-c)
    k1 = (kd * inv).astype(BF16)
    k2 = (bb * inv).astype(BF16)
    rest = jnp.exp(c_tot - c)
    k1p = (kd * rest).astype(BF16)
    k2p = (bb * rest).astype(BF16)
    dec_tot = jnp.exp(c_tot)
    def group_chain(g):
        sl = slice(g * gw, (g + 1) * gw)
        s0 = st_ref[z, g]
        s0_hi, s0_lo = _split(s0)
        lhs = jnp.concatenate([q1[:, sl], q2[:, sl]], axis=0)
        rhs = jnp.concatenate([bd(k1[:, sl]), bd(k2[:, sl]), bd(s0_hi), bd(s0_lo)], axis=0)
        m1 = _dot_nt(lhs, rhs)
        yield
        a_kd = jnp.where(strict, m1[:L, 0:gw], 0.0)
        a_b = jnp.where(strict, m1[:L, gw:2 * gw], 0.0)
        q1s = m1[:L, 2 * gw:3 * gw] + m1[:L, 3 * gw:]
        b_kd = jnp.where(incl, m1[L:, 0:gw], 0.0)
        b_b = jnp.where(incl, m1[L:, gw:2 * gw], 0.0)
        q2s = m1[L:, 2 * gw:3 * gw] + m1[L:, 3 * gw:]
        vg = v[:, sl]
        v_bd = bd(vg)
        x = q1s + _dot(a_kd.astype(BF16), v_bd)
        yield
        m = -a_b
        levels = L.bit_length() - 1
        for lvl in range(levels):
            mb = m.astype(BF16)
            if lvl < levels - 1:
                rr = _dot(mb, jnp.concatenate([bd(x.astype(BF16)), bd(mb)], axis=1))
                x = x + rr[:, :gw]
                m = rr[:, gw:]
            else:
                x = x + _dot(mb, bd(x.astype(BF16)))
            yield
        ub = x.astype(BF16)
        y = q2s + _dot(jnp.concatenate([b_kd, -b_b], axis=1).astype(BF16),
                       jnp.concatenate([v_bd, bd(ub)], axis=0))
        y_ref[0, :, sl] = y
        yield
        st_ref[z, g] = s0 * dec_tot[:, sl] + _dot_tn(
            jnp.concatenate([stack(vg), stack(ub)], axis=0),
            jnp.concatenate([bd(k1p[:, sl]), -bd(k2p[:, sl])], axis=0))

    return [group_chain(g) for g in range(n_groups)]


def _rwkv_scan_kernel(rf, kkf, vf, lwf, kdf, bbf, rb, kkb, vb, lwb, kdb, bbb, yf_ref, yb_ref, st_ref):
    @pl.when(pl.program_id(1) == 0)
    def _():
        st_ref[...] = jnp.zeros_like(st_ref)

    chains = (_rwkv_chunk(True, rf, kkf, vf, lwf, kdf, bbf, yf_ref, st_ref, 0)
              + _rwkv_chunk(False, rb, kkb, vb, lwb, kdb, bbb, yb_ref, st_ref, 1))
    while chains:
        alive = []
        for chain in chains:
            try:
                next(chain)
                alive.append(chain)
            except StopIteration:
                pass
        chains = alive


def _rwkv_scan(r, kk, v, lw, kd, bb):
    bsz, t_pad, w = r.shape
    L = RW_CHUNK
    nc = t_pad // L
    gw = RW_HEADS_PER_STEP * RW_HEAD
    grid = (bsz, nc)

    def block(fwd, j):
        logical = j if fwd else nc - 1 - j
        return (logical + nc - 1) % nc

    def shared(fwd):
        return pl.BlockSpec((1, L, w), lambda b, j: (b, block(fwd, j), 0))

    def per_dir(fwd):
        return pl.BlockSpec((1, 1, L, w), lambda b, j: (0 if fwd else 1, b, block(fwd, j), 0))

    in_specs = []
    for fwd in (True, False):
        in_specs += [shared(fwd), shared(fwd), shared(fwd), per_dir(fwd), per_dir(fwd), per_dir(fwd)]
    return pl.pallas_call(
        _rwkv_scan_kernel,
        grid=grid,
        in_specs=in_specs,
        out_specs=[shared(True), shared(False)],
        out_shape=[jax.ShapeDtypeStruct((bsz, t_pad, w), F32)] * 2,
        scratch_shapes=[pltpu.VMEM((2, w // gw, RW_HEAD, gw), F32)],
        compiler_params=_cparams("parallel", "arbitrary"),
        name="rwkv_scan",
    )(r, kk, v, lw, kd, bb, r, kk, v, lw, kd, bb)


def _rwkv_post_kernel(yf_ref, yb_ref, bv_ref, g_ref, lg_ref, lb_ref, e_ref, et_ref, o_ref):
    y = yf_ref[...] + yb_ref[...]
    mean = _head_sum(y, e_ref, et_ref) * (1.0 / RW_HEAD)
    yc = y - mean
    var = _head_sum(yc * yc, e_ref, et_ref) * (1.0 / RW_HEAD)
    y = yc * lax.rsqrt(var + GN_EPS) * lg_ref[...] + lb_ref[...]
    o_ref[...] = ((y + bv_ref[...].astype(F32)) * g_ref[...].astype(F32)).astype(o_ref.dtype)


def _rwkv_post(yf, yb, bv, g, wts, bm):
    n, w = yf.shape
    row = pl.BlockSpec((bm, w), lambda i: (i, 0))
    vec = pl.BlockSpec((1, w), lambda i: (0, 0))
    e, et = wts['head_e'], wts['head_et']
    return pl.pallas_call(
        _rwkv_post_kernel,
        grid=(n // bm,),
        in_specs=[row, row, row, row, vec, vec,
                  pl.BlockSpec(e.shape, lambda i: (0, 0)), pl.BlockSpec(et.shape, lambda i: (0, 0))],
        out_specs=row,
        out_shape=jax.ShapeDtypeStruct((n, w), BF16),
        compiler_params=_cparams("parallel"),
        name="rwkv_post",
    )(yf, yb, bv, g, wts['lnx_g'], wts['lnx_b'], e, et)


def _merge_kernel(h_ref, s5_ref, rw_ref, wg0_ref, wg1_ref, gb_ref, p0_ref, p1_ref, o_ref):
    h = h_ref[...]
    g0 = jax.nn.sigmoid(_dot(h, wg0_ref[...]) + gb_ref[0:1, :])
    g1 = jax.nn.sigmoid(_dot(h, wg1_ref[...]) + gb_ref[1:2, :])
    merged = g0 * _dot(s5_ref[...], p0_ref[...]) + g1 * _dot(rw_ref[...], p1_ref[...])
    o_ref[...] = merged.astype(o_ref.dtype)


def _merge(h0b, s5_out, rw_out, wts, bm):
    n, d = h0b.shape
    bn = 1024
    nj = d // bn
    return pl.pallas_call(
        _merge_kernel,
        grid=(nj, n // bm),
        in_specs=[pl.BlockSpec((bm, d), lambda j, i: (i, 0)),
                  pl.BlockSpec((bm, S5_WIDTH), lambda j, i: (i, 0)),
                  pl.BlockSpec((bm, RW_WIDTH), lambda j, i: (i, 0)),
                  pl.BlockSpec((d, bn), lambda j, i: (0, j)),
                  pl.BlockSpec((d, bn), lambda j, i: (0, nj + j)),
                  pl.BlockSpec((2, bn), lambda j, i: (0, j)),
                  pl.BlockSpec((S5_WIDTH, bn), lambda j, i: (0, j)),
                  pl.BlockSpec((RW_WIDTH, bn), lambda j, i: (0, j))],
        out_specs=pl.BlockSpec((bm, bn), lambda j, i: (i, j)),
        out_shape=jax.ShapeDtypeStruct((n, d), BF16),
        compiler_params=_cparams("parallel", "parallel"),
        name="merge",
    )(h0b, s5_out, rw_out, wts['w_gate'], wts['w_gate'], wts['gate_b'], wts['proj_s5'], wts['proj_rwkv'])


def _out_kernel(m_ref, h_ref, w_ref, g_ref, b_ref, rh_ref, rl_ref, o_ref, lg_ref):
    bm = m_ref.shape[0]
    halves = [slice(0, bm // 2), slice(bm // 2, bm)]
    xs = [ALPHA * h_ref[rows, :] + _dot(m_ref[rows, :], w_ref[...]) for rows in halves]
    for rows, x in zip(halves, xs):
        h1 = _layernorm(x, g_ref[...], b_ref[...])
        o_ref[rows, :] = h1
        hi, lo = _split(h1)
        lg_ref[rows, :] = _dot(hi, rh_ref[...]) + _dot(lo, rh_ref[...]) + _dot(hi, rl_ref[...])


def _out_proj(merged, h0, wts, bm):
    n, d = h0.shape
    row = pl.BlockSpec((bm, d), lambda i: (i, 0))
    vec = pl.BlockSpec((1, d), lambda i: (0, 0))
    rt = pl.BlockSpec((d, ROUTER_PAD), lambda i: (0, 0))
    return pl.pallas_call(
        _out_kernel,
        grid=(n // bm,),
        in_specs=[row, row, pl.BlockSpec((d, d), lambda i: (0, 0)), vec, vec, rt, rt],
        out_specs=[row, pl.BlockSpec((bm, ROUTER_PAD), lambda i: (i, 0))],
        out_shape=[jax.ShapeDtypeStruct((n, d), F32), jax.ShapeDtypeStruct((n, ROUTER_PAD), F32)],
        compiler_params=_cparams("parallel"),
        name="out_proj",
    )(merged, h0, wts['w_out'], wts['ln1_g'], wts['ln1_b'], wts['router_hi'], wts['router_lo'])


def _route(logits, wts, valid):
    i32 = jnp.int32
    lc = logits[:, :MOE_GROUPS] + wts['router_coarse_b']
    grp = jnp.argmax(lc, axis=-1).astype(i32)
    gate_c = jnp.max(jax.nn.softmax(lc, axis=-1), axis=-1)
    lf = (logits[:, MOE_GROUPS:MOE_GROUPS + N_EXPERTS] + wts['router_fine_b'])
    lf = lf.reshape(-1, MOE_GROUPS, EXPERTS_PER_GROUP)
    sel = grp[:, None, None] == jnp.arange(MOE_GROUPS, dtype=i32)[None, :, None]
    lf = jnp.sum(jnp.where(sel, lf, 0.0), axis=1)
    lane = jnp.arange(EXPERTS_PER_GROUP, dtype=i32)[None, :]
    i1 = jnp.argmax(lf, axis=-1).astype(i32)
    v1 = jnp.max(lf, axis=-1)
    rest = jnp.where(lane == i1[:, None], -jnp.inf, lf)
    i2 = jnp.argmax(rest, axis=-1).astype(i32)
    v2 = jnp.max(rest, axis=-1)
    top_v = jnp.stack([v1, v2], axis=-1)
    top_i = jnp.stack([i1, i2], axis=-1)
    w = gate_c[:, None] * jax.nn.softmax(top_v, axis=-1)
    expert = grp[:, None] * EXPERTS_PER_GROUP + top_i
    expert = jnp.where(valid[:, None], expert, N_EXPERTS)
    w = jnp.where(valid[:, None], w, 0.0)
    n_tok = logits.shape[0]
    n_asg = 2 * n_tok
    e_flat = expert.reshape(-1)
    order = jnp.argsort(e_flat).astype(i32)
    inv = jnp.argsort(order).astype(i32)
    bounds = jnp.sum(e_flat[None, :] < jnp.arange(N_EXPERTS + 1, dtype=i32)[:, None], axis=1, dtype=i32)
    start = bounds[:N_EXPERTS]
    counts = bounds[1:] - start
    padded = (counts + MOE_ROWS - 1) // MOE_ROWS * MOE_ROWS
    ex = jnp.arange(N_EXPERTS, dtype=i32)
    pend = jnp.sum(jnp.where(ex[None, :] <= ex[:, None], padded[None, :], 0), axis=1)
    pstart = pend - padded
    n_blocks = -(-n_asg // MOE_ROWS) + N_EXPERTS
    n_rows = n_blocks * MOE_ROWS
    n_used = pend[-1] // MOE_ROWS
    blk = jnp.minimum(jnp.arange(n_blocks, dtype=i32), n_used - 1)
    blk_exp = jnp.sum(pend[None, :] <= (blk * MOE_ROWS)[:, None], axis=1, dtype=i32)
    blk_exp = jnp.minimum(blk_exp, N_EXPERTS - 1)
    experts = jnp.arange(N_EXPERTS, dtype=i32)

    def lookup(table, idx):
        return jnp.sum(jnp.where(idx[..., None] == experts, table, 0), axis=-1)

    pos = jnp.where(e_flat < N_EXPERTS, lookup(pstart - start, e_flat) + inv, 0)
    d = jnp.arange(n_rows, dtype=i32).reshape(n_blocks, MOE_ROWS)
    k = d - lookup(pstart, blk_exp)[:, None]
    src = jnp.clip(k + lookup(start, blk_exp)[:, None], 0, n_asg - 1)
    row_tok = jnp.where((k < lookup(counts, blk_exp)[:, None]) & (d < pend[-1]), order[src] // 2, 0)
    return row_tok, pos, w, blk_exp, n_used.reshape(1).astype(i32), n_blocks


def _row_gather_start(src_hbm, idx_ref, n_rows, dst, sem, stride=1, offset=0):
    def issue(i, carry):
        for queue in range(2):
            r = 2 * i + queue
            row = idx_ref[0, 0, stride * r + offset]
            pltpu.make_async_copy(src_hbm.at[pl.ds(row, 1)], dst.at[pl.ds(r, 1)], sem).start(priority=queue)
        return carry

    lax.fori_loop(0, n_rows // 2, issue, 0, unroll=4)


def _row_gather_wait(src_hbm, n_rows, dst, sem):
    def wait(r, carry):
        pltpu.make_async_copy(src_hbm.at[pl.ds(0, 1)], dst.at[pl.ds(r, 1)], sem).wait()
        return carry

    lax.fori_loop(0, n_rows, wait, 0, unroll=8)


def _expert_kernel(nused_ref, bexp_ref, idx_ref, nidx_ref, x_hbm, wg_ref, wu_ref, wd_ref, o_ref, buf, sem):
    i = pl.program_id(0)
    n_used = nused_ref[0]
    slot = i % 2

    @pl.when((i == 0) & (n_used > 0))
    def _():
        _row_gather_start(x_hbm, idx_ref, MOE_ROWS, buf.at[0], sem.at[0])

    @pl.when(i + 1 < n_used)
    def _():
        _row_gather_start(x_hbm, nidx_ref, MOE_ROWS, buf.at[1 - slot], sem.at[1 - slot])

    @pl.when(i < n_used)
    def _():
        _row_gather_wait(x_hbm, MOE_ROWS, buf.at[slot], sem.at[slot])
        x = buf[slot].astype(BF16)
        hb = jax.nn.silu(_dot(x, wg_ref[0])) * _dot(x, wu_ref[0])
        o_ref[...] = _dot(hb.astype(BF16), wd_ref[0])

    @pl.when(i >= n_used)
    def _():
        o_ref[...] = jnp.zeros_like(o_ref)


def _moe_experts(h1, row_tok, blk_exp, n_used, n_blocks, wts):
    d = h1.shape[1]
    idx = row_tok.reshape(n_blocks, 1, MOE_ROWS)
    last = n_blocks - 1
    smem = pltpu.SMEM
    return pl.pallas_call(
        _expert_kernel,
        grid_spec=pltpu.PrefetchScalarGridSpec(
            num_scalar_prefetch=2,
            grid=(n_blocks,),
            in_specs=[pl.BlockSpec((1, 1, MOE_ROWS), lambda i, nu, be: (i, 0, 0), memory_space=smem),
                      pl.BlockSpec((1, 1, MOE_ROWS), lambda i, nu, be: (jnp.minimum(i + 1, last), 0, 0),
                                   memory_space=smem),
                      pl.BlockSpec(memory_space=pl.ANY),
                      pl.BlockSpec((1, d, D_EXPERT), lambda i, nu, be: (be[i], 0, 0)),
                      pl.BlockSpec((1, d, D_EXPERT), lambda i, nu, be: (be[i], 0, 0)),
                      pl.BlockSpec((1, D_EXPERT, d), lambda i, nu, be: (be[i], 0, 0))],
            out_specs=pl.BlockSpec((MOE_ROWS, d), lambda i, nu, be: (i, 0)),
            scratch_shapes=[pltpu.VMEM((2, MOE_ROWS, d), F32), pltpu.SemaphoreType.DMA((2,))],
        ),
        out_shape=jax.ShapeDtypeStruct((n_blocks * MOE_ROWS, d), F32),
        compiler_params=_cparams("arbitrary"),
        name="moe_experts",
    )(n_used, blk_exp, idx, idx, h1, wts['exp_w_gate'], wts['exp_w_up'], wts['exp_w_down'])


def _combine_kernel(pos_ref, npos_ref, eo_hbm, h_ref, w_ref, g_ref, b_ref, o_ref, buf, sem, *, n_steps, bm):
    i = pl.program_id(0)
    slot = i % 2

    def start(p_ref, s):
        for k in range(2):
            _row_gather_start(eo_hbm, p_ref, bm, buf.at[s, k], sem.at[s], stride=2, offset=k)

    @pl.when(i == 0)
    def _():
        start(pos_ref, 0)

    @pl.when(i + 1 < n_steps)
    def _():
        start(npos_ref, 1 - slot)

    for k in range(2):
        _row_gather_wait(eo_hbm, bm, buf.at[slot, k], sem.at[slot])
    w = w_ref[0]
    moe = w[:, 0:1] * buf[slot, 0] + w[:, 1:2] * buf[slot, 1]
    o_ref[0] = _layernorm(ALPHA * h_ref[0] + moe, g_ref[...], b_ref[...])


def _moe_combine(eo, pos, w, h1, wts, bsz, t, t_pad):
    n, d = h1.shape
    bm = _row_block(t, 256)
    per_seq = t // bm
    n_steps = bsz * per_seq
    last = n_steps - 1
    vec = pl.BlockSpec((1, d), lambda i: (0, 0))
    smem = pltpu.SMEM
    pos3 = pos.reshape(bsz, t_pad, 2)[:, :t].reshape(n_steps, 1, 2 * bm)

    def rows(i):
        return (i // per_seq, i % per_seq, 0)

    return pl.pallas_call(
        functools.partial(_combine_kernel, n_steps=n_steps, bm=bm),
        grid=(n_steps,),
        in_specs=[pl.BlockSpec((1, 1, 2 * bm), lambda i: (i, 0, 0), memory_space=smem),
                  pl.BlockSpec((1, 1, 2 * bm), lambda i: (jnp.minimum(i + 1, last), 0, 0), memory_space=smem),
                  pl.BlockSpec(memory_space=pl.ANY),
                  pl.BlockSpec((1, bm, d), rows),
                  pl.BlockSpec((1, bm, 2), rows), vec, vec],
        out_specs=pl.BlockSpec((1, bm, d), rows),
        out_shape=jax.ShapeDtypeStruct((bsz, t, d), F32),
        scratch_shapes=[pltpu.VMEM((2, 2, bm) + eo.shape[1:], F32), pltpu.SemaphoreType.DMA((2,))],
        compiler_params=_cparams("arbitrary"),
        name="moe_combine",
    )(pos3, pos3, eo, h1.reshape(bsz, t_pad, d), w.reshape(bsz, t_pad, 2), wts['ln2_g'], wts['ln2_b'])


def _prepare_weights(p):
    l = 0
    w_in = p['w_in'][l]
    c0 = S5_WIDTH
    c1 = c0 + 3 * RW_WIDTH + 2 * RW_DECAY_LORA + 2 * RW_ICLR_LORA + RW_GATE_LORA
    gpad = RW_GATE_PAD - RW_GATE_LORA
    wts = {}
    wts['w_u'] = w_in[:, :c0].astype(BF16)
    wts['w_rw'] = jnp.pad(w_in[:, c0:c1], ((0, 0), (0, gpad))).astype(BF16)
    wts['w_gate'] = w_in[:, c1:].astype(BF16)
    wts['mu'] = jnp.pad(p['shift_mu'][l], ((0, 0), (0, gpad)))
    z = jnp.zeros((RW_DECAY_LORA, RW_WIDTH), F32)
    wts['w2'] = jnp.block([[p['rw_w2'][l, 0], z], [z, p['rw_w2'][l, 1]]]).astype(BF16)
    wts['a2'] = jnp.block([[p['rw_a2'][l, 0], z], [z, p['rw_a2'][l, 1]]]).astype(BF16)
    wts['g2'] = jnp.pad(p['rw_g2'][l], ((0, gpad), (0, 0))).astype(BF16)
    wts['w0'] = p['rw_w0'][l]
    wts['a0'] = p['rw_a0'][l]
    wts['k_k'] = p['rw_k_k'][l].reshape(1, -1)
    wts['k_a'] = p['rw_k_a'][l].reshape(1, -1)
    wts['r_k'] = p['rw_r_k'][l].reshape(1, -1)
    wts['lnx_g'] = p['rw_lnx_g'][l].reshape(1, -1)
    wts['lnx_b'] = p['rw_lnx_b'][l].reshape(1, -1)
    head = jnp.arange(RW_WIDTH) // RW_HEAD
    e = (head[:, None] == jnp.arange(RW_HEADS)[None, :]).astype(BF16)
    wts['head_e'] = e
    wts['head_et'] = e.T
    wts['s5'] = _s5_block_operators(*_s5_matrices(
        p['s5_B_re'][l], p['s5_B_im'][l], p['s5_A_re'][l], p['s5_A_im'][l],
        p['s5_log_dt'][l], p['s5_C_re'][l], p['s5_C_im'][l]))
    wts['s5_D'] = p['s5_D'][l]
    wts['glu_w'] = p['s5_glu_w'][l].astype(BF16)
    wts['glu_b'] = p['s5_glu_b'][l]
    wts['proj_s5'] = p['proj_s5'][l].astype(BF16)
    wts['proj_rwkv'] = p['proj_rwkv'][l].astype(BF16)
    wts['gate_b'] = p['gate_b'][l]
    wts['w_out'] = p['w_out'][l].astype(BF16)
    wts['ln1_g'] = p['ln1_g'][l].reshape(1, -1)
    wts['ln1_b'] = p['ln1_b'][l].reshape(1, -1)
    router = jnp.concatenate([p['router_coarse'][l], p['router_fine'][l]], axis=1)
    router = jnp.pad(router, ((0, 0), (0, ROUTER_PAD - router.shape[1])))
    wts['router_hi'], wts['router_lo'] = _split(router)
    wts['router_coarse_b'] = p['router_coarse_b'][l]
    wts['router_fine_b'] = p['router_fine_b'][l]
    wts['exp_w_gate'] = p['exp_w_gate'][l].astype(BF16)
    wts['exp_w_up'] = p['exp_w_up'][l].astype(BF16)
    wts['exp_w_down'] = p['exp_w_down'][l].astype(BF16)
    wts['ln2_g'] = p['ln2_g'][l].reshape(1, -1)
    wts['ln2_b'] = p['ln2_b'][l].reshape(1, -1)
    return wts


def _encode(x, p, wts):
    bsz, t, d = x.shape
    t_pad = t + SEQ_TAIL
    n = bsz * t_pad
    h0, h0b = _ln_in(x, p['meta'], p['ln_in_g'], p['ln_in_b'])
    h0 = h0.reshape(n, d)
    h0b = h0b.reshape(n, d)
    bm = _row_block(t_pad, 1024)
    u = _mm(h0b, wts['w_u'], bm, S5_WIDTH, F32, "proj_s5_in")
    rw = _mm(h0b, wts['w_rw'], bm, RW_COLS // 4, F32, "proj_rwkv_in")
    y_ssm = _s5_ssm(u.reshape(bsz, t_pad, S5_WIDTH), wts['s5']).reshape(n, S5_WIDTH)
    s5_out = _s5_post(y_ssm, u, wts['s5_D'], wts['glu_w'], wts['glu_b'], bm)
    r, kk, v, g, bv, lw, kd, bb = _rwkv_prep(rw.reshape(bsz, t_pad, RW_COLS), wts, t)
    yf, yb = _rwkv_scan(r, kk, v, lw, kd, bb)
    rw_out = _rwkv_post(yf.reshape(n, -1), yb.reshape(n, -1), bv.reshape(n, -1), g.reshape(n, -1), wts, bm)
    merged = _merge(h0b, s5_out, rw_out, wts, bm)
    h1, logits = _out_proj(merged, h0, wts, _row_block(t_pad, 512))
    seq_pos = jnp.arange(n, dtype=jnp.int32) % t_pad
    valid = (seq_pos < t) | (seq_pos >= t_pad - N_META)
    row_tok, pos, w, blk_exp, n_used, n_blocks = _route(logits, wts, valid)
    eo = _moe_experts(h1, row_tok, blk_exp, n_used, n_blocks, wts)
    return _moe_combine(eo, pos, w, h1, wts, bsz, t, t_pad)


def kernel(x_prompt, x_sample, meta, ln_in_g, ln_in_b, w_in, shift_mu, s5_B_re, s5_B_im, s5_A_re, s5_A_im, s5_log_dt, s5_C_re, s5_C_im, s5_D, s5_glu_w, s5_glu_b, rw_w0, rw_w2, rw_a0, rw_a2, rw_g2, rw_k_k, rw_k_a, rw_r_k, rw_lnx_g, rw_lnx_b, proj_s5, proj_rwkv, gate_b, w_out, ln1_g, ln1_b, router_coarse, router_coarse_b, router_fine, router_fine_b, exp_w_gate, exp_w_up, exp_w_down, ln2_g, ln2_b):
    p = {
        'meta': meta, 'ln_in_g': ln_in_g, 'ln_in_b': ln_in_b, 'w_in': w_in, 'shift_mu': shift_mu,
        's5_B_re': s5_B_re, 's5_B_im': s5_B_im, 's5_A_re': s5_A_re, 's5_A_im': s5_A_im,
        's5_log_dt': s5_log_dt, 's5_C_re': s5_C_re, 's5_C_im': s5_C_im, 's5_D': s5_D,
        's5_glu_w': s5_glu_w, 's5_glu_b': s5_glu_b,
        'rw_w0': rw_w0, 'rw_w2': rw_w2, 'rw_a0': rw_a0, 'rw_a2': rw_a2, 'rw_g2': rw_g2,
        'rw_k_k': rw_k_k, 'rw_k_a': rw_k_a, 'rw_r_k': rw_r_k, 'rw_lnx_g': rw_lnx_g, 'rw_lnx_b': rw_lnx_b,
        'proj_s5': proj_s5, 'proj_rwkv': proj_rwkv, 'gate_b': gate_b, 'w_out': w_out,
        'ln1_g': ln1_g, 'ln1_b': ln1_b,
        'router_coarse': router_coarse, 'router_coarse_b': router_coarse_b,
        'router_fine': router_fine, 'router_fine_b': router_fine_b,
        'exp_w_gate': exp_w_gate, 'exp_w_up': exp_w_up, 'exp_w_down': exp_w_down,
        'ln2_g': ln2_g, 'ln2_b': ln2_b,
    }
    wts = _prepare_weights(p)
    return (_encode(x_prompt, p, wts), _encode(x_sample, p, wts))
```

```python
import functools
import math

import jax
import jax.numpy as jnp
from jax import lax
from jax.experimental import pallas as pl
from jax.experimental.pallas import tpu as pltpu

F32 = jnp.float32
BF16 = jnp.bfloat16

D_MODEL = 2048
N_META = 16
S5_WIDTH = 1024
S5_GROUP = 16
S5_GROUPS = 64
S5_STATE = 64
S5_GB = 8
S5_PAIR_MAX_CHUNKS = 160
S5_CHUNK = 16
RW_WIDTH = 1024
RW_HEAD = 64
RW_HEADS = 16
RW_DECAY_LORA = 64
RW_ICLR_LORA = 64
RW_GATE_LORA = 160
RW_GATE_PAD = 256
RW_COLS = 3 * RW_WIDTH + 2 * RW_DECAY_LORA + 2 * RW_ICLR_LORA + RW_GATE_PAD
RW_CHUNK = 64
RW_HEADS_PER_STEP = 2
MOE_GROUPS = 4
EXPERTS_PER_GROUP = 8
N_EXPERTS = 32
D_EXPERT = 512
MOE_ROWS = 256
ROUTER_PAD = 128
DEPTH = 1
ALPHA = (2 * DEPTH) ** 0.25
LN_EPS = 1e-5
GN_EPS = 64e-5
SEQ_TAIL = 64
VMEM_LIMIT = 56 * 1024 * 1024


def _cparams(*sem):
    return pltpu.CompilerParams(dimension_semantics=sem, vmem_limit_bytes=VMEM_LIMIT)


def _row_block(t_pad, cap, mult=8):
    best = mult
    for d in range(mult, cap + 1, mult):
        if t_pad % d == 0:
            best = d
    return best


def _dot(a, b):
    return jnp.dot(a, b, preferred_element_type=F32)


def _dot_nt(a, b):
    return lax.dot_general(a, b, (((1,), (1,)), ((), ())), preferred_element_type=F32)


def _dot_tn(a, b):
    return lax.dot_general(a, b, (((0,), (0,)), ((), ())), preferred_element_type=F32)


def _split(x):
    hi = x.astype(BF16)
    lo = (x - hi.astype(F32)).astype(BF16)
    return hi, lo


def _layernorm(x, g, b):
    mu = jnp.mean(x, axis=-1, keepdims=True)
    xc = x - mu
    var = jnp.mean(xc * xc, axis=-1, keepdims=True)
    return xc * lax.rsqrt(var + LN_EPS) * g + b


def _ln_in_kernel(x_ref, m_ref, g_ref, b_ref, of_ref, ob_ref, *, n_token_blocks):
    j = pl.program_id(1)
    bx, d = x_ref.shape[1], x_ref.shape[2]

    @pl.when(j < n_token_blocks)
    def _():
        y = _layernorm(x_ref[0], g_ref[...], b_ref[...])
        of_ref[0] = y
        ob_ref[0] = y.astype(BF16)

    @pl.when(j == n_token_blocks)
    def _():
        ym = _layernorm(m_ref[...], g_ref[...], b_ref[...])
        y = jnp.concatenate([jnp.zeros((SEQ_TAIL - N_META, d), F32), ym, jnp.zeros((bx - SEQ_TAIL, d), F32)],
                            axis=0)
        of_ref[0] = y
        ob_ref[0] = y.astype(BF16)


def _ln_in(x, meta, g, b):
    bsz, t, d = x.shape
    assert t % SEQ_TAIL == 0
    t_pad = t + SEQ_TAIL
    bx = _row_block(t, 512, SEQ_TAIL)
    nxb = t // bx
    row = pl.BlockSpec((1, bx, d), lambda i, j: (i, j, 0))
    vec = pl.BlockSpec((1, d), lambda i, j: (0, 0))
    return pl.pallas_call(
        functools.partial(_ln_in_kernel, n_token_blocks=nxb),
        grid=(bsz, nxb + 1),
        in_specs=[pl.BlockSpec((1, bx, d), lambda i, j: (i, jnp.minimum(j, nxb - 1), 0)),
                  pl.BlockSpec((N_META, d), lambda i, j: (0, 0)), vec, vec],
        out_specs=[row, row],
        out_shape=[jax.ShapeDtypeStruct((bsz, t_pad, d), F32), jax.ShapeDtypeStruct((bsz, t_pad, d), BF16)],
        compiler_params=_cparams("parallel", "parallel"),
        name="ln_in",
    )(x, meta, g.reshape(1, d), b.reshape(1, d))


def _mm_kernel(x_ref, w_ref, o_ref):
    o_ref[...] = _dot(x_ref[...], w_ref[...]).astype(o_ref.dtype)


def _mm(x, w, bm, bn, out_dtype, name):
    n, k = x.shape
    m = w.shape[1]
    return pl.pallas_call(
        _mm_kernel,
        grid=(m // bn, n // bm),
        in_specs=[pl.BlockSpec((bm, k), lambda j, i: (i, 0)),
                  pl.BlockSpec((k, bn), lambda j, i: (0, j))],
        out_specs=pl.BlockSpec((bm, bn), lambda j, i: (i, j)),
        out_shape=jax.ShapeDtypeStruct((n, m), out_dtype),
        compiler_params=_cparams("parallel", "parallel"),
        name=name,
    )(x, w)


def _mm_slab_kernel(x_ref, w_ref, o_ref):
    res = _dot(x_ref[...], w_ref[...])
    for g in range(o_ref.shape[0]):
        o_ref[g] = res[:, g * 128:(g + 1) * 128]


def _mm_slabs(x, w, bm, name):
    n, k = x.shape
    m = w.shape[1]
    return pl.pallas_call(
        _mm_slab_kernel,
        grid=(n // bm,),
        in_specs=[pl.BlockSpec((bm, k), lambda i: (i, 0)), pl.BlockSpec((k, m), lambda i: (0, 0))],
        out_specs=pl.BlockSpec((m // 128, bm, 128), lambda i: (0, i, 0)),
        out_shape=jax.ShapeDtypeStruct((m // 128, n, 128), F32),
        compiler_params=_cparams("parallel"),
        name=name,
    )(x, w)


def _s5_matrices(b_re, b_im, a_re, a_im, log_dt, c_re, c_im):
    L = S5_CHUNK
    dt = jnp.exp(log_dt)[..., None]
    mag = jnp.exp(a_re * dt)
    abr = mag * jnp.cos(a_im * dt)
    abi = mag * jnp.sin(a_im * dt)
    den = a_re * a_re + a_im * a_im
    nr = abr - 1.0
    cr = (nr * a_re + abi * a_im) / den
    ci = (abi * a_re - nr * a_im) / den
    bbr = cr[..., None] * b_re - ci[..., None] * b_im
    bbi = cr[..., None] * b_im + ci[..., None] * b_re
    tau = jnp.arange(L + 1, dtype=F32)[:, None, None, None]
    pmag = jnp.exp(tau * a_re * dt)
    pr = pmag * jnp.cos(tau * a_im * dt)
    pi = pmag * jnp.sin(tau * a_im * dt)
    wr = pr[..., None] * bbr - pi[..., None] * bbi
    wi = pr[..., None] * bbi + pi[..., None] * bbr
    kern = (jnp.einsum('zgop,tzgpi->tzgoi', c_re, wr)
            - jnp.einsum('zgop,tzgpi->tzgoi', c_im, wi))
    s = jnp.arange(L)[:, None]
    t = jnp.arange(L)[None, :]
    lag = t - s
    kf = jnp.where((lag >= 0)[..., None, None, None], kern[jnp.clip(lag, 0, L), 0], 0.0)
    kb = jnp.where((lag <= 0)[..., None, None, None], kern[jnp.clip(-lag, 0, L), 1], 0.0)
    toep = (kf + kb).transpose(2, 0, 4, 1, 3).reshape(S5_GROUPS, L * S5_GROUP, L * S5_GROUP)
    wf_r, wf_i = wr[::-1][1:, 0], wi[::-1][1:, 0]
    wb_r, wb_i = wr[:L, 1], wi[:L, 1]
    bmat = jnp.concatenate([wf_r, wb_r, wf_i, wb_i], axis=2)
    bmat = bmat.transpose(1, 0, 3, 2).reshape(S5_GROUPS, L * S5_GROUP, 4 * S5_STATE)
    pf_r, pf_i = pr[1:, 0], pi[1:, 0]
    pb_r, pb_i = pr[::-1][:L, 1], pi[::-1][:L, 1]
    c0r, c0i, c1r, c1i = c_re[0], c_im[0], c_re[1], c_im[1]

    def cpow(cre, cim, p_r, p_i):
        re = cre[None] * p_r[:, :, None, :] - cim[None] * p_i[:, :, None, :]
        im = cre[None] * p_i[:, :, None, :] + cim[None] * p_r[:, :, None, :]
        return re, -im

    f_re, f_im = cpow(c0r, c0i, pf_r, pf_i)
    g_re, g_im = cpow(c1r, c1i, pb_r, pb_i)
    cmat = jnp.concatenate([f_re, g_re, f_im, g_im], axis=3)
    cmat = cmat.transpose(1, 3, 0, 2).reshape(S5_GROUPS, 4 * S5_STATE, L * S5_GROUP)
    lam_re = jnp.concatenate([pr[L, 0], pr[L, 1]], axis=-1)[:, None, :]
    lam_im = jnp.concatenate([pi[L, 0], pi[L, 1]], axis=-1)[:, None, :]
    return bmat, toep, cmat, lam_re, lam_im


def _s5_expand_kernel(src_ref, e_ref, o_ref, *, pieces):
    for j in range(src_ref.shape[0]):
        x = _dot(src_ref[j].astype(BF16), e_ref[j]).astype(BF16)
        for src, rows, dst, step in pieces:
            o_ref[0, dst + j * step:dst + j * step + rows, :] = x[src:src + rows, :]


def _s5_expand(per_group, expand, pieces):
    groups, rows, cols = per_group.shape
    gb, _, wide = expand.shape
    return pl.pallas_call(
        functools.partial(_s5_expand_kernel, pieces=pieces),
        grid=(groups // gb,),
        in_specs=[pl.BlockSpec((gb, rows, cols), lambda g: (g, 0, 0)),
                  pl.BlockSpec((gb, cols, wide), lambda g: (0, 0, 0))],
        out_specs=pl.BlockSpec((1, gb * rows, wide), lambda g: (g, 0, 0)),
        out_shape=jax.ShapeDtypeStruct((groups // gb, gb * rows, wide), BF16),
        compiler_params=_cparams("parallel"),
        name="s5_expand",
    )(per_group, expand)


def _s5_block_operators(bmat, toep, cmat, lam_re, lam_im):
    nb, gb, L, c, p = S5_GROUPS // S5_GB, S5_GB, S5_CHUNK, S5_GROUP, S5_STATE
    quarters = (0, 2, 1, 3)
    wide = L * gb * c
    r = jnp.arange(L * c)[None, :, None]
    col = jnp.arange(wide)[None, None, :]
    j = jnp.arange(gb)[:, None, None]
    e_tok = ((r // c == col // (gb * c)) & ((col // c) % gb == j) & (r % c == col % c)).astype(BF16)
    k_of_col = jnp.array(quarters)[col // (gb * p)]
    e_state = ((r // p == k_of_col) & ((col // p) % gb == j) & (r % p == col % p)).astype(BF16)
    tok_rows = tuple((s * c, c, s * gb * c, c) for s in range(L))
    state_rows = tuple((quarters[k] * p, p, k * gb * p, p) for k in range(4))
    wb = _s5_expand(bmat, e_state, tok_rows)
    wt = _s5_expand(toep, e_tok, tok_rows)
    wc = _s5_expand(cmat, e_tok, state_rows)
    ar_f = lam_re[:, 0, :p].reshape(nb, gb * p)
    ar_b = lam_re[:, 0, p:].reshape(nb, gb * p)
    ai_f = lam_im[:, 0, :p].reshape(nb, gb * p)
    ai_b = lam_im[:, 0, p:].reshape(nb, gb * p)
    a1 = jnp.stack([ar_f, ar_f, ar_b, ar_b] * 2, axis=1)
    a2 = jnp.stack([-ai_f, ai_f, -ai_b, ai_b] * 2, axis=1)
    return wb, wt, wc, a1, a2


def _s5_kernel(u_ref, wb_ref, wt_ref, wc_ref, a1_ref, a2_ref, y_ref, s_scr, xf_scr, xb_scr, *, n_chunks):
    L = S5_CHUNK
    C = n_chunks
    nseq = u_ref.shape[1]
    cp = -(-C // 8) * 8
    sw = S5_GB * S5_STATE
    tail = SEQ_TAIL // L

    def rows_of(parts):
        if cp > C:
            pad = jnp.zeros((cp - C, parts[0].shape[1]), F32)
            parts = [x for part in parts for x in (part, pad)]
        return jnp.concatenate(parts, axis=0).astype(BF16)

    u8 = rows_of([jnp.concatenate([u_ref[0, q, pl.ds(s, C, stride=L), :] for s in range(L)], axis=1)
                  for q in range(nseq)])
    s_all = _dot(u8, wb_ref[0])
    for q in range(nseq):
        for k in range(4):
            s_scr[:, 4 * q + k, :] = s_all[q * cp:q * cp + C, k * sw:(k + 1) * sw]
    if nseq == 1:
        s_scr[:, 4:8, :] = jnp.zeros((C, 4, sw), F32)
    a1 = a1_ref[0]
    a2 = a2_ref[0]
    row = lax.broadcasted_iota(jnp.int32, (8, sw), 0)
    even = (row % 2) == 0
    is_fwd = (row % 4) < 2

    def step(c, x):
        cf = (c + C - tail) % C
        cb = (2 * C - 1 - c - tail) % C
        xf_scr[cf] = x
        xb_scr[cb] = x
        s = jnp.where(is_fwd, s_scr[cf], s_scr[cb])
        swapped = jnp.where(even, pltpu.roll(x, 7, 0), pltpu.roll(x, 1, 0))
        return a1 * x + a2 * swapped + s

    lax.fori_loop(0, C, step, jnp.zeros((8, sw), F32))
    x_in = rows_of([jnp.concatenate([xf_scr[:, 4 * q, :], xf_scr[:, 4 * q + 1, :],
                                     xb_scr[:, 4 * q + 2, :], xb_scr[:, 4 * q + 3, :]], axis=1)
                    for q in range(nseq)])
    y8 = _dot(u8, wt_ref[0]) + _dot(x_in, wc_ref[0])
    for q in range(nseq):
        for t in range(L):
            y_ref[0, q, pl.ds(t, C, stride=L), :] = y8[q * cp:q * cp + C, t * 128:(t + 1) * 128]


def _s5_ssm(u3, ops):
    wb, wt, wc, a1, a2 = ops
    _, bsz, t_pad, _ = u3.shape
    n_chunks = t_pad // S5_CHUNK
    nb = S5_GROUPS // S5_GB
    lanes = S5_GB * S5_GROUP
    sw = S5_GB * S5_STATE
    nseq = 2 if (bsz % 2 == 0 and n_chunks <= S5_PAIR_MAX_CHUNKS) else 1
    blk = pl.BlockSpec((1, nseq, t_pad, lanes), lambda g, b: (g, b, 0, 0))
    mat = pl.BlockSpec((1,) + wb.shape[1:], lambda g, b: (g, 0, 0), pipeline_mode=pl.Buffered(1))
    vec = pl.BlockSpec((1, 8, sw), lambda g, b: (g, 0, 0))
    return pl.pallas_call(
        functools.partial(_s5_kernel, n_chunks=n_chunks),
        grid=(nb, bsz // nseq),
        in_specs=[blk, mat, mat, mat, vec, vec],
        out_specs=blk,
        out_shape=jax.ShapeDtypeStruct(u3.shape, F32),
        scratch_shapes=[pltpu.VMEM((n_chunks, 8, sw), F32)] * 3,
        compiler_params=_cparams("arbitrary", "arbitrary"),
        name="s5_ssm",
    )(u3, wb, wt, wc, a1, a2)


def _s5_post_kernel(y_ref, u_ref, d_ref, w_ref, b_ref, o_ref):
    slabs = range(y_ref.shape[0])
    y = (jnp.concatenate([y_ref[g] for g in slabs], axis=1)
         + jnp.concatenate([u_ref[g] for g in slabs], axis=1) * d_ref[...])
    act = y * (0.5 * (1.0 + jnp.tanh(math.sqrt(2.0 / math.pi) * (y + 0.044715 * (y * y * y)))))
    z = _dot(act.astype(BF16), w_ref[...]) + b_ref[...]
    o_ref[...] = (act * jax.nn.sigmoid(z)).astype(o_ref.dtype)


def _s5_post(y, u, d_skip, glu_w, glu_b, bm):
    slabs, n, lanes = y.shape
    row = pl.BlockSpec((bm, S5_WIDTH), lambda i: (i, 0))
    slab = pl.BlockSpec((slabs, bm, lanes), lambda i: (0, i, 0))
    vec = pl.BlockSpec((1, S5_WIDTH), lambda i: (0, 0))
    return pl.pallas_call(
        _s5_post_kernel,
        grid=(n // bm,),
        in_specs=[slab, slab, vec, pl.BlockSpec((S5_WIDTH, S5_WIDTH), lambda i: (0, 0)), vec],
        out_specs=row,
        out_shape=jax.ShapeDtypeStruct((n, S5_WIDTH), BF16),
        compiler_params=_cparams("parallel"),
        name="s5_post",
    )(y, u, d_skip.reshape(1, -1), glu_w, glu_b.reshape(1, -1))


def _head_sum(x, e_ref, et_ref):
    hi, lo = _split(x)
    s = _dot(hi, e_ref[...]) + _dot(lo, e_ref[...])
    shi, slo = _split(s)
    return _dot(shi, et_ref[...]) + _dot(slo, et_ref[...])


def _rwkv_prep_kernel(cur_ref, prev_ref, next_ref, mu_ref, w2_ref, a2_ref, g2_ref, w0_ref, a0_ref,
                      kk_ref, ka_ref, rk_ref, e_ref, et_ref,
                      r_o, kk_o, v_o, g_o, bv_o, lw_o, kd_o, bb_o, *, t, t_pad, bm):
    j = pl.program_id(1)
    p = cur_ref[0]
    row = lax.broadcasted_iota(jnp.int32, (bm, 1), 0)
    prev_row = prev_ref[0, 7:8, :]
    next_row = next_ref[0, 0:1, :]
    prev = jnp.where(row == 0, prev_row, pltpu.roll(p, 1, 0))
    nxt = jnp.where(row == bm - 1, next_row, pltpu.roll(p, bm - 1, 0))
    xs = p + mu_ref[0:1, :] * (prev - p) + mu_ref[1:2, :] * (nxt - p)
    w = RW_WIDTH
    r = xs[:, 0:w]
    k = xs[:, w:2 * w]
    v = xs[:, 2 * w:3 * w]
    lw = xs[:, 3 * w:3 * w + 128]
    la = xs[:, 3 * w + 128:3 * w + 256]
    lg = xs[:, 3 * w + 256:]
    w_log = _dot(jnp.tanh(lw).astype(BF16), w2_ref[...])
    a_lin = _dot(la.astype(BF16), a2_ref[...])
    g = _dot(jax.nn.sigmoid(lg).astype(BF16), g2_ref[...])
    kk = k * kk_ref[...]
    n2 = _head_sum(kk * kk, e_ref, et_ref)
    kk = kk / jnp.maximum(jnp.sqrt(n2), 1e-12)
    pos = j * bm + row
    valid = (pos < t) | (pos >= t_pad - N_META)
    v = jnp.where(valid, v, 0.0)
    kd_sum = jnp.zeros_like(k)
    for z in range(2):
        wl = w_log[:, z * w:(z + 1) * w] + w0_ref[z:z + 1, :]
        lw_o[z, 0] = -math.exp(-0.5) * jax.nn.sigmoid(wl)
        a = jax.nn.sigmoid(a_lin[:, z * w:(z + 1) * w] + a0_ref[z:z + 1, :])
        kd = k * (1.0 + (a - 1.0) * ka_ref[...])
        kd_o[z, 0] = kd.astype(BF16)
        bb_o[z, 0] = (kk * a).astype(BF16)
        kd_sum = kd_sum + kd
    bonus = _head_sum(r * kd_sum * rk_ref[...], e_ref, et_ref)
    r_o[0] = r.astype(BF16)
    kk_o[0] = kk.astype(BF16)
    v_o[0] = v.astype(BF16)
    g_o[0] = g.astype(BF16)
    bv_o[0] = (bonus * v).astype(BF16)


def _rwkv_prep(rw3, wts, t):
    bsz, t_pad, _ = rw3.shape
    bm = _row_block(t_pad, 320, 64)
    nb8 = bm // 8
    n8 = t_pad // 8
    w = RW_WIDTH
    cur = pl.BlockSpec((1, bm, RW_COLS), lambda b, j: (b, j, 0))
    prev = pl.BlockSpec((1, 8, RW_COLS), lambda b, j: (b, (j * nb8 + n8 - 1) % n8, 0))
    nxt = pl.BlockSpec((1, 8, RW_COLS), lambda b, j: (b, ((j + 1) * nb8) % n8, 0))

    def full(a):
        return pl.BlockSpec(a.shape, lambda b, j: (0,) * a.ndim)

    shared = pl.BlockSpec((1, bm, w), lambda b, j: (b, j, 0))
    per_dir = pl.BlockSpec((2, 1, bm, w), lambda b, j: (0, b, j, 0))
    consts = [wts['mu'], wts['w2'], wts['a2'], wts['g2'], wts['w0'], wts['a0'],
              wts['k_k'], wts['k_a'], wts['r_k'], wts['head_e'], wts['head_et']]
    sds = jax.ShapeDtypeStruct
    return pl.pallas_call(
        functools.partial(_rwkv_prep_kernel, t=t, t_pad=t_pad, bm=bm),
        grid=(bsz, t_pad // bm),
        in_specs=[cur, prev, nxt] + [full(a) for a in consts],
        out_specs=[shared] * 5 + [per_dir] * 3,
        out_shape=[sds((bsz, t_pad, w), BF16)] * 5
        + [sds((2, bsz, t_pad, w), F32), sds((2, bsz, t_pad, w), BF16), sds((2, bsz, t_pad, w), BF16)],
        compiler_params=_cparams("parallel", "parallel"),
        name="rwkv_prep",
    )(rw3, rw3, rw3, *consts)


def _rwkv_chunk(fwd, r_ref, kk_ref, v_ref, lw_ref, kd_ref, bb_ref, y_ref, st_ref, z):
    L = RW_CHUNK
    hd = RW_HEAD
    gw = RW_HEADS_PER_STEP * hd
    n_groups = RW_WIDTH // gw
    row = lax.broadcasted_iota(jnp.int32, (L, L), 0)
    col = lax.broadcasted_iota(jnp.int32, (L, L), 1)
    tri = jnp.where((col <= row) if fwd else (col >= row), 1.0, 0.0).astype(BF16)
    grow = lax.broadcasted_iota(jnp.int32, (L, gw), 0)
    gcol = lax.broadcasted_iota(jnp.int32, (L, gw), 1) % L
    incl = (gcol <= grow) if fwd else (gcol >= grow)
    strict = (gcol < grow) if fwd else (gcol > grow)
    bd_mask = jnp.where(lax.broadcasted_iota(jnp.int32, (gw, gw), 0) // hd
                        == lax.broadcasted_iota(jnp.int32, (gw, gw), 1) // hd, 1.0, 0.0).astype(BF16)

    def bd(x):
        return jnp.concatenate([x] * RW_HEADS_PER_STEP, axis=0) * bd_mask

    def stack(x):
        return jnp.concatenate([x[:, h * hd:(h + 1) * hd] for h in range(RW_HEADS_PER_STEP)], axis=0)

    lw = lw_ref[0, 0]
    lw_hi, lw_lo = _split(lw)
    c = _dot(tri, lw_hi) + _dot(tri, lw_lo)
    e = c - lw
    c_tot = c[L - 1:L, :] if fwd else c[0:1, :]
    r = r_ref[0].astype(F32)
    kk = kk_ref[0].astype(F32)
    kd = kd_ref[0, 0].astype(F32)
    bb = bb_ref[0, 0].astype(F32)
    v = v_ref[0]
    q1 = (kk * jnp.exp(e)).astype(BF16)
    q2 = (r * jnp.exp(c)).astype(BF16)
    inv = jnp.exp(-c)
    k1 = (kd * inv).astype(BF16)
    k2 = (bb * inv).astype(BF16)
    rest = jnp.exp(c_tot - c)
    k1p = (kd * rest).astype(BF16)
    k2p = (bb * rest).astype(BF16)
    dec_tot = jnp.exp(c_tot)
    def group_chain(g):
        sl = slice(g * gw, (g + 1) * gw)
        s0 = st_ref[z, g]
        s0_hi, s0_lo = _split(s0)
        lhs = jnp.concatenate([q1[:, sl], q2[:, sl]], axis=0)
        rhs = jnp.concatenate([bd(k1[:, sl]), bd(k2[:, sl]), bd(s0_hi), bd(s0_lo)], axis=0)
        m1 = _dot_nt(lhs, rhs)
        yield
        a_kd = jnp.where(strict, m1[:L, 0:gw], 0.0)
        a_b = jnp.where(strict, m1[:L, gw:2 * gw], 0.0)
        q1s = m1[:L, 2 * gw:3 * gw] + m1[:L, 3 * gw:]
        b_kd = jnp.where(incl, m1[L:, 0:gw], 0.0)
        b_b = jnp.where(incl, m1[L:, gw:2 * gw], 0.0)
        q2s = m1[L:, 2 * gw:3 * gw] + m1[L:, 3 * gw:]
        vg = v[:, sl]
        v_bd = bd(vg)
        x = q1s + _dot(a_kd.astype(BF16), v_bd)
        yield
        m = -a_b
        levels = L.bit_length() - 1
        for lvl in range(levels):
            mb = m.astype(BF16)
            if lvl < levels - 1:
                rr = _dot(mb, jnp.concatenate([bd(x.astype(BF16)), bd(mb)], axis=1))
                x = x + rr[:, :gw]
                m = rr[:, gw:]
            else:
                x = x + _dot(mb, bd(x.astype(BF16)))
            yield
        ub = x.astype(BF16)
        y = q2s + _dot(jnp.concatenate([b_kd, -b_b], axis=1).astype(BF16),
                       jnp.concatenate([v_bd, bd(ub)], axis=0))
        y_ref[0, :, sl] = y
        yield
        st_ref[z, g] = s0 * dec_tot[:, sl] + _dot_tn(
            jnp.concatenate([stack(vg), stack(ub)], axis=0),
            jnp.concatenate([bd(k1p[:, sl]), -bd(k2p[:, sl])], axis=0))

    return [group_chain(g) for g in range(n_groups)]


def _rwkv_scan_kernel(rf, kkf, vf, lwf, kdf, bbf, rb, kkb, vb, lwb, kdb, bbb, yf_ref, yb_ref, st_ref):
    @pl.when(pl.program_id(1) == 0)
    def _():
        st_ref[...] = jnp.zeros_like(st_ref)

    chains = (_rwkv_chunk(True, rf, kkf, vf, lwf, kdf, bbf, yf_ref, st_ref, 0)
              + _rwkv_chunk(False, rb, kkb, vb, lwb, kdb, bbb, yb_ref, st_ref, 1))
    while chains:
        alive = []
        for chain in chains:
            try:
                next(chain)
                alive.append(chain)
            except StopIteration:
                pass
        chains = alive


def _rwkv_scan(r, kk, v, lw, kd, bb):
    bsz, t_pad, w = r.shape
    L = RW_CHUNK
    nc = t_pad // L
    gw = RW_HEADS_PER_STEP * RW_HEAD
    grid = (bsz, nc)

    def block(fwd, j):
        logical = j if fwd else nc - 1 - j
        return (logical + nc - 1) % nc

    def shared(fwd):
        return pl.BlockSpec((1, L, w), lambda b, j: (b, block(fwd, j), 0))

    def per_dir(fwd):
        return pl.BlockSpec((1, 1, L, w), lambda b, j: (0 if fwd else 1, b, block(fwd, j), 0))

    in_specs = []
    for fwd in (True, False):
        in_specs += [shared(fwd), shared(fwd), shared(fwd), per_dir(fwd), per_dir(fwd), per_dir(fwd)]
    return pl.pallas_call(
        _rwkv_scan_kernel,
        grid=grid,
        in_specs=in_specs,
        out_specs=[shared(True), shared(False)],
        out_shape=[jax.ShapeDtypeStruct((bsz, t_pad, w), F32)] * 2,
        scratch_shapes=[pltpu.VMEM((2, w // gw, RW_HEAD, gw), F32)],
        compiler_params=_cparams("parallel", "arbitrary"),
        name="rwkv_scan",
    )(r, kk, v, lw, kd, bb, r, kk, v, lw, kd, bb)


def _rwkv_post_kernel(yf_ref, yb_ref, bv_ref, g_ref, lg_ref, lb_ref, e_ref, et_ref, o_ref):
    y = yf_ref[...] + yb_ref[...]
    mean = _head_sum(y, e_ref, et_ref) * (1.0 / RW_HEAD)
    yc = y - mean
    var = _head_sum(yc * yc, e_ref, et_ref) * (1.0 / RW_HEAD)
    y = yc * lax.rsqrt(var + GN_EPS) * lg_ref[...] + lb_ref[...]
    o_ref[...] = ((y + bv_ref[...].astype(F32)) * g_ref[...].astype(F32)).astype(o_ref.dtype)


def _rwkv_post(yf, yb, bv, g, wts, bm):
    n, w = yf.shape
    row = pl.BlockSpec((bm, w), lambda i: (i, 0))
    vec = pl.BlockSpec((1, w), lambda i: (0, 0))
    e, et = wts['head_e'], wts['head_et']
    return pl.pallas_call(
        _rwkv_post_kernel,
        grid=(n // bm,),
        in_specs=[row, row, row, row, vec, vec,
                  pl.BlockSpec(e.shape, lambda i: (0, 0)), pl.BlockSpec(et.shape, lambda i: (0, 0))],
        out_specs=row,
        out_shape=jax.ShapeDtypeStruct((n, w), BF16),
        compiler_params=_cparams("parallel"),
        name="rwkv_post",
    )(yf, yb, bv, g, wts['lnx_g'], wts['lnx_b'], e, et)


def _merge_kernel(h_ref, s5_ref, rw_ref, wg0_ref, wg1_ref, gb_ref, p0_ref, p1_ref, o_ref):
    h = h_ref[...]
    g0 = jax.nn.sigmoid(_dot(h, wg0_ref[...]) + gb_ref[0:1, :])
    g1 = jax.nn.sigmoid(_dot(h, wg1_ref[...]) + gb_ref[1:2, :])
    merged = g0 * _dot(s5_ref[...], p0_ref[...]) + g1 * _dot(rw_ref[...], p1_ref[...])
    o_ref[...] = merged.astype(o_ref.dtype)


def _merge(h0b, s5_out, rw_out, wts, bm):
    n, d = h0b.shape
    bn = 1024
    nj = d // bn
    return pl.pallas_call(
        _merge_kernel,
        grid=(nj, n // bm),
        in_specs=[pl.BlockSpec((bm, d), lambda j, i: (i, 0)),
                  pl.BlockSpec((bm, S5_WIDTH), lambda j, i: (i, 0)),
                  pl.BlockSpec((bm, RW_WIDTH), lambda j, i: (i, 0)),
                  pl.BlockSpec((d, bn), lambda j, i: (0, j)),
                  pl.BlockSpec((d, bn), lambda j, i: (0, nj + j)),
                  pl.BlockSpec((2, bn), lambda j, i: (0, j)),
                  pl.BlockSpec((S5_WIDTH, bn), lambda j, i: (0, j)),
                  pl.BlockSpec((RW_WIDTH, bn), lambda j, i: (0, j))],
        out_specs=pl.BlockSpec((bm, bn), lambda j, i: (i, j)),
        out_shape=jax.ShapeDtypeStruct((n, d), BF16),
        compiler_params=_cparams("parallel", "parallel"),
        name="merge",
    )(h0b, s5_out, rw_out, wts['w_gate'], wts['w_gate'], wts['gate_b'], wts['proj_s5'], wts['proj_rwkv'])


def _out_kernel(m_ref, h_ref, w_ref, g_ref, b_ref, rh_ref, rl_ref, o_ref, lg_ref):
    bm = m_ref.shape[0]
    halves = [slice(0, bm // 2), slice(bm // 2, bm)]
    xs = [ALPHA * h_ref[rows, :] + _dot(m_ref[rows, :], w_ref[...]) for rows in halves]
    for rows, x in zip(halves, xs):
        h1 = _layernorm(x, g_ref[...], b_ref[...])
        o_ref[rows, :] = h1
        hi, lo = _split(h1)
        lg_ref[rows, :] = _dot(hi, rh_ref[...]) + _dot(lo, rh_ref[...]) + _dot(hi, rl_ref[...])


def _out_proj(merged, h0, wts, bm):
    n, d = h0.shape
    row = pl.BlockSpec((bm, d), lambda i: (i, 0))
    vec = pl.BlockSpec((1, d), lambda i: (0, 0))
    rt = pl.BlockSpec((d, ROUTER_PAD), lambda i: (0, 0))
    return pl.pallas_call(
        _out_kernel,
        grid=(n // bm,),
        in_specs=[row, row, pl.BlockSpec((d, d), lambda i: (0, 0)), vec, vec, rt, rt],
        out_specs=[row, pl.BlockSpec((bm, ROUTER_PAD), lambda i: (i, 0))],
        out_shape=[jax.ShapeDtypeStruct((n, d), F32), jax.ShapeDtypeStruct((n, ROUTER_PAD), F32)],
        compiler_params=_cparams("parallel"),
        name="out_proj",
    )(merged, h0, wts['w_out'], wts['ln1_g'], wts['ln1_b'], wts['router_hi'], wts['router_lo'])


def _route(logits, wts, valid):
    i32 = jnp.int32
    lc = logits[:, :MOE_GROUPS] + wts['router_coarse_b']
    grp = jnp.argmax(lc, axis=-1).astype(i32)
    gate_c = jnp.max(jax.nn.softmax(lc, axis=-1), axis=-1)
    lf = (logits[:, MOE_GROUPS:MOE_GROUPS + N_EXPERTS] + wts['router_fine_b'])
    lf = lf.reshape(-1, MOE_GROUPS, EXPERTS_PER_GROUP)
    sel = grp[:, None, None] == jnp.arange(MOE_GROUPS, dtype=i32)[None, :, None]
    lf = jnp.sum(jnp.where(sel, lf, 0.0), axis=1)
    lane = jnp.arange(EXPERTS_PER_GROUP, dtype=i32)[None, :]
    i1 = jnp.argmax(lf, axis=-1).astype(i32)
    v1 = jnp.max(lf, axis=-1)
    rest = jnp.where(lane == i1[:, None], -jnp.inf, lf)
    i2 = jnp.argmax(rest, axis=-1).astype(i32)
    v2 = jnp.max(rest, axis=-1)
    top_v = jnp.stack([v1, v2], axis=-1)
    top_i = jnp.stack([i1, i2], axis=-1)
    w = gate_c[:, None] * jax.nn.softmax(top_v, axis=-1)
    expert = grp[:, None] * EXPERTS_PER_GROUP + top_i
    expert = jnp.where(valid[:, None], expert, N_EXPERTS)
    w = jnp.where(valid[:, None], w, 0.0)
    n_tok = logits.shape[0]
    n_asg = 2 * n_tok
    e_flat = expert.reshape(-1)
    order = jnp.argsort(e_flat).astype(i32)
    inv = jnp.argsort(order).astype(i32)
    bounds = jnp.sum(e_flat[None, :] < jnp.arange(N_EXPERTS + 1, dtype=i32)[:, None], axis=1, dtype=i32)
    start = bounds[:N_EXPERTS]
    counts = bounds[1:] - start
    padded = (counts + MOE_ROWS - 1) // MOE_ROWS * MOE_ROWS
    ex = jnp.arange(N_EXPERTS, dtype=i32)
    pend = jnp.sum(jnp.where(ex[None, :] <= ex[:, None], padded[None, :], 0), axis=1)
    pstart = pend - padded
    n_blocks = -(-n_asg // MOE_ROWS) + N_EXPERTS
    n_rows = n_blocks * MOE_ROWS
    n_used = pend[-1] // MOE_ROWS
    blk = jnp.minimum(jnp.arange(n_blocks, dtype=i32), n_used - 1)
    blk_exp = jnp.sum(pend[None, :] <= (blk * MOE_ROWS)[:, None], axis=1, dtype=i32)
    blk_exp = jnp.minimum(blk_exp, N_EXPERTS - 1)
    experts = jnp.arange(N_EXPERTS, dtype=i32)

    def lookup(table, idx):
        return jnp.sum(jnp.where(idx[..., None] == experts, table, 0), axis=-1)

    pos = jnp.where(e_flat < N_EXPERTS, lookup(pstart - start, e_flat) + inv, 0)
    d = jnp.arange(n_rows, dtype=i32).reshape(n_blocks, MOE_ROWS)
    k = d - lookup(pstart, blk_exp)[:, None]
    src = jnp.clip(k + lookup(start, blk_exp)[:, None], 0, n_asg - 1)
    row_tok = jnp.where((k < lookup(counts, blk_exp)[:, None]) & (d < pend[-1]), order[src] // 2, 0)
    return row_tok, pos, w, blk_exp, n_used.reshape(1).astype(i32), n_blocks


def _row_gather_start(src_hbm, idx_ref, n_rows, dst, sem, stride=1, offset=0):
    for r in range(n_rows):
        row = idx_ref[0, 0, stride * r + offset]
        pltpu.make_async_copy(src_hbm.at[pl.ds(row, 1)], dst.at[pl.ds(r, 1)], sem).start(priority=r % 2)


def _row_gather_wait(src_hbm, n_rows, dst, sem):
    def wait(r, carry):
        pltpu.make_async_copy(src_hbm.at[pl.ds(0, 1)], dst.at[pl.ds(r, 1)], sem).wait()
        return carry

    lax.fori_loop(0, n_rows, wait, 0, unroll=8)


def _expert_kernel(nused_ref, bexp_ref, idx_ref, nidx_ref, x_hbm, wg_ref, wu_ref, wd_ref, o_ref, buf, sem):
    i = pl.program_id(0)
    n_used = nused_ref[0]
    slot = i % 2

    @pl.when((i == 0) & (n_used > 0))
    def _():
        _row_gather_start(x_hbm, idx_ref, MOE_ROWS, buf.at[0], sem.at[0])

    @pl.when(i + 1 < n_used)
    def _():
        _row_gather_start(x_hbm, nidx_ref, MOE_ROWS, buf.at[1 - slot], sem.at[1 - slot])

    @pl.when(i < n_used)
    def _():
        _row_gather_wait(x_hbm, MOE_ROWS, buf.at[slot], sem.at[slot])
        x = buf[slot].astype(BF16)
        hb = jax.nn.silu(_dot(x, wg_ref[0])) * _dot(x, wu_ref[0])
        o_ref[...] = _dot(hb.astype(BF16), wd_ref[0])

    @pl.when(i >= n_used)
    def _():
        o_ref[...] = jnp.zeros_like(o_ref)


def _moe_experts(h1, row_tok, blk_exp, n_used, n_blocks, wts):
    d = h1.shape[1]
    idx = row_tok.reshape(n_blocks, 1, MOE_ROWS)
    last = n_blocks - 1
    smem = pltpu.SMEM
    return pl.pallas_call(
        _expert_kernel,
        grid_spec=pltpu.PrefetchScalarGridSpec(
            num_scalar_prefetch=2,
            grid=(n_blocks,),
            in_specs=[pl.BlockSpec((1, 1, MOE_ROWS), lambda i, nu, be: (i, 0, 0), memory_space=smem),
                      pl.BlockSpec((1, 1, MOE_ROWS), lambda i, nu, be: (jnp.minimum(i + 1, last), 0, 0),
                                   memory_space=smem),
                      pl.BlockSpec(memory_space=pl.ANY),
                      pl.BlockSpec((1, d, D_EXPERT), lambda i, nu, be: (be[i], 0, 0)),
                      pl.BlockSpec((1, d, D_EXPERT), lambda i, nu, be: (be[i], 0, 0)),
                      pl.BlockSpec((1, D_EXPERT, d), lambda i, nu, be: (be[i], 0, 0))],
            out_specs=pl.BlockSpec((MOE_ROWS, d), lambda i, nu, be: (i, 0)),
            scratch_shapes=[pltpu.VMEM((2, MOE_ROWS, d), F32), pltpu.SemaphoreType.DMA((2,))],
        ),
        out_shape=jax.ShapeDtypeStruct((n_blocks * MOE_ROWS, d), F32),
        compiler_params=_cparams("arbitrary"),
        name="moe_experts",
    )(n_used, blk_exp, idx, idx, h1, wts['exp_w_gate'], wts['exp_w_up'], wts['exp_w_down'])


def _combine_kernel(pos_ref, npos_ref, eo_hbm, h_ref, w_ref, g_ref, b_ref, o_ref, buf, sem, *, n_steps, bm):
    i = pl.program_id(0)
    slot = i % 2

    def start(p_ref, s):
        for k in range(2):
            _row_gather_start(eo_hbm, p_ref, bm, buf.at[s, k], sem.at[s], stride=2, offset=k)

    @pl.when(i == 0)
    def _():
        start(pos_ref, 0)

    @pl.when(i + 1 < n_steps)
    def _():
        start(npos_ref, 1 - slot)

    for k in range(2):
        _row_gather_wait(eo_hbm, bm, buf.at[slot, k], sem.at[slot])
    w = w_ref[0]
    moe = w[:, 0:1] * buf[slot, 0] + w[:, 1:2] * buf[slot, 1]
    o_ref[0] = _layernorm(ALPHA * h_ref[0] + moe, g_ref[...], b_ref[...])


def _moe_combine(eo, pos, w, h1, wts, bsz, t, t_pad):
    n, d = h1.shape
    bm = _row_block(t, 256)
    per_seq = t // bm
    n_steps = bsz * per_seq
    last = n_steps - 1
    vec = pl.BlockSpec((1, d), lambda i: (0, 0))
    smem = pltpu.SMEM
    pos3 = pos.reshape(bsz, t_pad, 2)[:, :t].reshape(n_steps, 1, 2 * bm)

    def rows(i):
        return (i // per_seq, i % per_seq, 0)

    return pl.pallas_call(
        functools.partial(_combine_kernel, n_steps=n_steps, bm=bm),
        grid=(n_steps,),
        in_specs=[pl.BlockSpec((1, 1, 2 * bm), lambda i: (i, 0, 0), memory_space=smem),
                  pl.BlockSpec((1, 1, 2 * bm), lambda i: (jnp.minimum(i + 1, last), 0, 0), memory_space=smem),
                  pl.BlockSpec(memory_space=pl.ANY),
                  pl.BlockSpec((1, bm, d), rows),
                  pl.BlockSpec((1, bm, 2), rows), vec, vec],
        out_specs=pl.BlockSpec((1, bm, d), rows),
        out_shape=jax.ShapeDtypeStruct((bsz, t, d), F32),
        scratch_shapes=[pltpu.VMEM((2, 2, bm) + eo.shape[1:], F32), pltpu.SemaphoreType.DMA((2,))],
        compiler_params=_cparams("arbitrary"),
        name="moe_combine",
    )(pos3, pos3, eo, h1.reshape(bsz, t_pad, d), w.reshape(bsz, t_pad, 2), wts['ln2_g'], wts['ln2_b'])


def _prepare_weights(p):
    l = 0
    w_in = p['w_in'][l]
    c0 = S5_WIDTH
    c1 = c0 + 3 * RW_WIDTH + 2 * RW_DECAY_LORA + 2 * RW_ICLR_LORA + RW_GATE_LORA
    gpad = RW_GATE_PAD - RW_GATE_LORA
    wts = {}
    wts['w_u'] = w_in[:, :c0].astype(BF16)
    wts['w_rw'] = jnp.pad(w_in[:, c0:c1], ((0, 0), (0, gpad))).astype(BF16)
    wts['w_gate'] = w_in[:, c1:].astype(BF16)
    wts['mu'] = jnp.pad(p['shift_mu'][l], ((0, 0), (0, gpad)))
    z = jnp.zeros((RW_DECAY_LORA, RW_WIDTH), F32)
    wts['w2'] = jnp.block([[p['rw_w2'][l, 0], z], [z, p['rw_w2'][l, 1]]]).astype(BF16)
    wts['a2'] = jnp.block([[p['rw_a2'][l, 0], z], [z, p['rw_a2'][l, 1]]]).astype(BF16)
    wts['g2'] = jnp.pad(p['rw_g2'][l], ((0, gpad), (0, 0))).astype(BF16)
    wts['w0'] = p['rw_w0'][l]
    wts['a0'] = p['rw_a0'][l]
    wts['k_k'] = p['rw_k_k'][l].reshape(1, -1)
    wts['k_a'] = p['rw_k_a'][l].reshape(1, -1)
    wts['r_k'] = p['rw_r_k'][l].reshape(1, -1)
    wts['lnx_g'] = p['rw_lnx_g'][l].reshape(1, -1)
    wts['lnx_b'] = p['rw_lnx_b'][l].reshape(1, -1)
    head = jnp.arange(RW_WIDTH) // RW_HEAD
    e = (head[:, None] == jnp.arange(RW_HEADS)[None, :]).astype(BF16)
    wts['head_e'] = e
    wts['head_et'] = e.T
    wts['s5'] = _s5_block_operators(*_s5_matrices(
        p['s5_B_re'][l], p['s5_B_im'][l], p['s5_A_re'][l], p['s5_A_im'][l],
        p['s5_log_dt'][l], p['s5_C_re'][l], p['s5_C_im'][l]))
    wts['s5_D'] = p['s5_D'][l]
    wts['glu_w'] = p['s5_glu_w'][l].astype(BF16)
    wts['glu_b'] = p['s5_glu_b'][l]
    wts['proj_s5'] = p['proj_s5'][l].astype(BF16)
    wts['proj_rwkv'] = p['proj_rwkv'][l].astype(BF16)
    wts['gate_b'] = p['gate_b'][l]
    wts['w_out'] = p['w_out'][l].astype(BF16)
    wts['ln1_g'] = p['ln1_g'][l].reshape(1, -1)
    wts['ln1_b'] = p['ln1_b'][l].reshape(1, -1)
    router = jnp.concatenate([p['router_coarse'][l], p['router_fine'][l]], axis=1)
    router = jnp.pad(router, ((0, 0), (0, ROUTER_PAD - router.shape[1])))
    wts['router_hi'], wts['router_lo'] = _split(router)
    wts['router_coarse_b'] = p['router_coarse_b'][l]
    wts['router_fine_b'] = p['router_fine_b'][l]
    wts['exp_w_gate'] = p['exp_w_gate'][l].astype(BF16)
    wts['exp_w_up'] = p['exp_w_up'][l].astype(BF16)
    wts['exp_w_down'] = p['exp_w_down'][l].astype(BF16)
    wts['ln2_g'] = p['ln2_g'][l].reshape(1, -1)
    wts['ln2_b'] = p['ln2_b'][l].reshape(1, -1)
    return wts


def _encode(x, p, wts):
    bsz, t, d = x.shape
    t_pad = t + SEQ_TAIL
    n = bsz * t_pad
    h0, h0b = _ln_in(x, p['meta'], p['ln_in_g'], p['ln_in_b'])
    h0 = h0.reshape(n, d)
    h0b = h0b.reshape(n, d)
    bm = _row_block(t_pad, 1024)
    u = _mm_slabs(h0b, wts['w_u'], bm, "proj_s5_in")
    rw = _mm(h0b, wts['w_rw'], bm, RW_COLS // 4, F32, "proj_rwkv_in")
    slabs = u.shape[0]
    y_ssm = _s5_ssm(u.reshape(slabs, bsz, t_pad, 128), wts['s5']).reshape(slabs, n, 128)
    s5_out = _s5_post(y_ssm, u, wts['s5_D'], wts['glu_w'], wts['glu_b'], bm)
    r, kk, v, g, bv, lw, kd, bb = _rwkv_prep(rw.reshape(bsz, t_pad, RW_COLS), wts, t)
    yf, yb = _rwkv_scan(r, kk, v, lw, kd, bb)
    rw_out = _rwkv_post(yf.reshape(n, -1), yb.reshape(n, -1), bv.reshape(n, -1), g.reshape(n, -1), wts, bm)
    merged = _merge(h0b, s5_out, rw_out, wts, bm)
    h1, logits = _out_proj(merged, h0, wts, _row_block(t_pad, 512))
    seq_pos = jnp.arange(n, dtype=jnp.int32) % t_pad
    valid = (seq_pos < t) | (seq_pos >= t_pad - N_META)
    row_tok, pos, w, blk_exp, n_used, n_blocks = _route(logits, wts, valid)
    eo = _moe_experts(h1, row_tok, blk_exp, n_used, n_blocks, wts)
    return _moe_combine(eo, pos, w, h1, wts, bsz, t, t_pad)


def kernel(x_prompt, x_sample, meta, ln_in_g, ln_in_b, w_in, shift_mu, s5_B_re, s5_B_im, s5_A_re, s5_A_im, s5_log_dt, s5_C_re, s5_C_im, s5_D, s5_glu_w, s5_glu_b, rw_w0, rw_w2, rw_a0, rw_a2, rw_g2, rw_k_k, rw_k_a, rw_r_k, rw_lnx_g, rw_lnx_b, proj_s5, proj_rwkv, gate_b, w_out, ln1_g, ln1_b, router_coarse, router_coarse_b, router_fine, router_fine_b, exp_w_gate, exp_w_up, exp_w_down, ln2_g, ln2_b):
    p = {
        'meta': meta, 'ln_in_g': ln_in_g, 'ln_in_b': ln_in_b, 'w_in': w_in, 'shift_mu': shift_mu,
        's5_B_re': s5_B_re, 's5_B_im': s5_B_im, 's5_A_re': s5_A_re, 's5_A_im': s5_A_im,
        's5_log_dt': s5_log_dt, 's5_C_re': s5_C_re, 's5_C_im': s5_C_im, 's5_D': s5_D,
        's5_glu_w': s5_glu_w, 's5_glu_b': s5_glu_b,
        'rw_w0': rw_w0, 'rw_w2': rw_w2, 'rw_a0': rw_a0, 'rw_a2': rw_a2, 'rw_g2': rw_g2,
        'rw_k_k': rw_k_k, 'rw_k_a': rw_k_a, 'rw_r_k': rw_r_k, 'rw_lnx_g': rw_lnx_g, 'rw_lnx_b': rw_lnx_b,
        'proj_s5': proj_s5, 'proj_rwkv': proj_rwkv, 'gate_b': gate_b, 'w_out': w_out,
        'ln1_g': ln1_g, 'ln1_b': ln1_b,
        'router_coarse': router_coarse, 'router_coarse_b': router_coarse_b,
        'router_fine': router_fine, 'router_fine_b': router_fine_b,
        'exp_w_gate': exp_w_gate, 'exp_w_up': exp_w_up, 'exp_w_down': exp_w_down,
        'ln2_g': ln2_g, 'ln2_b': ln2_b,
    }
    wts = _prepare_weights(p)
    return (_encode(x_prompt, p, wts), _encode(x_sample, p, wts))
```

```python
import functools
import math

import jax
import jax.numpy as jnp
from jax import lax
from jax.experimental import pallas as pl
from jax.experimental.pallas import tpu as pltpu

F32 = jnp.float32
BF16 = jnp.bfloat16

D_MODEL = 2048
N_META = 16
S5_WIDTH = 1024
S5_GROUP = 16
S5_GROUPS = 64
S5_STATE = 64
S5_GB = 8
S5_PAIR_MAX_CHUNKS = 160
S5_CHUNK = 16
RW_WIDTH = 1024
RW_HEAD = 64
RW_HEADS = 16
RW_DECAY_LORA = 64
RW_ICLR_LORA = 64
RW_GATE_LORA = 160
RW_GATE_PAD = 256
RW_COLS = 3 * RW_WIDTH + 2 * RW_DECAY_LORA + 2 * RW_ICLR_LORA + RW_GATE_PAD
RW_CHUNK = 64
RW_HEADS_PER_STEP = 2
MOE_GROUPS = 4
EXPERTS_PER_GROUP = 8
N_EXPERTS = 32
D_EXPERT = 512
MOE_ROWS = 512
ROUTER_PAD = 128
DEPTH = 1
ALPHA = (2 * DEPTH) ** 0.25
LN_EPS = 1e-5
GN_EPS = 64e-5
SEQ_TAIL = 64
VMEM_LIMIT = 56 * 1024 * 1024


def _cparams(*sem):
    return pltpu.CompilerParams(dimension_semantics=sem, vmem_limit_bytes=VMEM_LIMIT)


def _row_block(t_pad, cap, mult=8):
    best = mult
    for d in range(mult, cap + 1, mult):
        if t_pad % d == 0:
            best = d
    return best


def _dot(a, b):
    return jnp.dot(a, b, preferred_element_type=F32)


def _dot_nt(a, b):
    return lax.dot_general(a, b, (((1,), (1,)), ((), ())), preferred_element_type=F32)


def _dot_tn(a, b):
    return lax.dot_general(a, b, (((0,), (0,)), ((), ())), preferred_element_type=F32)


def _split(x):
    hi = x.astype(BF16)
    lo = (x - hi.astype(F32)).astype(BF16)
    return hi, lo


def _layernorm(x, g, b):
    mu = jnp.mean(x, axis=-1, keepdims=True)
    xc = x - mu
    var = jnp.mean(xc * xc, axis=-1, keepdims=True)
    return xc * lax.rsqrt(var + LN_EPS) * g + b


def _ln_in_kernel(x_ref, m_ref, g_ref, b_ref, of_ref, ob_ref, *, n_token_blocks):
    j = pl.program_id(1)
    bx, d = x_ref.shape[1], x_ref.shape[2]

    @pl.when(j < n_token_blocks)
    def _():
        y = _layernorm(x_ref[0], g_ref[...], b_ref[...])
        of_ref[0] = y
        ob_ref[0] = y.astype(BF16)

    @pl.when(j == n_token_blocks)
    def _():
        ym = _layernorm(m_ref[...], g_ref[...], b_ref[...])
        y = jnp.concatenate([jnp.zeros((SEQ_TAIL - N_META, d), F32), ym, jnp.zeros((bx - SEQ_TAIL, d), F32)],
                            axis=0)
        of_ref[0] = y
        ob_ref[0] = y.astype(BF16)


def _ln_in(x, meta, g, b):
    bsz, t, d = x.shape
    assert t % SEQ_TAIL == 0
    t_pad = t + SEQ_TAIL
    bx = _row_block(t, 512, SEQ_TAIL)
    nxb = t // bx
    row = pl.BlockSpec((1, bx, d), lambda i, j: (i, j, 0))
    vec = pl.BlockSpec((1, d), lambda i, j: (0, 0))
    return pl.pallas_call(
        functools.partial(_ln_in_kernel, n_token_blocks=nxb),
        grid=(bsz, nxb + 1),
        in_specs=[pl.BlockSpec((1, bx, d), lambda i, j: (i, jnp.minimum(j, nxb - 1), 0)),
                  pl.BlockSpec((N_META, d), lambda i, j: (0, 0)), vec, vec],
        out_specs=[row, row],
        out_shape=[jax.ShapeDtypeStruct((bsz, t_pad, d), F32), jax.ShapeDtypeStruct((bsz, t_pad, d), BF16)],
        compiler_params=_cparams("parallel", "parallel"),
        name="ln_in",
    )(x, meta, g.reshape(1, d), b.reshape(1, d))


def _mm_kernel(x_ref, w_ref, o_ref):
    o_ref[...] = _dot(x_ref[...], w_ref[...]).astype(o_ref.dtype)


def _mm(x, w, bm, bn, out_dtype, name):
    n, k = x.shape
    m = w.shape[1]
    return pl.pallas_call(
        _mm_kernel,
        grid=(m // bn, n // bm),
        in_specs=[pl.BlockSpec((bm, k), lambda j, i: (i, 0)),
                  pl.BlockSpec((k, bn), lambda j, i: (0, j))],
        out_specs=pl.BlockSpec((bm, bn), lambda j, i: (i, j)),
        out_shape=jax.ShapeDtypeStruct((n, m), out_dtype),
        compiler_params=_cparams("parallel", "parallel"),
        name=name,
    )(x, w)


def _mm_slab_kernel(x_ref, w_ref, o_ref):
    res = _dot(x_ref[...], w_ref[...])
    for g in range(o_ref.shape[0]):
        o_ref[g] = res[:, g * 128:(g + 1) * 128]


def _mm_slabs(x, w, bm, name):
    n, k = x.shape
    m = w.shape[1]
    return pl.pallas_call(
        _mm_slab_kernel,
        grid=(n // bm,),
        in_specs=[pl.BlockSpec((bm, k), lambda i: (i, 0)), pl.BlockSpec((k, m), lambda i: (0, 0))],
        out_specs=pl.BlockSpec((m // 128, bm, 128), lambda i: (0, i, 0)),
        out_shape=jax.ShapeDtypeStruct((m // 128, n, 128), F32),
        compiler_params=_cparams("parallel"),
        name=name,
    )(x, w)


def _s5_matrices(b_re, b_im, a_re, a_im, log_dt, c_re, c_im):
    L = S5_CHUNK
    dt = jnp.exp(log_dt)[..., None]
    mag = jnp.exp(a_re * dt)
    abr = mag * jnp.cos(a_im * dt)
    abi = mag * jnp.sin(a_im * dt)
    den = a_re * a_re + a_im * a_im
    nr = abr - 1.0
    cr = (nr * a_re + abi * a_im) / den
    ci = (abi * a_re - nr * a_im) / den
    bbr = cr[..., None] * b_re - ci[..., None] * b_im
    bbi = cr[..., None] * b_im + ci[..., None] * b_re
    tau = jnp.arange(L + 1, dtype=F32)[:, None, None, None]
    pmag = jnp.exp(tau * a_re * dt)
    pr = pmag * jnp.cos(tau * a_im * dt)
    pi = pmag * jnp.sin(tau * a_im * dt)
    wr = pr[..., None] * bbr - pi[..., None] * bbi
    wi = pr[..., None] * bbi + pi[..., None] * bbr
    kern = (jnp.einsum('zgop,tzgpi->tzgoi', c_re, wr)
            - jnp.einsum('zgop,tzgpi->tzgoi', c_im, wi))
    s = jnp.arange(L)[:, None]
    t = jnp.arange(L)[None, :]
    lag = t - s
    kf = jnp.where((lag >= 0)[..., None, None, None], kern[jnp.clip(lag, 0, L), 0], 0.0)
    kb = jnp.where((lag <= 0)[..., None, None, None], kern[jnp.clip(-lag, 0, L), 1], 0.0)
    toep = (kf + kb).transpose(2, 0, 4, 1, 3).reshape(S5_GROUPS, L * S5_GROUP, L * S5_GROUP)
    wf_r, wf_i = wr[::-1][1:, 0], wi[::-1][1:, 0]
    wb_r, wb_i = wr[:L, 1], wi[:L, 1]
    bmat = jnp.concatenate([wf_r, wb_r, wf_i, wb_i], axis=2)
    bmat = bmat.transpose(1, 0, 3, 2).reshape(S5_GROUPS, L * S5_GROUP, 4 * S5_STATE)
    pf_r, pf_i = pr[1:, 0], pi[1:, 0]
    pb_r, pb_i = pr[::-1][:L, 1], pi[::-1][:L, 1]
    c0r, c0i, c1r, c1i = c_re[0], c_im[0], c_re[1], c_im[1]

    def cpow(cre, cim, p_r, p_i):
        re = cre[None] * p_r[:, :, None, :] - cim[None] * p_i[:, :, None, :]
        im = cre[None] * p_i[:, :, None, :] + cim[None] * p_r[:, :, None, :]
        return re, -im

    f_re, f_im = cpow(c0r, c0i, pf_r, pf_i)
    g_re, g_im = cpow(c1r, c1i, pb_r, pb_i)
    cmat = jnp.concatenate([f_re, g_re, f_im, g_im], axis=3)
    cmat = cmat.transpose(1, 3, 0, 2).reshape(S5_GROUPS, 4 * S5_STATE, L * S5_GROUP)
    lam_re = jnp.concatenate([pr[L, 0], pr[L, 1]], axis=-1)[:, None, :]
    lam_im = jnp.concatenate([pi[L, 0], pi[L, 1]], axis=-1)[:, None, :]
    return bmat, toep, cmat, lam_re, lam_im


def _s5_expand_kernel(src_ref, e_ref, o_ref, *, pieces):
    for j in range(src_ref.shape[0]):
        x = _dot(src_ref[j].astype(BF16), e_ref[j]).astype(BF16)
        for src, rows, dst, step in pieces:
            o_ref[0, dst + j * step:dst + j * step + rows, :] = x[src:src + rows, :]


def _s5_expand(per_group, expand, pieces):
    groups, rows, cols = per_group.shape
    gb, _, wide = expand.shape
    return pl.pallas_call(
        functools.partial(_s5_expand_kernel, pieces=pieces),
        grid=(groups // gb,),
        in_specs=[pl.BlockSpec((gb, rows, cols), lambda g: (g, 0, 0)),
                  pl.BlockSpec((gb, cols, wide), lambda g: (0, 0, 0))],
        out_specs=pl.BlockSpec((1, gb * rows, wide), lambda g: (g, 0, 0)),
        out_shape=jax.ShapeDtypeStruct((groups // gb, gb * rows, wide), BF16),
        compiler_params=_cparams("parallel"),
        name="s5_expand",
    )(per_group, expand)


def _s5_block_operators(bmat, toep, cmat, lam_re, lam_im):
    nb, gb, L, c, p = S5_GROUPS // S5_GB, S5_GB, S5_CHUNK, S5_GROUP, S5_STATE
    quarters = (0, 2, 1, 3)
    wide = L * gb * c
    r = jnp.arange(L * c)[None, :, None]
    col = jnp.arange(wide)[None, None, :]
    j = jnp.arange(gb)[:, None, None]
    e_tok = ((r // c == col // (gb * c)) & ((col // c) % gb == j) & (r % c == col % c)).astype(BF16)
    k_of_col = jnp.array(quarters)[col // (gb * p)]
    e_state = ((r // p == k_of_col) & ((col // p) % gb == j) & (r % p == col % p)).astype(BF16)
    tok_rows = tuple((s * c, c, s * gb * c, c) for s in range(L))
    state_rows = tuple((quarters[k] * p, p, k * gb * p, p) for k in range(4))
    wb = _s5_expand(bmat, e_state, tok_rows)
    wt = _s5_expand(toep, e_tok, tok_rows)
    wc = _s5_expand(cmat, e_tok, state_rows)
    ar_f = lam_re[:, 0, :p].reshape(nb, gb * p)
    ar_b = lam_re[:, 0, p:].reshape(nb, gb * p)
    ai_f = lam_im[:, 0, :p].reshape(nb, gb * p)
    ai_b = lam_im[:, 0, p:].reshape(nb, gb * p)
    a1 = jnp.stack([ar_f, ar_f, ar_b, ar_b] * 2, axis=1)
    a2 = jnp.stack([-ai_f, ai_f, -ai_b, ai_b] * 2, axis=1)
    return wb, wt, wc, a1, a2


def _s5_kernel(u_ref, wb_ref, wt_ref, wc_ref, a1_ref, a2_ref, y_ref, s_scr, xf_scr, xb_scr, *, n_chunks):
    L = S5_CHUNK
    C = n_chunks
    nseq = u_ref.shape[1]
    cp = -(-C // 8) * 8
    sw = S5_GB * S5_STATE
    tail = SEQ_TAIL // L

    def rows_of(parts):
        if cp > C:
            pad = jnp.zeros((cp - C, parts[0].shape[1]), F32)
            parts = [x for part in parts for x in (part, pad)]
        return jnp.concatenate(parts, axis=0).astype(BF16)

    u8 = rows_of([jnp.concatenate([u_ref[0, q, pl.ds(s, C, stride=L), :] for s in range(L)], axis=1)
                  for q in range(nseq)])
    s_all = _dot(u8, wb_ref[0])
    for q in range(nseq):
        for k in range(4):
            s_scr[:, 4 * q + k, :] = s_all[q * cp:q * cp + C, k * sw:(k + 1) * sw]
    if nseq == 1:
        s_scr[:, 4:8, :] = jnp.zeros((C, 4, sw), F32)
    a1 = a1_ref[0]
    a2 = a2_ref[0]
    row = lax.broadcasted_iota(jnp.int32, (8, sw), 0)
    even = (row % 2) == 0
    is_fwd = (row % 4) < 2

    def step(c, x):
        cf = (c + C - tail) % C
        cb = (2 * C - 1 - c - tail) % C
        xf_scr[cf] = x
        xb_scr[cb] = x
        s = jnp.where(is_fwd, s_scr[cf], s_scr[cb])
        swapped = jnp.where(even, pltpu.roll(x, 7, 0), pltpu.roll(x, 1, 0))
        return a1 * x + a2 * swapped + s

    lax.fori_loop(0, C, step, jnp.zeros((8, sw), F32))
    x_in = rows_of([jnp.concatenate([xf_scr[:, 4 * q, :], xf_scr[:, 4 * q + 1, :],
                                     xb_scr[:, 4 * q + 2, :], xb_scr[:, 4 * q + 3, :]], axis=1)
                    for q in range(nseq)])
    y8 = _dot(u8, wt_ref[0]) + _dot(x_in, wc_ref[0])
    for q in range(nseq):
        for t in range(L):
            y_ref[0, q, pl.ds(t, C, stride=L), :] = y8[q * cp:q * cp + C, t * 128:(t + 1) * 128]


def _s5_ssm(u3, ops):
    wb, wt, wc, a1, a2 = ops
    _, bsz, t_pad, _ = u3.shape
    n_chunks = t_pad // S5_CHUNK
    nb = S5_GROUPS // S5_GB
    lanes = S5_GB * S5_GROUP
    sw = S5_GB * S5_STATE
    nseq = 2 if (bsz % 2 == 0 and n_chunks <= S5_PAIR_MAX_CHUNKS) else 1
    blk = pl.BlockSpec((1, nseq, t_pad, lanes), lambda g, b: (g, b, 0, 0))
    mat = pl.BlockSpec((1,) + wb.shape[1:], lambda g, b: (g, 0, 0), pipeline_mode=pl.Buffered(1))
    vec = pl.BlockSpec((1, 8, sw), lambda g, b: (g, 0, 0))
    return pl.pallas_call(
        functools.partial(_s5_kernel, n_chunks=n_chunks),
        grid=(nb, bsz // nseq),
        in_specs=[blk, mat, mat, mat, vec, vec],
        out_specs=blk,
        out_shape=jax.ShapeDtypeStruct(u3.shape, F32),
        scratch_shapes=[pltpu.VMEM((n_chunks, 8, sw), F32)] * 3,
        compiler_params=_cparams("arbitrary", "arbitrary"),
        name="s5_ssm",
    )(u3, wb, wt, wc, a1, a2)


def _s5_post_kernel(y_ref, u_ref, d_ref, w_ref, b_ref, o_ref):
    slabs = range(y_ref.shape[0])
    y = (jnp.concatenate([y_ref[g] for g in slabs], axis=1)
         + jnp.concatenate([u_ref[g] for g in slabs], axis=1) * d_ref[...])
    act = y * (0.5 * (1.0 + jnp.tanh(math.sqrt(2.0 / math.pi) * (y + 0.044715 * (y * y * y)))))
    z = _dot(act.astype(BF16), w_ref[...]) + b_ref[...]
    o_ref[...] = (act * jax.nn.sigmoid(z)).astype(o_ref.dtype)


def _s5_post(y, u, d_skip, glu_w, glu_b, bm):
    slabs, n, lanes = y.shape
    row = pl.BlockSpec((bm, S5_WIDTH), lambda i: (i, 0))
    slab = pl.BlockSpec((slabs, bm, lanes), lambda i: (0, i, 0))
    vec = pl.BlockSpec((1, S5_WIDTH), lambda i: (0, 0))
    return pl.pallas_call(
        _s5_post_kernel,
        grid=(n // bm,),
        in_specs=[slab, slab, vec, pl.BlockSpec((S5_WIDTH, S5_WIDTH), lambda i: (0, 0)), vec],
        out_specs=row,
        out_shape=jax.ShapeDtypeStruct((n, S5_WIDTH), BF16),
        compiler_params=_cparams("parallel"),
        name="s5_post",
    )(y, u, d_skip.reshape(1, -1), glu_w, glu_b.reshape(1, -1))


def _head_sum(x, e_ref, et_ref):
    hi, lo = _split(x)
    s = _dot(hi, e_ref[...]) + _dot(lo, e_ref[...])
    shi, slo = _split(s)
    return _dot(shi, et_ref[...]) + _dot(slo, et_ref[...])


def _rwkv_prep_kernel(cur_ref, prev_ref, next_ref, mu_ref, w2_ref, a2_ref, g2_ref, w0_ref, a0_ref,
                      kk_ref, ka_ref, rk_ref, e_ref, et_ref,
                      r_o, kk_o, v_o, g_o, bv_o, lw_o, kd_o, bb_o, *, t, t_pad, bm):
    j = pl.program_id(1)
    p = cur_ref[0]
    row = lax.broadcasted_iota(jnp.int32, (bm, 1), 0)
    prev_row = prev_ref[0, 7:8, :]
    next_row = next_ref[0, 0:1, :]
    prev = jnp.where(row == 0, prev_row, pltpu.roll(p, 1, 0))
    nxt = jnp.where(row == bm - 1, next_row, pltpu.roll(p, bm - 1, 0))
    xs = p + mu_ref[0:1, :] * (prev - p) + mu_ref[1:2, :] * (nxt - p)
    w = RW_WIDTH
    r = xs[:, 0:w]
    k = xs[:, w:2 * w]
    v = xs[:, 2 * w:3 * w]
    lw = xs[:, 3 * w:3 * w + 128]
    la = xs[:, 3 * w + 128:3 * w + 256]
    lg = xs[:, 3 * w + 256:]
    w_log = _dot(jnp.tanh(lw).astype(BF16), w2_ref[...])
    a_lin = _dot(la.astype(BF16), a2_ref[...])
    g = _dot(jax.nn.sigmoid(lg).astype(BF16), g2_ref[...])
    kk = k * kk_ref[...]
    n2 = _head_sum(kk * kk, e_ref, et_ref)
    kk = kk / jnp.maximum(jnp.sqrt(n2), 1e-12)
    pos = j * bm + row
    valid = (pos < t) | (pos >= t_pad - N_META)
    v = jnp.where(valid, v, 0.0)
    kd_sum = jnp.zeros_like(k)
    for z in range(2):
        wl = w_log[:, z * w:(z + 1) * w] + w0_ref[z:z + 1, :]
        lw_o[z, 0] = -math.exp(-0.5) * jax.nn.sigmoid(wl)
        a = jax.nn.sigmoid(a_lin[:, z * w:(z + 1) * w] + a0_ref[z:z + 1, :])
        kd = k * (1.0 + (a - 1.0) * ka_ref[...])
        kd_o[z, 0] = kd.astype(BF16)
        bb_o[z, 0] = (kk * a).astype(BF16)
        kd_sum = kd_sum + kd
    bonus = _head_sum(r * kd_sum * rk_ref[...], e_ref, et_ref)
    r_o[0] = r.astype(BF16)
    kk_o[0] = kk.astype(BF16)
    v_o[0] = v.astype(BF16)
    g_o[0] = g.astype(BF16)
    bv_o[0] = (bonus * v).astype(BF16)


def _rwkv_prep(rw3, wts, t):
    bsz, t_pad, _ = rw3.shape
    bm = _row_block(t_pad, 320, 64)
    nb8 = bm // 8
    n8 = t_pad // 8
    w = RW_WIDTH
    cur = pl.BlockSpec((1, bm, RW_COLS), lambda b, j: (b, j, 0))
    prev = pl.BlockSpec((1, 8, RW_COLS), lambda b, j: (b, (j * nb8 + n8 - 1) % n8, 0))
    nxt = pl.BlockSpec((1, 8, RW_COLS), lambda b, j: (b, ((j + 1) * nb8) % n8, 0))

    def full(a):
        return pl.BlockSpec(a.shape, lambda b, j: (0,) * a.ndim)

    shared = pl.BlockSpec((1, bm, w), lambda b, j: (b, j, 0))
    per_dir = pl.BlockSpec((2, 1, bm, w), lambda b, j: (0, b, j, 0))
    consts = [wts['mu'], wts['w2'], wts['a2'], wts['g2'], wts['w0'], wts['a0'],
              wts['k_k'], wts['k_a'], wts['r_k'], wts['head_e'], wts['head_et']]
    sds = jax.ShapeDtypeStruct
    return pl.pallas_call(
        functools.partial(_rwkv_prep_kernel, t=t, t_pad=t_pad, bm=bm),
        grid=(bsz, t_pad // bm),
        in_specs=[cur, prev, nxt] + [full(a) for a in consts],
        out_specs=[shared] * 5 + [per_dir] * 3,
        out_shape=[sds((bsz, t_pad, w), BF16)] * 5
        + [sds((2, bsz, t_pad, w), F32), sds((2, bsz, t_pad, w), BF16), sds((2, bsz, t_pad, w), BF16)],
        compiler_params=_cparams("parallel", "parallel"),
        name="rwkv_prep",
    )(rw3, rw3, rw3, *consts)


def _rwkv_chunk(fwd, r_ref, kk_ref, v_ref, lw_ref, kd_ref, bb_ref, y_ref, st_ref, z):
    L = RW_CHUNK
    hd = RW_HEAD
    gw = RW_HEADS_PER_STEP * hd
    n_groups = RW_WIDTH // gw
    row = lax.broadcasted_iota(jnp.int32, (L, L), 0)
    col = lax.broadcasted_iota(jnp.int32, (L, L), 1)
    tri = jnp.where((col <= row) if fwd else (col >= row), 1.0, 0.0).astype(BF16)
    grow = lax.broadcasted_iota(jnp.int32, (L, gw), 0)
    gcol = lax.broadcasted_iota(jnp.int32, (L, gw), 1) % L
    incl = (gcol <= grow) if fwd else (gcol >= grow)
    strict = (gcol < grow) if fwd else (gcol > grow)
    bd_mask = jnp.where(lax.broadcasted_iota(jnp.int32, (gw, gw), 0) // hd
                        == lax.broadcasted_iota(jnp.int32, (gw, gw), 1) // hd, 1.0, 0.0).astype(BF16)

    def bd(x):
        return jnp.concatenate([x] * RW_HEADS_PER_STEP, axis=0) * bd_mask

    def stack(x):
        return jnp.concatenate([x[:, h * hd:(h + 1) * hd] for h in range(RW_HEADS_PER_STEP)], axis=0)

    lw = lw_ref[0, 0]
    lw_hi, lw_lo = _split(lw)
    c = _dot(tri, lw_hi) + _dot(tri, lw_lo)
    e = c - lw
    c_tot = c[L - 1:L, :] if fwd else c[0:1, :]
    r = r_ref[0].astype(F32)
    kk = kk_ref[0].astype(F32)
    kd = kd_ref[0, 0].astype(F32)
    bb = bb_ref[0, 0].astype(F32)
    v = v_ref[0]
    q1 = (kk * jnp.exp(e)).astype(BF16)
    q2 = (r * jnp.exp(c)).astype(BF16)
    inv = jnp.exp(-c)
    k1 = (kd * inv).astype(BF16)
    k2 = (bb * inv).astype(BF16)
    rest = jnp.exp(c_tot - c)
    k1p = (kd * rest).astype(BF16)
    k2p = (bb * rest).astype(BF16)
    dec_tot = jnp.exp(c_tot)
    def group_chain(g):
        sl = slice(g * gw, (g + 1) * gw)
        s0 = st_ref[z, g]
        s0_hi, s0_lo = _split(s0)
        lhs = jnp.concatenate([q1[:, sl], q2[:, sl]], axis=0)
        rhs = jnp.concatenate([bd(k1[:, sl]), bd(k2[:, sl]), bd(s0_hi), bd(s0_lo)], axis=0)
        m1 = _dot_nt(lhs, rhs)
        yield
        a_kd = jnp.where(strict, m1[:L, 0:gw], 0.0)
        a_b = jnp.where(strict, m1[:L, gw:2 * gw], 0.0)
        q1s = m1[:L, 2 * gw:3 * gw] + m1[:L, 3 * gw:]
        b_kd = jnp.where(incl, m1[L:, 0:gw], 0.0)
        b_b = jnp.where(incl, m1[L:, gw:2 * gw], 0.0)
        q2s = m1[L:, 2 * gw:3 * gw] + m1[L:, 3 * gw:]
        vg = v[:, sl]
        v_bd = bd(vg)
        x = q1s + _dot(a_kd.astype(BF16), v_bd)
        yield
        m = -a_b
        levels = L.bit_length() - 1
        for lvl in range(levels):
            mb = m.astype(BF16)
            if lvl < levels - 1:
                rr = _dot(mb, jnp.concatenate([bd(x.astype(BF16)), bd(mb)], axis=1))
                x = x + rr[:, :gw]
                m = rr[:, gw:]
            else:
                x = x + _dot(mb, bd(x.astype(BF16)))
            yield
        ub = x.astype(BF16)
        y = q2s + _dot(jnp.concatenate([b_kd, -b_b], axis=1).astype(BF16),
                       jnp.concatenate([v_bd, bd(ub)], axis=0))
        y_ref[0, :, sl] = y
        yield
        st_ref[z, g] = s0 * dec_tot[:, sl] + _dot_tn(
            jnp.concatenate([stack(vg), stack(ub)], axis=0),
            jnp.concatenate([bd(k1p[:, sl]), -bd(k2p[:, sl])], axis=0))

    return [group_chain(g) for g in range(n_groups)]


def _rwkv_scan_kernel(rf, kkf, vf, lwf, kdf, bbf, rb, kkb, vb, lwb, kdb, bbb, yf_ref, yb_ref, st_ref):
    @pl.when(pl.program_id(1) == 0)
    def _():
        st_ref[...] = jnp.zeros_like(st_ref)

    chains = (_rwkv_chunk(True, rf, kkf, vf, lwf, kdf, bbf, yf_ref, st_ref, 0)
              + _rwkv_chunk(False, rb, kkb, vb, lwb, kdb, bbb, yb_ref, st_ref, 1))
    while chains:
        alive = []
        for chain in chains:
            try:
                next(chain)
                alive.append(chain)
            except StopIteration:
                pass
        chains = alive


def _rwkv_scan(r, kk, v, lw, kd, bb):
    bsz, t_pad, w = r.shape
    L = RW_CHUNK
    nc = t_pad // L
    gw = RW_HEADS_PER_STEP * RW_HEAD
    grid = (bsz, nc)

    def block(fwd, j):
        logical = j if fwd else nc - 1 - j
        return (logical + nc - 1) % nc

    def shared(fwd):
        return pl.BlockSpec((1, L, w), lambda b, j: (b, block(fwd, j), 0))

    def per_dir(fwd):
        return pl.BlockSpec((1, 1, L, w), lambda b, j: (0 if fwd else 1, b, block(fwd, j), 0))

    in_specs = []
    for fwd in (True, False):
        in_specs += [shared(fwd), shared(fwd), shared(fwd), per_dir(fwd), per_dir(fwd), per_dir(fwd)]
    return pl.pallas_call(
        _rwkv_scan_kernel,
        grid=grid,
        in_specs=in_specs,
        out_specs=[shared(True), shared(False)],
        out_shape=[jax.ShapeDtypeStruct((bsz, t_pad, w), F32)] * 2,
        scratch_shapes=[pltpu.VMEM((2, w // gw, RW_HEAD, gw), F32)],
        compiler_params=_cparams("parallel", "arbitrary"),
        name="rwkv_scan",
    )(r, kk, v, lw, kd, bb, r, kk, v, lw, kd, bb)


def _rwkv_post_kernel(yf_ref, yb_ref, bv_ref, g_ref, lg_ref, lb_ref, e_ref, et_ref, o_ref):
    y = yf_ref[...] + yb_ref[...]
    mean = _head_sum(y, e_ref, et_ref) * (1.0 / RW_HEAD)
    yc = y - mean
    var = _head_sum(yc * yc, e_ref, et_ref) * (1.0 / RW_HEAD)
    y = yc * lax.rsqrt(var + GN_EPS) * lg_ref[...] + lb_ref[...]
    o_ref[...] = ((y + bv_ref[...].astype(F32)) * g_ref[...].astype(F32)).astype(o_ref.dtype)


def _rwkv_post(yf, yb, bv, g, wts, bm):
    n, w = yf.shape
    row = pl.BlockSpec((bm, w), lambda i: (i, 0))
    vec = pl.BlockSpec((1, w), lambda i: (0, 0))
    e, et = wts['head_e'], wts['head_et']
    return pl.pallas_call(
        _rwkv_post_kernel,
        grid=(n // bm,),
        in_specs=[row, row, row, row, vec, vec,
                  pl.BlockSpec(e.shape, lambda i: (0, 0)), pl.BlockSpec(et.shape, lambda i: (0, 0))],
        out_specs=row,
        out_shape=jax.ShapeDtypeStruct((n, w), BF16),
        compiler_params=_cparams("parallel"),
        name="rwkv_post",
    )(yf, yb, bv, g, wts['lnx_g'], wts['lnx_b'], e, et)


def _merge_kernel(h_ref, s5_ref, rw_ref, wg0_ref, wg1_ref, gb_ref, p0_ref, p1_ref, o_ref):
    h = h_ref[...]
    g0 = jax.nn.sigmoid(_dot(h, wg0_ref[...]) + gb_ref[0:1, :])
    g1 = jax.nn.sigmoid(_dot(h, wg1_ref[...]) + gb_ref[1:2, :])
    merged = g0 * _dot(s5_ref[...], p0_ref[...]) + g1 * _dot(rw_ref[...], p1_ref[...])
    o_ref[...] = merged.astype(o_ref.dtype)


def _merge(h0b, s5_out, rw_out, wts, bm):
    n, d = h0b.shape
    bn = 1024
    nj = d // bn
    return pl.pallas_call(
        _merge_kernel,
        grid=(nj, n // bm),
        in_specs=[pl.BlockSpec((bm, d), lambda j, i: (i, 0)),
                  pl.BlockSpec((bm, S5_WIDTH), lambda j, i: (i, 0)),
                  pl.BlockSpec((bm, RW_WIDTH), lambda j, i: (i, 0)),
                  pl.BlockSpec((d, bn), lambda j, i: (0, j)),
                  pl.BlockSpec((d, bn), lambda j, i: (0, nj + j)),
                  pl.BlockSpec((2, bn), lambda j, i: (0, j)),
                  pl.BlockSpec((S5_WIDTH, bn), lambda j, i: (0, j)),
                  pl.BlockSpec((RW_WIDTH, bn), lambda j, i: (0, j))],
        out_specs=pl.BlockSpec((bm, bn), lambda j, i: (i, j)),
        out_shape=jax.ShapeDtypeStruct((n, d), BF16),
        compiler_params=_cparams("parallel", "parallel"),
        name="merge",
    )(h0b, s5_out, rw_out, wts['w_gate'], wts['w_gate'], wts['gate_b'], wts['proj_s5'], wts['proj_rwkv'])


def _out_kernel(m_ref, h_ref, w_ref, g_ref, b_ref, rh_ref, rl_ref, o_ref, lg_ref):
    bm = m_ref.shape[0]
    halves = [slice(0, bm // 2), slice(bm // 2, bm)]
    xs = [ALPHA * h_ref[rows, :] + _dot(m_ref[rows, :], w_ref[...]) for rows in halves]
    for rows, x in zip(halves, xs):
        h1 = _layernorm(x, g_ref[...], b_ref[...])
        o_ref[rows, :] = h1
        hi, lo = _split(h1)
        lg_ref[rows, :] = _dot(hi, rh_ref[...]) + _dot(lo, rh_ref[...]) + _dot(hi, rl_ref[...])


def _out_proj(merged, h0, wts, bm):
    n, d = h0.shape
    row = pl.BlockSpec((bm, d), lambda i: (i, 0))
    vec = pl.BlockSpec((1, d), lambda i: (0, 0))
    rt = pl.BlockSpec((d, ROUTER_PAD), lambda i: (0, 0))
    return pl.pallas_call(
        _out_kernel,
        grid=(n // bm,),
        in_specs=[row, row, pl.BlockSpec((d, d), lambda i: (0, 0)), vec, vec, rt, rt],
        out_specs=[row, pl.BlockSpec((bm, ROUTER_PAD), lambda i: (i, 0))],
        out_shape=[jax.ShapeDtypeStruct((n, d), F32), jax.ShapeDtypeStruct((n, ROUTER_PAD), F32)],
        compiler_params=_cparams("parallel"),
        name="out_proj",
    )(merged, h0, wts['w_out'], wts['ln1_g'], wts['ln1_b'], wts['router_hi'], wts['router_lo'])


def _route(logits, wts, valid):
    i32 = jnp.int32
    lc = logits[:, :MOE_GROUPS] + wts['router_coarse_b']
    grp = jnp.argmax(lc, axis=-1).astype(i32)
    gate_c = jnp.max(jax.nn.softmax(lc, axis=-1), axis=-1)
    lf = (logits[:, MOE_GROUPS:MOE_GROUPS + N_EXPERTS] + wts['router_fine_b'])
    lf = lf.reshape(-1, MOE_GROUPS, EXPERTS_PER_GROUP)
    sel = grp[:, None, None] == jnp.arange(MOE_GROUPS, dtype=i32)[None, :, None]
    lf = jnp.sum(jnp.where(sel, lf, 0.0), axis=1)
    lane = jnp.arange(EXPERTS_PER_GROUP, dtype=i32)[None, :]
    i1 = jnp.argmax(lf, axis=-1).astype(i32)
    v1 = jnp.max(lf, axis=-1)
    rest = jnp.where(lane == i1[:, None], -jnp.inf, lf)
    i2 = jnp.argmax(rest, axis=-1).astype(i32)
    v2 = jnp.max(rest, axis=-1)
    top_v = jnp.stack([v1, v2], axis=-1)
    top_i = jnp.stack([i1, i2], axis=-1)
    w = gate_c[:, None] * jax.nn.softmax(top_v, axis=-1)
    expert = grp[:, None] * EXPERTS_PER_GROUP + top_i
    expert = jnp.where(valid[:, None], expert, N_EXPERTS)
    w = jnp.where(valid[:, None], w, 0.0)
    n_tok = logits.shape[0]
    n_asg = 2 * n_tok
    e_flat = expert.reshape(-1)
    order = jnp.argsort(e_flat).astype(i32)
    inv = jnp.argsort(order).astype(i32)
    bounds = jnp.sum(e_flat[None, :] < jnp.arange(N_EXPERTS + 1, dtype=i32)[:, None], axis=1, dtype=i32)
    start = bounds[:N_EXPERTS]
    counts = bounds[1:] - start
    padded = (counts + MOE_ROWS - 1) // MOE_ROWS * MOE_ROWS
    ex = jnp.arange(N_EXPERTS, dtype=i32)
    pend = jnp.sum(jnp.where(ex[None, :] <= ex[:, None], padded[None, :], 0), axis=1)
    pstart = pend - padded
    n_blocks = -(-n_asg // MOE_ROWS) + N_EXPERTS
    n_rows = n_blocks * MOE_ROWS
    n_used = pend[-1] // MOE_ROWS
    blk = jnp.minimum(jnp.arange(n_blocks, dtype=i32), n_used - 1)
    blk_exp = jnp.sum(pend[None, :] <= (blk * MOE_ROWS)[:, None], axis=1, dtype=i32)
    blk_exp = jnp.minimum(blk_exp, N_EXPERTS - 1)
    experts = jnp.arange(N_EXPERTS, dtype=i32)

    def lookup(table, idx):
        return jnp.sum(jnp.where(idx[..., None] == experts, table, 0), axis=-1)

    pos = jnp.where(e_flat < N_EXPERTS, lookup(pstart - start, e_flat) + inv, 0)
    d = jnp.arange(n_rows, dtype=i32).reshape(n_blocks, MOE_ROWS)
    k = d - lookup(pstart, blk_exp)[:, None]
    src = jnp.clip(k + lookup(start, blk_exp)[:, None], 0, n_asg - 1)
    row_tok = jnp.where((k < lookup(counts, blk_exp)[:, None]) & (d < pend[-1]), order[src] // 2, 0)
    return row_tok, pos, w, blk_exp, n_used.reshape(1).astype(i32), n_blocks


def _row_gather_start(src_hbm, idx_ref, n_rows, dst, sem, stride=1, offset=0):
    for r in range(n_rows):
        row = idx_ref[0, 0, stride * r + offset]
        pltpu.make_async_copy(src_hbm.at[pl.ds(row, 1)], dst.at[pl.ds(r, 1)], sem).start(priority=r % 2)


def _row_gather_wait(src_hbm, n_rows, dst, sem):
    def wait(r, carry):
        pltpu.make_async_copy(src_hbm.at[pl.ds(0, 1)], dst.at[pl.ds(r, 1)], sem).wait()
        return carry

    lax.fori_loop(0, n_rows, wait, 0, unroll=8)


def _expert_kernel(nused_ref, bexp_ref, idx_ref, nidx_ref, x_hbm, wg_ref, wu_ref, wd_ref, o_ref, buf, sem):
    i = pl.program_id(0)
    n_used = nused_ref[0]
    slot = i % 2

    @pl.when((i == 0) & (n_used > 0))
    def _():
        _row_gather_start(x_hbm, idx_ref, MOE_ROWS, buf.at[0], sem.at[0])

    @pl.when(i + 1 < n_used)
    def _():
        _row_gather_start(x_hbm, nidx_ref, MOE_ROWS, buf.at[1 - slot], sem.at[1 - slot])

    @pl.when(i < n_used)
    def _():
        _row_gather_wait(x_hbm, MOE_ROWS, buf.at[slot], sem.at[slot])
        x = buf[slot].astype(BF16)
        hb = jax.nn.silu(_dot(x, wg_ref[0])) * _dot(x, wu_ref[0])
        o_ref[...] = _dot(hb.astype(BF16), wd_ref[0])

    @pl.when(i >= n_used)
    def _():
        o_ref[...] = jnp.zeros_like(o_ref)


def _moe_experts(h1, row_tok, blk_exp, n_used, n_blocks, wts):
    d = h1.shape[1]
    idx = row_tok.reshape(n_blocks, 1, MOE_ROWS)
    last = n_blocks - 1
    smem = pltpu.SMEM
    return pl.pallas_call(
        _expert_kernel,
        grid_spec=pltpu.PrefetchScalarGridSpec(
            num_scalar_prefetch=2,
            grid=(n_blocks,),
            in_specs=[pl.BlockSpec((1, 1, MOE_ROWS), lambda i, nu, be: (i, 0, 0), memory_space=smem),
                      pl.BlockSpec((1, 1, MOE_ROWS), lambda i, nu, be: (jnp.minimum(i + 1, last), 0, 0),
                                   memory_space=smem),
                      pl.BlockSpec(memory_space=pl.ANY),
                      pl.BlockSpec((1, d, D_EXPERT), lambda i, nu, be: (be[i], 0, 0)),
                      pl.BlockSpec((1, d, D_EXPERT), lambda i, nu, be: (be[i], 0, 0)),
                      pl.BlockSpec((1, D_EXPERT, d), lambda i, nu, be: (be[i], 0, 0))],
            out_specs=pl.BlockSpec((MOE_ROWS, d), lambda i, nu, be: (i, 0)),
            scratch_shapes=[pltpu.VMEM((2, MOE_ROWS, d), F32), pltpu.SemaphoreType.DMA((2,))],
        ),
        out_shape=jax.ShapeDtypeStruct((n_blocks * MOE_ROWS, d), F32),
        compiler_params=_cparams("arbitrary"),
        name="moe_experts",
    )(n_used, blk_exp, idx, idx, h1, wts['exp_w_gate'], wts['exp_w_up'], wts['exp_w_down'])


def _combine_kernel(pos_ref, npos_ref, eo_hbm, h_ref, w_ref, g_ref, b_ref, o_ref, buf, sem, *, n_steps, bm):
    i = pl.program_id(0)
    slot = i % 2

    def start(p_ref, s):
        for k in range(2):
            _row_gather_start(eo_hbm, p_ref, bm, buf.at[s, k], sem.at[s], stride=2, offset=k)

    @pl.when(i == 0)
    def _():
        start(pos_ref, 0)

    @pl.when(i + 1 < n_steps)
    def _():
        start(npos_ref, 1 - slot)

    for k in range(2):
        _row_gather_wait(eo_hbm, bm, buf.at[slot, k], sem.at[slot])
    w = w_ref[0]
    moe = w[:, 0:1] * buf[slot, 0] + w[:, 1:2] * buf[slot, 1]
    o_ref[0] = _layernorm(ALPHA * h_ref[0] + moe, g_ref[...], b_ref[...])


def _moe_combine(eo, pos, w, h1, wts, bsz, t, t_pad):
    n, d = h1.shape
    bm = _row_block(t, 256)
    per_seq = t // bm
    n_steps = bsz * per_seq
    last = n_steps - 1
    vec = pl.BlockSpec((1, d), lambda i: (0, 0))
    smem = pltpu.SMEM
    pos3 = pos.reshape(bsz, t_pad, 2)[:, :t].reshape(n_steps, 1, 2 * bm)

    def rows(i):
        return (i // per_seq, i % per_seq, 0)

    return pl.pallas_call(
        functools.partial(_combine_kernel, n_steps=n_steps, bm=bm),
        grid=(n_steps,),
        in_specs=[pl.BlockSpec((1, 1, 2 * bm), lambda i: (i, 0, 0), memory_space=smem),
                  pl.BlockSpec((1, 1, 2 * bm), lambda i: (jnp.minimum(i + 1, last), 0, 0), memory_space=smem),
                  pl.BlockSpec(memory_space=pl.ANY),
                  pl.BlockSpec((1, bm, d), rows),
                  pl.BlockSpec((1, bm, 2), rows), vec, vec],
        out_specs=pl.BlockSpec((1, bm, d), rows),
        out_shape=jax.ShapeDtypeStruct((bsz, t, d), F32),
        scratch_shapes=[pltpu.VMEM((2, 2, bm) + eo.shape[1:], F32), pltpu.SemaphoreType.DMA((2,))],
        compiler_params=_cparams("arbitrary"),
        name="moe_combine",
    )(pos3, pos3, eo, h1.reshape(bsz, t_pad, d), w.reshape(bsz, t_pad, 2), wts['ln2_g'], wts['ln2_b'])


def _prepare_weights(p):
    l = 0
    w_in = p['w_in'][l]
    c0 = S5_WIDTH
    c1 = c0 + 3 * RW_WIDTH + 2 * RW_DECAY_LORA + 2 * RW_ICLR_LORA + RW_GATE_LORA
    gpad = RW_GATE_PAD - RW_GATE_LORA
    wts = {}
    wts['w_u'] = w_in[:, :c0].astype(BF16)
    wts['w_rw'] = jnp.pad(w_in[:, c0:c1], ((0, 0), (0, gpad))).astype(BF16)
    wts['w_gate'] = w_in[:, c1:].astype(BF16)
    wts['mu'] = jnp.pad(p['shift_mu'][l], ((0, 0), (0, gpad)))
    z = jnp.zeros((RW_DECAY_LORA, RW_WIDTH), F32)
    wts['w2'] = jnp.block([[p['rw_w2'][l, 0], z], [z, p['rw_w2'][l, 1]]]).astype(BF16)
    wts['a2'] = jnp.block([[p['rw_a2'][l, 0], z], [z, p['rw_a2'][l, 1]]]).astype(BF16)
    wts['g2'] = jnp.pad(p['rw_g2'][l], ((0, gpad), (0, 0))).astype(BF16)
    wts['w0'] = p['rw_w0'][l]
    wts['a0'] = p['rw_a0'][l]
    wts['k_k'] = p['rw_k_k'][l].reshape(1, -1)
    wts['k_a'] = p['rw_k_a'][l].reshape(1, -1)
    wts['r_k'] = p['rw_r_k'][l].reshape(1, -1)
    wts['lnx_g'] = p['rw_lnx_g'][l].reshape(1, -1)
    wts['lnx_b'] = p['rw_lnx_b'][l].reshape(1, -1)
    head = jnp.arange(RW_WIDTH) // RW_HEAD
    e = (head[:, None] == jnp.arange(RW_HEADS)[None, :]).astype(BF16)
    wts['head_e'] = e
    wts['head_et'] = e.T
    wts['s5'] = _s5_block_operators(*_s5_matrices(
        p['s5_B_re'][l], p['s5_B_im'][l], p['s5_A_re'][l], p['s5_A_im'][l],
        p['s5_log_dt'][l], p['s5_C_re'][l], p['s5_C_im'][l]))
    wts['s5_D'] = p['s5_D'][l]
    wts['glu_w'] = p['s5_glu_w'][l].astype(BF16)
    wts['glu_b'] = p['s5_glu_b'][l]
    wts['proj_s5'] = p['proj_s5'][l].astype(BF16)
    wts['proj_rwkv'] = p['proj_rwkv'][l].astype(BF16)
    wts['gate_b'] = p['gate_b'][l]
    wts['w_out'] = p['w_out'][l].astype(BF16)
    wts['ln1_g'] = p['ln1_g'][l].reshape(1, -1)
    wts['ln1_b'] = p['ln1_b'][l].reshape(1, -1)
    router = jnp.concatenate([p['router_coarse'][l], p['router_fine'][l]], axis=1)
    router = jnp.pad(router, ((0, 0), (0, ROUTER_PAD - router.shape[1])))
    wts['router_hi'], wts['router_lo'] = _split(router)
    wts['router_coarse_b'] = p['router_coarse_b'][l]
    wts['router_fine_b'] = p['router_fine_b'][l]
    wts['exp_w_gate'] = p['exp_w_gate'][l].astype(BF16)
    wts['exp_w_up'] = p['exp_w_up'][l].astype(BF16)
    wts['exp_w_down'] = p['exp_w_down'][l].astype(BF16)
    wts['ln2_g'] = p['ln2_g'][l].reshape(1, -1)
    wts['ln2_b'] = p['ln2_b'][l].reshape(1, -1)
    return wts


def _encode(x, p, wts):
    bsz, t, d = x.shape
    t_pad = t + SEQ_TAIL
    n = bsz * t_pad
    h0, h0b = _ln_in(x, p['meta'], p['ln_in_g'], p['ln_in_b'])
    h0 = h0.reshape(n, d)
    h0b = h0b.reshape(n, d)
    bm = _row_block(t_pad, 1024)
    u = _mm_slabs(h0b, wts['w_u'], bm, "proj_s5_in")
    rw = _mm(h0b, wts['w_rw'], bm, RW_COLS // 4, F32, "proj_rwkv_in")
    slabs = u.shape[0]
    y_ssm = _s5_ssm(u.reshape(slabs, bsz, t_pad, 128), wts['s5']).reshape(slabs, n, 128)
    s5_out = _s5_post(y_ssm, u, wts['s5_D'], wts['glu_w'], wts['glu_b'], bm)
    r, kk, v, g, bv, lw, kd, bb = _rwkv_prep(rw.reshape(bsz, t_pad, RW_COLS), wts, t)
    yf, yb = _rwkv_scan(r, kk, v, lw, kd, bb)
    rw_out = _rwkv_post(yf.reshape(n, -1), yb.reshape(n, -1), bv.reshape(n, -1), g.reshape(n, -1), wts, bm)
    merged = _merge(h0b, s5_out, rw_out, wts, bm)
    h1, logits = _out_proj(merged, h0, wts, _row_block(t_pad, 512))
    seq_pos = jnp.arange(n, dtype=jnp.int32) % t_pad
    valid = (seq_pos < t) | (seq_pos >= t_pad - N_META)
    row_tok, pos, w, blk_exp, n_used, n_blocks = _route(logits, wts, valid)
    eo = _moe_experts(h1, row_tok, blk_exp, n_used, n_blocks, wts)
    return _moe_combine(eo, pos, w, h1, wts, bsz, t, t_pad)


def kernel(x_prompt, x_sample, meta, ln_in_g, ln_in_b, w_in, shift_mu, s5_B_re, s5_B_im, s5_A_re, s5_A_im, s5_log_dt, s5_C_re, s5_C_im, s5_D, s5_glu_w, s5_glu_b, rw_w0, rw_w2, rw_a0, rw_a2, rw_g2, rw_k_k, rw_k_a, rw_r_k, rw_lnx_g, rw_lnx_b, proj_s5, proj_rwkv, gate_b, w_out, ln1_g, ln1_b, router_coarse, router_coarse_b, router_fine, router_fine_b, exp_w_gate, exp_w_up, exp_w_down, ln2_g, ln2_b):
    p = {
        'meta': meta, 'ln_in_g': ln_in_g, 'ln_in_b': ln_in_b, 'w_in': w_in, 'shift_mu': shift_mu,
        's5_B_re': s5_B_re, 's5_B_im': s5_B_im, 's5_A_re': s5_A_re, 's5_A_im': s5_A_im,
        's5_log_dt': s5_log_dt, 's5_C_re': s5_C_re, 's5_C_im': s5_C_im, 's5_D': s5_D,
        's5_glu_w': s5_glu_w, 's5_glu_b': s5_glu_b,
        'rw_w0': rw_w0, 'rw_w2': rw_w2, 'rw_a0': rw_a0, 'rw_a2': rw_a2, 'rw_g2': rw_g2,
        'rw_k_k': rw_k_k, 'rw_k_a': rw_k_a, 'rw_r_k': rw_r_k, 'rw_lnx_g': rw_lnx_g, 'rw_lnx_b': rw_lnx_b,
        'proj_s5': proj_s5, 'proj_rwkv': proj_rwkv, 'gate_b': gate_b, 'w_out': w_out,
        'ln1_g': ln1_g, 'ln1_b': ln1_b,
        'router_coarse': router_coarse, 'router_coarse_b': router_coarse_b,
        'router_fine': router_fine, 'router_fine_b': router_fine_b,
        'exp_w_gate': exp_w_gate, 'exp_w_up': exp_w_up, 'exp_w_down': exp_w_down,
        'ln2_g': ln2_g, 'ln2_b': ln2_b,
    }
    wts = _prepare_weights(p)
    return (_encode(x_prompt, p, wts), _encode(x_sample, p, wts))
```

```python
import functools
import math

import jax
import jax.numpy as jnp
from jax import lax
from jax.experimental import pallas as pl
from jax.experimental.pallas import tpu as pltpu

F32 = jnp.float32
BF16 = jnp.bfloat16

D_MODEL = 2048
N_META = 16
S5_WIDTH = 1024
S5_GROUP = 16
S5_GROUPS = 64
S5_STATE = 64
S5_GB = 8
S5_PAIR_MAX_CHUNKS = 160
S5_CHUNK = 16
RW_WIDTH = 1024
RW_HEAD = 64
RW_HEADS = 16
RW_DECAY_LORA = 64
RW_ICLR_LORA = 64
RW_GATE_LORA = 160
RW_GATE_PAD = 256
RW_COLS = 3 * RW_WIDTH + 2 * RW_DECAY_LORA + 2 * RW_ICLR_LORA + RW_GATE_PAD
RW_CHUNK = 64
RW_HEADS_PER_STEP = 2
MOE_GROUPS = 4
EXPERTS_PER_GROUP = 8
N_EXPERTS = 32
D_EXPERT = 512
MOE_ROWS = 256
ROUTER_PAD = 128
DEPTH = 1
ALPHA = (2 * DEPTH) ** 0.25
LN_EPS = 1e-5
GN_EPS = 64e-5
SEQ_TAIL = 64
VMEM_LIMIT = 56 * 1024 * 1024


def _cparams(*sem):
    return pltpu.CompilerParams(dimension_semantics=sem, vmem_limit_bytes=VMEM_LIMIT)


def _row_block(t_pad, cap, mult=8):
    best = mult
    for d in range(mult, cap + 1, mult):
        if t_pad % d == 0:
            best = d
    return best


def _dot(a, b):
    return jnp.dot(a, b, preferred_element_type=F32)


def _dot_nt(a, b):
    return lax.dot_general(a, b, (((1,), (1,)), ((), ())), preferred_element_type=F32)


def _dot_tn(a, b):
    return lax.dot_general(a, b, (((0,), (0,)), ((), ())), preferred_element_type=F32)


def _split(x):
    hi = x.astype(BF16)
    lo = (x - hi.astype(F32)).astype(BF16)
    return hi, lo


def _layernorm(x, g, b):
    mu = jnp.mean(x, axis=-1, keepdims=True)
    xc = x - mu
    var = jnp.mean(xc * xc, axis=-1, keepdims=True)
    return xc * lax.rsqrt(var + LN_EPS) * g + b


def _ln_in_kernel(x_ref, m_ref, g_ref, b_ref, of_ref, ob_ref, *, n_token_blocks):
    j = pl.program_id(1)
    bx, d = x_ref.shape[1], x_ref.shape[2]

    @pl.when(j < n_token_blocks)
    def _():
        y = _layernorm(x_ref[0], g_ref[...], b_ref[...])
        of_ref[0] = y
        ob_ref[0] = y.astype(BF16)

    @pl.when(j == n_token_blocks)
    def _():
        ym = _layernorm(m_ref[...], g_ref[...], b_ref[...])
        y = jnp.concatenate([jnp.zeros((SEQ_TAIL - N_META, d), F32), ym, jnp.zeros((bx - SEQ_TAIL, d), F32)],
                            axis=0)
        of_ref[0] = y
        ob_ref[0] = y.astype(BF16)


def _ln_in(x, meta, g, b):
    bsz, t, d = x.shape
    assert t % SEQ_TAIL == 0
    t_pad = t + SEQ_TAIL
    bx = _row_block(t, 512, SEQ_TAIL)
    nxb = t // bx
    row = pl.BlockSpec((1, bx, d), lambda i, j: (i, j, 0))
    vec = pl.BlockSpec((1, d), lambda i, j: (0, 0))
    return pl.pallas_call(
        functools.partial(_ln_in_kernel, n_token_blocks=nxb),
        grid=(bsz, nxb + 1),
        in_specs=[pl.BlockSpec((1, bx, d), lambda i, j: (i, jnp.minimum(j, nxb - 1), 0)),
                  pl.BlockSpec((N_META, d), lambda i, j: (0, 0)), vec, vec],
        out_specs=[row, row],
        out_shape=[jax.ShapeDtypeStruct((bsz, t_pad, d), F32), jax.ShapeDtypeStruct((bsz, t_pad, d), BF16)],
        compiler_params=_cparams("parallel", "parallel"),
        name="ln_in",
    )(x, meta, g.reshape(1, d), b.reshape(1, d))


def _mm_kernel(x_ref, w_ref, o_ref):
    o_ref[...] = _dot(x_ref[...], w_ref[...]).astype(o_ref.dtype)


def _mm(x, w, bm, bn, out_dtype, name):
    n, k = x.shape
    m = w.shape[1]
    return pl.pallas_call(
        _mm_kernel,
        grid=(m // bn, n // bm),
        in_specs=[pl.BlockSpec((bm, k), lambda j, i: (i, 0)),
                  pl.BlockSpec((k, bn), lambda j, i: (0, j))],
        out_specs=pl.BlockSpec((bm, bn), lambda j, i: (i, j)),
        out_shape=jax.ShapeDtypeStruct((n, m), out_dtype),
        compiler_params=_cparams("parallel", "parallel"),
        name=name,
    )(x, w)


def _mm_slab_kernel(x_ref, w_ref, o_ref):
    res = _dot(x_ref[...], w_ref[...])
    for g in range(o_ref.shape[0]):
        o_ref[g] = res[:, g * 128:(g + 1) * 128]


def _mm_slabs(x, w, bm, name):
    n, k = x.shape
    m = w.shape[1]
    return pl.pallas_call(
        _mm_slab_kernel,
        grid=(n // bm,),
        in_specs=[pl.BlockSpec((bm, k), lambda i: (i, 0)), pl.BlockSpec((k, m), lambda i: (0, 0))],
        out_specs=pl.BlockSpec((m // 128, bm, 128), lambda i: (0, i, 0)),
        out_shape=jax.ShapeDtypeStruct((m // 128, n, 128), F32),
        compiler_params=_cparams("parallel"),
        name=name,
    )(x, w)


def _s5_matrices(b_re, b_im, a_re, a_im, log_dt, c_re, c_im):
    L = S5_CHUNK
    dt = jnp.exp(log_dt)[..., None]
    mag = jnp.exp(a_re * dt)
    abr = mag * jnp.cos(a_im * dt)
    abi = mag * jnp.sin(a_im * dt)
    den = a_re * a_re + a_im * a_im
    nr = abr - 1.0
    cr = (nr * a_re + abi * a_im) / den
    ci = (abi * a_re - nr * a_im) / den
    bbr = cr[..., None] * b_re - ci[..., None] * b_im
    bbi = cr[..., None] * b_im + ci[..., None] * b_re
    tau = jnp.arange(L + 1, dtype=F32)[:, None, None, None]
    pmag = jnp.exp(tau * a_re * dt)
    pr = pmag * jnp.cos(tau * a_im * dt)
    pi = pmag * jnp.sin(tau * a_im * dt)
    wr = pr[..., None] * bbr - pi[..., None] * bbi
    wi = pr[..., None] * bbi + pi[..., None] * bbr
    kern = (jnp.einsum('zgop,tzgpi->tzgoi', c_re, wr)
            - jnp.einsum('zgop,tzgpi->tzgoi', c_im, wi))
    s = jnp.arange(L)[:, None]
    t = jnp.arange(L)[None, :]
    lag = t - s
    kf = jnp.where((lag >= 0)[..., None, None, None], kern[jnp.clip(lag, 0, L), 0], 0.0)
    kb = jnp.where((lag <= 0)[..., None, None, None], kern[jnp.clip(-lag, 0, L), 1], 0.0)
    toep = (kf + kb).transpose(2, 0, 4, 1, 3).reshape(S5_GROUPS, L * S5_GROUP, L * S5_GROUP)
    wf_r, wf_i = wr[::-1][1:, 0], wi[::-1][1:, 0]
    wb_r, wb_i = wr[:L, 1], wi[:L, 1]
    bmat = jnp.concatenate([wf_r, wb_r, wf_i, wb_i], axis=2)
    bmat = bmat.transpose(1, 0, 3, 2).reshape(S5_GROUPS, L * S5_GROUP, 4 * S5_STATE)
    pf_r, pf_i = pr[1:, 0], pi[1:, 0]
    pb_r, pb_i = pr[::-1][:L, 1], pi[::-1][:L, 1]
    c0r, c0i, c1r, c1i = c_re[0], c_im[0], c_re[1], c_im[1]

    def cpow(cre, cim, p_r, p_i):
        re = cre[None] * p_r[:, :, None, :] - cim[None] * p_i[:, :, None, :]
        im = cre[None] * p_i[:, :, None, :] + cim[None] * p_r[:, :, None, :]
        return re, -im

    f_re, f_im = cpow(c0r, c0i, pf_r, pf_i)
    g_re, g_im = cpow(c1r, c1i, pb_r, pb_i)
    cmat = jnp.concatenate([f_re, g_re, f_im, g_im], axis=3)
    cmat = cmat.transpose(1, 3, 0, 2).reshape(S5_GROUPS, 4 * S5_STATE, L * S5_GROUP)
    lam_re = jnp.concatenate([pr[L, 0], pr[L, 1]], axis=-1)[:, None, :]
    lam_im = jnp.concatenate([pi[L, 0], pi[L, 1]], axis=-1)[:, None, :]
    return bmat, toep, cmat, lam_re, lam_im


def _s5_expand_kernel(src_ref, e_ref, o_ref, *, pieces):
    for j in range(src_ref.shape[0]):
        x = _dot(src_ref[j].astype(BF16), e_ref[j]).astype(BF16)
        for src, rows, dst, step in pieces:
            o_ref[0, dst + j * step:dst + j * step + rows, :] = x[src:src + rows, :]


def _s5_expand(per_group, expand, pieces):
    groups, rows, cols = per_group.shape
    gb, _, wide = expand.shape
    return pl.pallas_call(
        functools.partial(_s5_expand_kernel, pieces=pieces),
        grid=(groups // gb,),
        in_specs=[pl.BlockSpec((gb, rows, cols), lambda g: (g, 0, 0)),
                  pl.BlockSpec((gb, cols, wide), lambda g: (0, 0, 0))],
        out_specs=pl.BlockSpec((1, gb * rows, wide), lambda g: (g, 0, 0)),
        out_shape=jax.ShapeDtypeStruct((groups // gb, gb * rows, wide), BF16),
        compiler_params=_cparams("parallel"),
        name="s5_expand",
    )(per_group, expand)


def _s5_block_operators(bmat, toep, cmat, lam_re, lam_im):
    nb, gb, L, c, p = S5_GROUPS // S5_GB, S5_GB, S5_CHUNK, S5_GROUP, S5_STATE
    quarters = (0, 2, 1, 3)
    wide = L * gb * c
    r = jnp.arange(L * c)[None, :, None]
    col = jnp.arange(wide)[None, None, :]
    j = jnp.arange(gb)[:, None, None]
    e_tok = ((r // c == col // (gb * c)) & ((col // c) % gb == j) & (r % c == col % c)).astype(BF16)
    k_of_col = jnp.array(quarters)[col // (gb * p)]
    e_state = ((r // p == k_of_col) & ((col // p) % gb == j) & (r % p == col % p)).astype(BF16)
    tok_rows = tuple((s * c, c, s * gb * c, c) for s in range(L))
    state_rows = tuple((quarters[k] * p, p, k * gb * p, p) for k in range(4))
    wb = _s5_expand(bmat, e_state, tok_rows)
    wt = _s5_expand(toep, e_tok, tok_rows)
    wc = _s5_expand(cmat, e_tok, state_rows)
    ar_f = lam_re[:, 0, :p].reshape(nb, gb * p)
    ar_b = lam_re[:, 0, p:].reshape(nb, gb * p)
    ai_f = lam_im[:, 0, :p].reshape(nb, gb * p)
    ai_b = lam_im[:, 0, p:].reshape(nb, gb * p)
    a1 = jnp.stack([ar_f, ar_f, ar_b, ar_b] * 2, axis=1)
    a2 = jnp.stack([-ai_f, ai_f, -ai_b, ai_b] * 2, axis=1)
    return wb, wt, wc, a1, a2


def _s5_kernel(u_ref, wb_ref, wt_ref, wc_ref, a1_ref, a2_ref, y_ref, s_scr, xf_scr, xb_scr, *, n_chunks):
    L = S5_CHUNK
    C = n_chunks
    nseq = u_ref.shape[1]
    cp = -(-C // 8) * 8
    sw = S5_GB * S5_STATE
    tail = SEQ_TAIL // L

    def rows_of(parts):
        if cp > C:
            pad = jnp.zeros((cp - C, parts[0].shape[1]), F32)
            parts = [x for part in parts for x in (part, pad)]
        return jnp.concatenate(parts, axis=0).astype(BF16)

    u8 = rows_of([jnp.concatenate([u_ref[0, q, pl.ds(s, C, stride=L), :] for s in range(L)], axis=1)
                  for q in range(nseq)])
    s_all = _dot(u8, wb_ref[0])
    for q in range(nseq):
        for k in range(4):
            s_scr[:, 4 * q + k, :] = s_all[q * cp:q * cp + C, k * sw:(k + 1) * sw]
    if nseq == 1:
        s_scr[:, 4:8, :] = jnp.zeros((C, 4, sw), F32)
    a1 = a1_ref[0]
    a2 = a2_ref[0]
    row = lax.broadcasted_iota(jnp.int32, (8, sw), 0)
    even = (row % 2) == 0
    is_fwd = (row % 4) < 2

    def step(c, x):
        cf = (c + C - tail) % C
        cb = (2 * C - 1 - c - tail) % C
        xf_scr[cf] = x
        xb_scr[cb] = x
        s = jnp.where(is_fwd, s_scr[cf], s_scr[cb])
        swapped = jnp.where(even, pltpu.roll(x, 7, 0), pltpu.roll(x, 1, 0))
        return a1 * x + a2 * swapped + s

    lax.fori_loop(0, C, step, jnp.zeros((8, sw), F32))
    x_in = rows_of([jnp.concatenate([xf_scr[:, 4 * q, :], xf_scr[:, 4 * q + 1, :],
                                     xb_scr[:, 4 * q + 2, :], xb_scr[:, 4 * q + 3, :]], axis=1)
                    for q in range(nseq)])
    y8 = _dot(u8, wt_ref[0]) + _dot(x_in, wc_ref[0])
    for q in range(nseq):
        for t in range(L):
            y_ref[0, q, pl.ds(t, C, stride=L), :] = y8[q * cp:q * cp + C, t * 128:(t + 1) * 128]


def _s5_ssm(u3, ops):
    wb, wt, wc, a1, a2 = ops
    _, bsz, t_pad, _ = u3.shape
    n_chunks = t_pad // S5_CHUNK
    nb = S5_GROUPS // S5_GB
    lanes = S5_GB * S5_GROUP
    sw = S5_GB * S5_STATE
    nseq = 2 if (bsz % 2 == 0 and n_chunks <= S5_PAIR_MAX_CHUNKS) else 1
    blk = pl.BlockSpec((1, nseq, t_pad, lanes), lambda g, b: (g, b, 0, 0))
    mat = pl.BlockSpec((1,) + wb.shape[1:], lambda g, b: (g, 0, 0), pipeline_mode=pl.Buffered(1))
    vec = pl.BlockSpec((1, 8, sw), lambda g, b: (g, 0, 0))
    return pl.pallas_call(
        functools.partial(_s5_kernel, n_chunks=n_chunks),
        grid=(nb, bsz // nseq),
        in_specs=[blk, mat, mat, mat, vec, vec],
        out_specs=blk,
        out_shape=jax.ShapeDtypeStruct(u3.shape, F32),
        scratch_shapes=[pltpu.VMEM((n_chunks, 8, sw), F32)] * 3,
        compiler_params=_cparams("arbitrary", "arbitrary"),
        name="s5_ssm",
    )(u3, wb, wt, wc, a1, a2)


def _s5_post_kernel(y_ref, u_ref, d_ref, w_ref, b_ref, o_ref):
    slabs = range(y_ref.shape[0])
    y = (jnp.concatenate([y_ref[g] for g in slabs], axis=1)
         + jnp.concatenate([u_ref[g] for g in slabs], axis=1) * d_ref[...])
    act = y * (0.5 * (1.0 + jnp.tanh(math.sqrt(2.0 / math.pi) * (y + 0.044715 * (y * y * y)))))
    z = _dot(act.astype(BF16), w_ref[...]) + b_ref[...]
    o_ref[...] = (act * jax.nn.sigmoid(z)).astype(o_ref.dtype)


def _s5_post(y, u, d_skip, glu_w, glu_b, bm):
    slabs, n, lanes = y.shape
    row = pl.BlockSpec((bm, S5_WIDTH), lambda i: (i, 0))
    slab = pl.BlockSpec((slabs, bm, lanes), lambda i: (0, i, 0))
    vec = pl.BlockSpec((1, S5_WIDTH), lambda i: (0, 0))
    return pl.pallas_call(
        _s5_post_kernel,
        grid=(n // bm,),
        in_specs=[slab, slab, vec, pl.BlockSpec((S5_WIDTH, S5_WIDTH), lambda i: (0, 0)), vec],
        out_specs=row,
        out_shape=jax.ShapeDtypeStruct((n, S5_WIDTH), BF16),
        compiler_params=_cparams("parallel"),
        name="s5_post",
    )(y, u, d_skip.reshape(1, -1), glu_w, glu_b.reshape(1, -1))


def _head_sum(x, e_ref, et_ref):
    hi, lo = _split(x)
    s = _dot(hi, e_ref[...]) + _dot(lo, e_ref[...])
    shi, slo = _split(s)
    return _dot(shi, et_ref[...]) + _dot(slo, et_ref[...])


def _rwkv_prep_kernel(cur_ref, prev_ref, next_ref, mu_ref, w2_ref, a2_ref, g2_ref, w0_ref, a0_ref,
                      kk_ref, ka_ref, rk_ref, e_ref, et_ref,
                      r_o, kk_o, v_o, g_o, bv_o, lw_o, kd_o, bb_o, *, t, t_pad, bm):
    j = pl.program_id(1)
    p = cur_ref[0]
    row = lax.broadcasted_iota(jnp.int32, (bm, 1), 0)
    prev_row = prev_ref[0, 7:8, :]
    next_row = next_ref[0, 0:1, :]
    prev = jnp.where(row == 0, prev_row, pltpu.roll(p, 1, 0))
    nxt = jnp.where(row == bm - 1, next_row, pltpu.roll(p, bm - 1, 0))
    xs = p + mu_ref[0:1, :] * (prev - p) + mu_ref[1:2, :] * (nxt - p)
    w = RW_WIDTH
    r = xs[:, 0:w]
    k = xs[:, w:2 * w]
    v = xs[:, 2 * w:3 * w]
    lw = xs[:, 3 * w:3 * w + 128]
    la = xs[:, 3 * w + 128:3 * w + 256]
    lg = xs[:, 3 * w + 256:]
    w_log = _dot(jnp.tanh(lw).astype(BF16), w2_ref[...])
    a_lin = _dot(la.astype(BF16), a2_ref[...])
    g = _dot(jax.nn.sigmoid(lg).astype(BF16), g2_ref[...])
    kk = k * kk_ref[...]
    n2 = _head_sum(kk * kk, e_ref, et_ref)
    kk = kk / jnp.maximum(jnp.sqrt(n2), 1e-12)
    pos = j * bm + row
    valid = (pos < t) | (pos >= t_pad - N_META)
    v = jnp.where(valid, v, 0.0)
    kd_sum = jnp.zeros_like(k)
    for z in range(2):
        wl = w_log[:, z * w:(z + 1) * w] + w0_ref[z:z + 1, :]
        lw_o[z, 0] = -math.exp(-0.5) * jax.nn.sigmoid(wl)
        a = jax.nn.sigmoid(a_lin[:, z * w:(z + 1) * w] + a0_ref[z:z + 1, :])
        kd = k * (1.0 + (a - 1.0) * ka_ref[...])
        kd_o[z, 0] = kd.astype(BF16)
        bb_o[z, 0] = (kk * a).astype(BF16)
        kd_sum = kd_sum + kd
    bonus = _head_sum(r * kd_sum * rk_ref[...], e_ref, et_ref)
    r_o[0] = r.astype(BF16)
    kk_o[0] = kk.astype(BF16)
    v_o[0] = v.astype(BF16)
    g_o[0] = g.astype(BF16)
    bv_o[0] = (bonus * v).astype(BF16)


def _rwkv_prep(rw3, wts, t):
    bsz, t_pad, _ = rw3.shape
    bm = _row_block(t_pad, 320, 64)
    nb8 = bm // 8
    n8 = t_pad // 8
    w = RW_WIDTH
    cur = pl.BlockSpec((1, bm, RW_COLS), lambda b, j: (b, j, 0))
    prev = pl.BlockSpec((1, 8, RW_COLS), lambda b, j: (b, (j * nb8 + n8 - 1) % n8, 0))
    nxt = pl.BlockSpec((1, 8, RW_COLS), lambda b, j: (b, ((j + 1) * nb8) % n8, 0))

    def full(a):
        return pl.BlockSpec(a.shape, lambda b, j: (0,) * a.ndim)

    shared = pl.BlockSpec((1, bm, w), lambda b, j: (b, j, 0))
    per_dir = pl.BlockSpec((2, 1, bm, w), lambda b, j: (0, b, j, 0))
    consts = [wts['mu'], wts['w2'], wts['a2'], wts['g2'], wts['w0'], wts['a0'],
              wts['k_k'], wts['k_a'], wts['r_k'], wts['head_e'], wts['head_et']]
    sds = jax.ShapeDtypeStruct
    return pl.pallas_call(
        functools.partial(_rwkv_prep_kernel, t=t, t_pad=t_pad, bm=bm),
        grid=(bsz, t_pad // bm),
        in_specs=[cur, prev, nxt] + [full(a) for a in consts],
        out_specs=[shared] * 5 + [per_dir] * 3,
        out_shape=[sds((bsz, t_pad, w), BF16)] * 5
        + [sds((2, bsz, t_pad, w), F32), sds((2, bsz, t_pad, w), BF16), sds((2, bsz, t_pad, w), BF16)],
        compiler_params=_cparams("parallel", "parallel"),
        name="rwkv_prep",
    )(rw3, rw3, rw3, *consts)


def _rwkv_chunk(fwd, r_ref, kk_ref, v_ref, lw_ref, kd_ref, bb_ref, y_ref, st_ref, z):
    L = RW_CHUNK
    hd = RW_HEAD
    gw = RW_HEADS_PER_STEP * hd
    n_groups = RW_WIDTH // gw
    row = lax.broadcasted_iota(jnp.int32, (L, L), 0)
    col = lax.broadcasted_iota(jnp.int32, (L, L), 1)
    tri = jnp.where((col <= row) if fwd else (col >= row), 1.0, 0.0).astype(BF16)
    grow = lax.broadcasted_iota(jnp.int32, (L, gw), 0)
    gcol = lax.broadcasted_iota(jnp.int32, (L, gw), 1) % L
    incl = (gcol <= grow) if fwd else (gcol >= grow)
    strict = (gcol < grow) if fwd else (gcol > grow)
    bd_mask = jnp.where(lax.broadcasted_iota(jnp.int32, (gw, gw), 0) // hd
                        == lax.broadcasted_iota(jnp.int32, (gw, gw), 1) // hd, 1.0, 0.0).astype(BF16)

    def bd(x):
        return jnp.concatenate([x] * RW_HEADS_PER_STEP, axis=0) * bd_mask

    def stack(x):
        return jnp.concatenate([x[:, h * hd:(h + 1) * hd] for h in range(RW_HEADS_PER_STEP)], axis=0)

    lw = lw_ref[0, 0]
    lw_hi, lw_lo = _split(lw)
    c = _dot(tri, lw_hi) + _dot(tri, lw_lo)
    e = c - lw
    c_tot = c[L - 1:L, :] if fwd else c[0:1, :]
    r = r_ref[0].astype(F32)
    kk = kk_ref[0].astype(F32)
    kd = kd_ref[0, 0].astype(F32)
    bb = bb_ref[0, 0].astype(F32)
    v = v_ref[0]
    q1 = (kk * jnp.exp(e)).astype(BF16)
    q2 = (r * jnp.exp(c)).astype(BF16)
    inv = jnp.exp(-c)
    k1 = (kd * inv).astype(BF16)
    k2 = (bb * inv).astype(BF16)
    rest = jnp.exp(c_tot - c)
    k1p = (kd * rest).astype(BF16)
    k2p = (bb * rest).astype(BF16)
    dec_tot = jnp.exp(c_tot)
    def group_chain(g):
        sl = slice(g * gw, (g + 1) * gw)
        s0 = st_ref[z, g]
        s0_hi, s0_lo = _split(s0)
        lhs = jnp.concatenate([q1[:, sl], q2[:, sl]], axis=0)
        rhs = jnp.concatenate([bd(k1[:, sl]), bd(k2[:, sl]), bd(s0_hi), bd(s0_lo)], axis=0)
        m1 = _dot_nt(lhs, rhs)
        yield
        a_kd = jnp.where(strict, m1[:L, 0:gw], 0.0)
        a_b = jnp.where(strict, m1[:L, gw:2 * gw], 0.0)
        q1s = m1[:L, 2 * gw:3 * gw] + m1[:L, 3 * gw:]
        b_kd = jnp.where(incl, m1[L:, 0:gw], 0.0)
        b_b = jnp.where(incl, m1[L:, gw:2 * gw], 0.0)
        q2s = m1[L:, 2 * gw:3 * gw] + m1[L:, 3 * gw:]
        vg = v[:, sl]
        v_bd = bd(vg)
        x = q1s + _dot(a_kd.astype(BF16), v_bd)
        yield
        m = -a_b
        levels = L.bit_length() - 1
        for lvl in range(levels):
            mb = m.astype(BF16)
            if lvl < levels - 1:
                rr = _dot(mb, jnp.concatenate([bd(x.astype(BF16)), bd(mb)], axis=1))
                x = x + rr[:, :gw]
                m = rr[:, gw:]
            else:
                x = x + _dot(mb, bd(x.astype(BF16)))
            yield
        ub = x.astype(BF16)
        y = q2s + _dot(jnp.concatenate([b_kd, -b_b], axis=1).astype(BF16),
                       jnp.concatenate([v_bd, bd(ub)], axis=0))
        y_ref[0, :, sl] = y
        yield
        st_ref[z, g] = s0 * dec_tot[:, sl] + _dot_tn(
            jnp.concatenate([stack(vg), stack(ub)], axis=0),
            jnp.concatenate([bd(k1p[:, sl]), -bd(k2p[:, sl])], axis=0))

    return [group_chain(g) for g in range(n_groups)]


def _rwkv_scan_kernel(rf, kkf, vf, lwf, kdf, bbf, rb, kkb, vb, lwb, kdb, bbb, yf_ref, yb_ref, st_ref):
    @pl.when(pl.program_id(1) == 0)
    def _():
        st_ref[...] = jnp.zeros_like(st_ref)

    chains = (_rwkv_chunk(True, rf, kkf, vf, lwf, kdf, bbf, yf_ref, st_ref, 0)
              + _rwkv_chunk(False, rb, kkb, vb, lwb, kdb, bbb, yb_ref, st_ref, 1))
    while chains:
        alive = []
        for chain in chains:
            try:
                next(chain)
                alive.append(chain)
            except StopIteration:
                pass
        chains = alive


def _rwkv_scan(r, kk, v, lw, kd, bb):
    bsz, t_pad, w = r.shape
    L = RW_CHUNK
    nc = t_pad // L
    gw = RW_HEADS_PER_STEP * RW_HEAD
    grid = (bsz, nc)

    def block(fwd, j):
        logical = j if fwd else nc - 1 - j
        return (logical + nc - 1) % nc

    def shared(fwd):
        return pl.BlockSpec((1, L, w), lambda b, j: (b, block(fwd, j), 0))

    def per_dir(fwd):
        return pl.BlockSpec((1, 1, L, w), lambda b, j: (0 if fwd else 1, b, block(fwd, j), 0))

    in_specs = []
    for fwd in (True, False):
        in_specs += [shared(fwd), shared(fwd), shared(fwd), per_dir(fwd), per_dir(fwd), per_dir(fwd)]
    return pl.pallas_call(
        _rwkv_scan_kernel,
        grid=grid,
        in_specs=in_specs,
        out_specs=[shared(True), shared(False)],
        out_shape=[jax.ShapeDtypeStruct((bsz, t_pad, w), F32)] * 2,
        scratch_shapes=[pltpu.VMEM((2, w // gw, RW_HEAD, gw), F32)],
        compiler_params=_cparams("parallel", "arbitrary"),
        name="rwkv_scan",
    )(r, kk, v, lw, kd, bb, r, kk, v, lw, kd, bb)


def _rwkv_post_kernel(yf_ref, yb_ref, bv_ref, g_ref, lg_ref, lb_ref, e_ref, et_ref, o_ref):
    y = yf_ref[...] + yb_ref[...]
    mean = _head_sum(y, e_ref, et_ref) * (1.0 / RW_HEAD)
    yc = y - mean
    var = _head_sum(yc * yc, e_ref, et_ref) * (1.0 / RW_HEAD)
    y = yc * lax.rsqrt(var + GN_EPS) * lg_ref[...] + lb_ref[...]
    o_ref[...] = ((y + bv_ref[...].astype(F32)) * g_ref[...].astype(F32)).astype(o_ref.dtype)


def _rwkv_post(yf, yb, bv, g, wts, bm):
    n, w = yf.shape
    row = pl.BlockSpec((bm, w), lambda i: (i, 0))
    vec = pl.BlockSpec((1, w), lambda i: (0, 0))
    e, et = wts['head_e'], wts['head_et']
    return pl.pallas_call(
        _rwkv_post_kernel,
        grid=(n // bm,),
        in_specs=[row, row, row, row, vec, vec,
                  pl.BlockSpec(e.shape, lambda i: (0, 0)), pl.BlockSpec(et.shape, lambda i: (0, 0))],
        out_specs=row,
        out_shape=jax.ShapeDtypeStruct((n, w), BF16),
        compiler_params=_cparams("parallel"),
        name="rwkv_post",
    )(yf, yb, bv, g, wts['lnx_g'], wts['lnx_b'], e, et)


def _merge_kernel(h_ref, s5_ref, rw_ref, wg0_ref, wg1_ref, gb_ref, p0_ref, p1_ref, o_ref):
    h = h_ref[...]
    g0 = jax.nn.sigmoid(_dot(h, wg0_ref[...]) + gb_ref[0:1, :])
    g1 = jax.nn.sigmoid(_dot(h, wg1_ref[...]) + gb_ref[1:2, :])
    merged = g0 * _dot(s5_ref[...], p0_ref[...]) + g1 * _dot(rw_ref[...], p1_ref[...])
    o_ref[...] = merged.astype(o_ref.dtype)


def _merge(h0b, s5_out, rw_out, wts, bm):
    n, d = h0b.shape
    bn = 1024
    nj = d // bn
    return pl.pallas_call(
        _merge_kernel,
        grid=(nj, n // bm),
        in_specs=[pl.BlockSpec((bm, d), lambda j, i: (i, 0)),
                  pl.BlockSpec((bm, S5_WIDTH), lambda j, i: (i, 0)),
                  pl.BlockSpec((bm, RW_WIDTH), lambda j, i: (i, 0)),
                  pl.BlockSpec((d, bn), lambda j, i: (0, j)),
                  pl.BlockSpec((d, bn), lambda j, i: (0, nj + j)),
                  pl.BlockSpec((2, bn), lambda j, i: (0, j)),
                  pl.BlockSpec((S5_WIDTH, bn), lambda j, i: (0, j)),
                  pl.BlockSpec((RW_WIDTH, bn), lambda j, i: (0, j))],
        out_specs=pl.BlockSpec((bm, bn), lambda j, i: (i, j)),
        out_shape=jax.ShapeDtypeStruct((n, d), BF16),
        compiler_params=_cparams("parallel", "parallel"),
        name="merge",
    )(h0b, s5_out, rw_out, wts['w_gate'], wts['w_gate'], wts['gate_b'], wts['proj_s5'], wts['proj_rwkv'])


def _pack_bf16_pairs(x):
    half = x.shape[1] // 2
    bits = lax.bitcast_convert_type(x.astype(BF16).astype(F32), jnp.uint32)
    return bits[:, half:] | (bits[:, :half] >> 16)


def _unpack_bf16_pairs(p, dtype):
    lo = lax.bitcast_convert_type(p << 16, F32)
    hi = lax.bitcast_convert_type(p & jnp.uint32(0xFFFF0000), F32)
    return jnp.concatenate([lo, hi], axis=1).astype(dtype)


def _out_kernel(m_ref, h_ref, w_ref, g_ref, b_ref, rh_ref, rl_ref, o_ref, op_ref, lg_ref):
    bm = m_ref.shape[0]
    halves = [slice(0, bm // 2), slice(bm // 2, bm)]
    xs = [ALPHA * h_ref[rows, :] + _dot(m_ref[rows, :], w_ref[...]) for rows in halves]
    for rows, x in zip(halves, xs):
        h1 = _layernorm(x, g_ref[...], b_ref[...])
        o_ref[rows, :] = h1
        op_ref[rows, :] = _pack_bf16_pairs(h1)
        hi, lo = _split(h1)
        lg_ref[rows, :] = _dot(hi, rh_ref[...]) + _dot(lo, rh_ref[...]) + _dot(hi, rl_ref[...])


def _out_proj(merged, h0, wts, bm):
    n, d = h0.shape
    row = pl.BlockSpec((bm, d), lambda i: (i, 0))
    vec = pl.BlockSpec((1, d), lambda i: (0, 0))
    rt = pl.BlockSpec((d, ROUTER_PAD), lambda i: (0, 0))
    return pl.pallas_call(
        _out_kernel,
        grid=(n // bm,),
        in_specs=[row, row, pl.BlockSpec((d, d), lambda i: (0, 0)), vec, vec, rt, rt],
        out_specs=[row, pl.BlockSpec((bm, d // 2), lambda i: (i, 0)),
                   pl.BlockSpec((bm, ROUTER_PAD), lambda i: (i, 0))],
        out_shape=[jax.ShapeDtypeStruct((n, d), F32), jax.ShapeDtypeStruct((n, d // 2), jnp.uint32),
                   jax.ShapeDtypeStruct((n, ROUTER_PAD), F32)],
        compiler_params=_cparams("parallel"),
        name="out_proj",
    )(merged, h0, wts['w_out'], wts['ln1_g'], wts['ln1_b'], wts['router_hi'], wts['router_lo'])


def _route(logits, wts, valid):
    i32 = jnp.int32
    lc = logits[:, :MOE_GROUPS] + wts['router_coarse_b']
    grp = jnp.argmax(lc, axis=-1).astype(i32)
    gate_c = jnp.max(jax.nn.softmax(lc, axis=-1), axis=-1)
    lf = (logits[:, MOE_GROUPS:MOE_GROUPS + N_EXPERTS] + wts['router_fine_b'])
    lf = lf.reshape(-1, MOE_GROUPS, EXPERTS_PER_GROUP)
    sel = grp[:, None, None] == jnp.arange(MOE_GROUPS, dtype=i32)[None, :, None]
    lf = jnp.sum(jnp.where(sel, lf, 0.0), axis=1)
    lane = jnp.arange(EXPERTS_PER_GROUP, dtype=i32)[None, :]
    i1 = jnp.argmax(lf, axis=-1).astype(i32)
    v1 = jnp.max(lf, axis=-1)
    rest = jnp.where(lane == i1[:, None], -jnp.inf, lf)
    i2 = jnp.argmax(rest, axis=-1).astype(i32)
    v2 = jnp.max(rest, axis=-1)
    top_v = jnp.stack([v1, v2], axis=-1)
    top_i = jnp.stack([i1, i2], axis=-1)
    w = gate_c[:, None] * jax.nn.softmax(top_v, axis=-1)
    expert = grp[:, None] * EXPERTS_PER_GROUP + top_i
    expert = jnp.where(valid[:, None], expert, N_EXPERTS)
    w = jnp.where(valid[:, None], w, 0.0)
    n_tok = logits.shape[0]
    n_asg = 2 * n_tok
    e_flat = expert.reshape(-1)
    order = jnp.argsort(e_flat).astype(i32)
    inv = jnp.argsort(order).astype(i32)
    bounds = jnp.sum(e_flat[None, :] < jnp.arange(N_EXPERTS + 1, dtype=i32)[:, None], axis=1, dtype=i32)
    start = bounds[:N_EXPERTS]
    counts = bounds[1:] - start
    padded = (counts + MOE_ROWS - 1) // MOE_ROWS * MOE_ROWS
    ex = jnp.arange(N_EXPERTS, dtype=i32)
    pend = jnp.sum(jnp.where(ex[None, :] <= ex[:, None], padded[None, :], 0), axis=1)
    pstart = pend - padded
    n_blocks = -(-n_asg // MOE_ROWS) + N_EXPERTS
    n_rows = n_blocks * MOE_ROWS
    n_used = pend[-1] // MOE_ROWS
    blk = jnp.minimum(jnp.arange(n_blocks, dtype=i32), n_used - 1)
    blk_exp = jnp.sum(pend[None, :] <= (blk * MOE_ROWS)[:, None], axis=1, dtype=i32)
    blk_exp = jnp.minimum(blk_exp, N_EXPERTS - 1)
    experts = jnp.arange(N_EXPERTS, dtype=i32)

    def lookup(table, idx):
        return jnp.sum(jnp.where(idx[..., None] == experts, table, 0), axis=-1)

    pos = jnp.where(e_flat < N_EXPERTS, lookup(pstart - start, e_flat) + inv, 0)
    d = jnp.arange(n_rows, dtype=i32).reshape(n_blocks, MOE_ROWS)
    k = d - lookup(pstart, blk_exp)[:, None]
    src = jnp.clip(k + lookup(start, blk_exp)[:, None], 0, n_asg - 1)
    row_tok = jnp.where((k < lookup(counts, blk_exp)[:, None]) & (d < pend[-1]), order[src] // 2, 0)
    return row_tok, pos, w, blk_exp, n_used.reshape(1).astype(i32), n_blocks


def _row_gather_start(src_hbm, idx_ref, n_rows, dst, sem, stride=1, offset=0):
    for r in range(n_rows):
        row = idx_ref[0, 0, stride * r + offset]
        pltpu.make_async_copy(src_hbm.at[pl.ds(row, 1)], dst.at[pl.ds(r, 1)], sem).start(priority=r % 2)


def _row_gather_wait(src_hbm, n_rows, dst, sem):
    def wait(r, carry):
        pltpu.make_async_copy(src_hbm.at[pl.ds(0, 1)], dst.at[pl.ds(r, 1)], sem).wait()
        return carry

    lax.fori_loop(0, n_rows, wait, 0, unroll=8)


def _expert_kernel(nused_ref, bexp_ref, idx_ref, nidx_ref, x_hbm, wg_ref, wu_ref, wd_ref, o_ref, buf, sem):
    i = pl.program_id(0)
    n_used = nused_ref[0]
    slot = i % 2

    @pl.when((i == 0) & (n_used > 0))
    def _():
        _row_gather_start(x_hbm, idx_ref, MOE_ROWS, buf.at[0], sem.at[0])

    @pl.when(i + 1 < n_used)
    def _():
        _row_gather_start(x_hbm, nidx_ref, MOE_ROWS, buf.at[1 - slot], sem.at[1 - slot])

    @pl.when(i < n_used)
    def _():
        _row_gather_wait(x_hbm, MOE_ROWS, buf.at[slot], sem.at[slot])
        x = _unpack_bf16_pairs(buf[slot], BF16)
        hb = jax.nn.silu(_dot(x, wg_ref[0])) * _dot(x, wu_ref[0])
        o_ref[...] = _pack_bf16_pairs(_dot(hb.astype(BF16), wd_ref[0]))

    @pl.when(i >= n_used)
    def _():
        o_ref[...] = jnp.zeros_like(o_ref)


def _moe_experts(h1p, row_tok, blk_exp, n_used, n_blocks, wts):
    d = 2 * h1p.shape[1]
    idx = row_tok.reshape(n_blocks, 1, MOE_ROWS)
    last = n_blocks - 1
    smem = pltpu.SMEM
    return pl.pallas_call(
        _expert_kernel,
        grid_spec=pltpu.PrefetchScalarGridSpec(
            num_scalar_prefetch=2,
            grid=(n_blocks,),
            in_specs=[pl.BlockSpec((1, 1, MOE_ROWS), lambda i, nu, be: (i, 0, 0), memory_space=smem),
                      pl.BlockSpec((1, 1, MOE_ROWS), lambda i, nu, be: (jnp.minimum(i + 1, last), 0, 0),
                                   memory_space=smem),
                      pl.BlockSpec(memory_space=pl.ANY),
                      pl.BlockSpec((1, d, D_EXPERT), lambda i, nu, be: (be[i], 0, 0)),
                      pl.BlockSpec((1, d, D_EXPERT), lambda i, nu, be: (be[i], 0, 0)),
                      pl.BlockSpec((1, D_EXPERT, d), lambda i, nu, be: (be[i], 0, 0))],
            out_specs=pl.BlockSpec((MOE_ROWS, d // 2), lambda i, nu, be: (i, 0)),
            scratch_shapes=[pltpu.VMEM((2, MOE_ROWS, d // 2), jnp.uint32), pltpu.SemaphoreType.DMA((2,))],
        ),
        out_shape=jax.ShapeDtypeStruct((n_blocks * MOE_ROWS, d // 2), jnp.uint32),
        compiler_params=_cparams("arbitrary"),
        name="moe_experts",
    )(n_used, blk_exp, idx, idx, h1p, wts['exp_w_gate'], wts['exp_w_up'], wts['exp_w_down'])


def _combine_kernel(pos_ref, npos_ref, eo_hbm, h_ref, w_ref, g_ref, b_ref, o_ref, buf, sem, *, n_steps, bm):
    i = pl.program_id(0)
    slot = i % 2

    def start(p_ref, s):
        for k in range(2):
            _row_gather_start(eo_hbm, p_ref, bm, buf.at[s, k], sem.at[s], stride=2, offset=k)

    @pl.when(i == 0)
    def _():
        start(pos_ref, 0)

    @pl.when(i + 1 < n_steps)
    def _():
        start(npos_ref, 1 - slot)

    for k in range(2):
        _row_gather_wait(eo_hbm, bm, buf.at[slot, k], sem.at[slot])
    w = w_ref[0]
    moe = (w[:, 0:1] * _unpack_bf16_pairs(buf[slot, 0], F32)
           + w[:, 1:2] * _unpack_bf16_pairs(buf[slot, 1], F32))
    o_ref[0] = _layernorm(ALPHA * h_ref[0] + moe, g_ref[...], b_ref[...])


def _moe_combine(eo, pos, w, h1, wts, bsz, t, t_pad):
    n, d = h1.shape
    bm = _row_block(t, 256)
    per_seq = t // bm
    n_steps = bsz * per_seq
    last = n_steps - 1
    vec = pl.BlockSpec((1, d), lambda i: (0, 0))
    smem = pltpu.SMEM
    pos3 = pos.reshape(bsz, t_pad, 2)[:, :t].reshape(n_steps, 1, 2 * bm)

    def rows(i):
        return (i // per_seq, i % per_seq, 0)

    return pl.pallas_call(
        functools.partial(_combine_kernel, n_steps=n_steps, bm=bm),
        grid=(n_steps,),
        in_specs=[pl.BlockSpec((1, 1, 2 * bm), lambda i: (i, 0, 0), memory_space=smem),
                  pl.BlockSpec((1, 1, 2 * bm), lambda i: (jnp.minimum(i + 1, last), 0, 0), memory_space=smem),
                  pl.BlockSpec(memory_space=pl.ANY),
                  pl.BlockSpec((1, bm, d), rows),
                  pl.BlockSpec((1, bm, 2), rows), vec, vec],
        out_specs=pl.BlockSpec((1, bm, d), rows),
        out_shape=jax.ShapeDtypeStruct((bsz, t, d), F32),
        scratch_shapes=[pltpu.VMEM((2, 2, bm) + eo.shape[1:], eo.dtype), pltpu.SemaphoreType.DMA((2,))],
        compiler_params=_cparams("arbitrary"),
        name="moe_combine",
    )(pos3, pos3, eo, h1.reshape(bsz, t_pad, d), w.reshape(bsz, t_pad, 2), wts['ln2_g'], wts['ln2_b'])


def _prepare_weights(p):
    l = 0
    w_in = p['w_in'][l]
    c0 = S5_WIDTH
    c1 = c0 + 3 * RW_WIDTH + 2 * RW_DECAY_LORA + 2 * RW_ICLR_LORA + RW_GATE_LORA
    gpad = RW_GATE_PAD - RW_GATE_LORA
    wts = {}
    wts['w_u'] = w_in[:, :c0].astype(BF16)
    wts['w_rw'] = jnp.pad(w_in[:, c0:c1], ((0, 0), (0, gpad))).astype(BF16)
    wts['w_gate'] = w_in[:, c1:].astype(BF16)
    wts['mu'] = jnp.pad(p['shift_mu'][l], ((0, 0), (0, gpad)))
    z = jnp.zeros((RW_DECAY_LORA, RW_WIDTH), F32)
    wts['w2'] = jnp.block([[p['rw_w2'][l, 0], z], [z, p['rw_w2'][l, 1]]]).astype(BF16)
    wts['a2'] = jnp.block([[p['rw_a2'][l, 0], z], [z, p['rw_a2'][l, 1]]]).astype(BF16)
    wts['g2'] = jnp.pad(p['rw_g2'][l], ((0, gpad), (0, 0))).astype(BF16)
    wts['w0'] = p['rw_w0'][l]
    wts['a0'] = p['rw_a0'][l]
    wts['k_k'] = p['rw_k_k'][l].reshape(1, -1)
    wts['k_a'] = p['rw_k_a'][l].reshape(1, -1)
    wts['r_k'] = p['rw_r_k'][l].reshape(1, -1)
    wts['lnx_g'] = p['rw_lnx_g'][l].reshape(1, -1)
    wts['lnx_b'] = p['rw_lnx_b'][l].reshape(1, -1)
    head = jnp.arange(RW_WIDTH) // RW_HEAD
    e = (head[:, None] == jnp.arange(RW_HEADS)[None, :]).astype(BF16)
    wts['head_e'] = e
    wts['head_et'] = e.T
    wts['s5'] = _s5_block_operators(*_s5_matrices(
        p['s5_B_re'][l], p['s5_B_im'][l], p['s5_A_re'][l], p['s5_A_im'][l],
        p['s5_log_dt'][l], p['s5_C_re'][l], p['s5_C_im'][l]))
    wts['s5_D'] = p['s5_D'][l]
    wts['glu_w'] = p['s5_glu_w'][l].astype(BF16)
    wts['glu_b'] = p['s5_glu_b'][l]
    wts['proj_s5'] = p['proj_s5'][l].astype(BF16)
    wts['proj_rwkv'] = p['proj_rwkv'][l].astype(BF16)
    wts['gate_b'] = p['gate_b'][l]
    wts['w_out'] = p['w_out'][l].astype(BF16)
    wts['ln1_g'] = p['ln1_g'][l].reshape(1, -1)
    wts['ln1_b'] = p['ln1_b'][l].reshape(1, -1)
    router = jnp.concatenate([p['router_coarse'][l], p['router_fine'][l]], axis=1)
    router = jnp.pad(router, ((0, 0), (0, ROUTER_PAD - router.shape[1])))
    wts['router_hi'], wts['router_lo'] = _split(router)
    wts['router_coarse_b'] = p['router_coarse_b'][l]
    wts['router_fine_b'] = p['router_fine_b'][l]
    wts['exp_w_gate'] = p['exp_w_gate'][l].astype(BF16)
    wts['exp_w_up'] = p['exp_w_up'][l].astype(BF16)
    wts['exp_w_down'] = p['exp_w_down'][l].astype(BF16)
    wts['ln2_g'] = p['ln2_g'][l].reshape(1, -1)
    wts['ln2_b'] = p['ln2_b'][l].reshape(1, -1)
    return wts


def _encode(x, p, wts):
    bsz, t, d = x.shape
    t_pad = t + SEQ_TAIL
    n = bsz * t_pad
    h0, h0b = _ln_in(x, p['meta'], p['ln_in_g'], p['ln_in_b'])
    h0 = h0.reshape(n, d)
    h0b = h0b.reshape(n, d)
    bm = _row_block(t_pad, 1024)
    u = _mm_slabs(h0b, wts['w_u'], bm, "proj_s5_in")
    rw = _mm(h0b, wts['w_rw'], bm, RW_COLS // 4, F32, "proj_rwkv_in")
    slabs = u.shape[0]
    y_ssm = _s5_ssm(u.reshape(slabs, bsz, t_pad, 128), wts['s5']).reshape(slabs, n, 128)
    s5_out = _s5_post(y_ssm, u, wts['s5_D'], wts['glu_w'], wts['glu_b'], bm)
    r, kk, v, g, bv, lw, kd, bb = _rwkv_prep(rw.reshape(bsz, t_pad, RW_COLS), wts, t)
    yf, yb = _rwkv_scan(r, kk, v, lw, kd, bb)
    rw_out = _rwkv_post(yf.reshape(n, -1), yb.reshape(n, -1), bv.reshape(n, -1), g.reshape(n, -1), wts, bm)
    merged = _merge(h0b, s5_out, rw_out, wts, bm)
    h1, h1p, logits = _out_proj(merged, h0, wts, _row_block(t_pad, 512))
    seq_pos = jnp.arange(n, dtype=jnp.int32) % t_pad
    valid = (seq_pos < t) | (seq_pos >= t_pad - N_META)
    row_tok, pos, w, blk_exp, n_used, n_blocks = _route(logits, wts, valid)
    eo = _moe_experts(h1p, row_tok, blk_exp, n_used, n_blocks, wts)
    return _moe_combine(eo, pos, w, h1, wts, bsz, t, t_pad)


def kernel(x_prompt, x_sample, meta, ln_in_g, ln_in_b, w_in, shift_mu, s5_B_re, s5_B_im, s5_A_re, s5_A_im, s5_log_dt, s5_C_re, s5_C_im, s5_D, s5_glu_w, s5_glu_b, rw_w0, rw_w2, rw_a0, rw_a2, rw_g2, rw_k_k, rw_k_a, rw_r_k, rw_lnx_g, rw_lnx_b, proj_s5, proj_rwkv, gate_b, w_out, ln1_g, ln1_b, router_coarse, router_coarse_b, router_fine, router_fine_b, exp_w_gate, exp_w_up, exp_w_down, ln2_g, ln2_b):
    p = {
        'meta': meta, 'ln_in_g': ln_in_g, 'ln_in_b': ln_in_b, 'w_in': w_in, 'shift_mu': shift_mu,
        's5_B_re': s5_B_re, 's5_B_im': s5_B_im, 's5_A_re': s5_A_re, 's5_A_im': s5_A_im,
        's5_log_dt': s5_log_dt, 's5_C_re': s5_C_re, 's5_C_im': s5_C_im, 's5_D': s5_D,
        's5_glu_w': s5_glu_w, 's5_glu_b': s5_glu_b,
        'rw_w0': rw_w0, 'rw_w2': rw_w2, 'rw_a0': rw_a0, 'rw_a2': rw_a2, 'rw_g2': rw_g2,
        'rw_k_k': rw_k_k, 'rw_k_a': rw_k_a, 'rw_r_k': rw_r_k, 'rw_lnx_g': rw_lnx_g, 'rw_lnx_b': rw_lnx_b,
        'proj_s5': proj_s5, 'proj_rwkv': proj_rwkv, 'gate_b': gate_b, 'w_out': w_out,
        'ln1_g': ln1_g, 'ln1_b': ln1_b,
        'router_coarse': router_coarse, 'router_coarse_b': router_coarse_b,
        'router_fine': router_fine, 'router_fine_b': router_fine_b,
        'exp_w_gate': exp_w_gate, 'exp_w_up': exp_w_up, 'exp_w_down': exp_w_down,
        'ln2_g': ln2_g, 'ln2_b': ln2_b,
    }
    wts = _prepare_weights(p)
    return (_encode(x_prompt, p, wts), _encode(x_sample, p, wts))
```

```python
import functools
import math

import jax
import jax.numpy as jnp
from jax import lax
from jax.experimental import pallas as pl
from jax.experimental.pallas import tpu as pltpu

F32 = jnp.float32
BF16 = jnp.bfloat16

D_MODEL = 2048
N_META = 16
S5_WIDTH = 1024
S5_GROUP = 16
S5_GROUPS = 64
S5_STATE = 64
S5_GB = 8
S5_PAIR_MAX_CHUNKS = 160
S5_CHUNK = 16
RW_WIDTH = 1024
RW_HEAD = 64
RW_HEADS = 16
RW_DECAY_LORA = 64
RW_ICLR_LORA = 64
RW_GATE_LORA = 160
RW_GATE_PAD = 256
RW_COLS = 3 * RW_WIDTH + 2 * RW_DECAY_LORA + 2 * RW_ICLR_LORA + RW_GATE_PAD
RW_CHUNK = 64
RW_HEADS_PER_STEP = 2
MOE_GROUPS = 4
EXPERTS_PER_GROUP = 8
N_EXPERTS = 32
D_EXPERT = 512
MOE_ROWS = 256
ROUTER_PAD = 128
DEPTH = 1
ALPHA = (2 * DEPTH) ** 0.25
LN_EPS = 1e-5
GN_EPS = 64e-5
SEQ_TAIL = 64
VMEM_LIMIT = 56 * 1024 * 1024


def _cparams(*sem):
    return pltpu.CompilerParams(dimension_semantics=sem, vmem_limit_bytes=VMEM_LIMIT)


def _row_block(t_pad, cap, mult=8):
    best = mult
    for d in range(mult, cap + 1, mult):
        if t_pad % d == 0:
            best = d
    return best


def _dot(a, b):
    return jnp.dot(a, b, preferred_element_type=F32)


def _dot_nt(a, b):
    return lax.dot_general(a, b, (((1,), (1,)), ((), ())), preferred_element_type=F32)


def _dot_tn(a, b):
    return lax.dot_general(a, b, (((0,), (0,)), ((), ())), preferred_element_type=F32)


def _split(x):
    hi = x.astype(BF16)
    lo = (x - hi.astype(F32)).astype(BF16)
    return hi, lo


def _layernorm(x, g, b):
    mu = jnp.mean(x, axis=-1, keepdims=True)
    xc = x - mu
    var = jnp.mean(xc * xc, axis=-1, keepdims=True)
    return xc * lax.rsqrt(var + LN_EPS) * g + b


def _ln_in_kernel(x_ref, m_ref, g_ref, b_ref, of_ref, ob_ref, *, n_token_blocks):
    j = pl.program_id(1)
    bx, d = x_ref.shape[1], x_ref.shape[2]

    @pl.when(j < n_token_blocks)
    def _():
        y = _layernorm(x_ref[0], g_ref[...], b_ref[...])
        of_ref[0] = y
        ob_ref[0] = y.astype(BF16)

    @pl.when(j == n_token_blocks)
    def _():
        ym = _layernorm(m_ref[...], g_ref[...], b_ref[...])
        y = jnp.concatenate([jnp.zeros((SEQ_TAIL - N_META, d), F32), ym, jnp.zeros((bx - SEQ_TAIL, d), F32)],
                            axis=0)
        of_ref[0] = y
        ob_ref[0] = y.astype(BF16)


def _ln_in(x, meta, g, b):
    bsz, t, d = x.shape
    assert t % SEQ_TAIL == 0
    t_pad = t + SEQ_TAIL
    bx = _row_block(t, 512, SEQ_TAIL)
    nxb = t // bx
    row = pl.BlockSpec((1, bx, d), lambda i, j: (i, j, 0))
    vec = pl.BlockSpec((1, d), lambda i, j: (0, 0))
    return pl.pallas_call(
        functools.partial(_ln_in_kernel, n_token_blocks=nxb),
        grid=(bsz, nxb + 1),
        in_specs=[pl.BlockSpec((1, bx, d), lambda i, j: (i, jnp.minimum(j, nxb - 1), 0)),
                  pl.BlockSpec((N_META, d), lambda i, j: (0, 0)), vec, vec],
        out_specs=[row, row],
        out_shape=[jax.ShapeDtypeStruct((bsz, t_pad, d), F32), jax.ShapeDtypeStruct((bsz, t_pad, d), BF16)],
        compiler_params=_cparams("parallel", "parallel"),
        name="ln_in",
    )(x, meta, g.reshape(1, d), b.reshape(1, d))


def _mm_kernel(x_ref, w_ref, o_ref):
    o_ref[...] = _dot(x_ref[...], w_ref[...]).astype(o_ref.dtype)


def _mm(x, w, bm, bn, out_dtype, name):
    n, k = x.shape
    m = w.shape[1]
    return pl.pallas_call(
        _mm_kernel,
        grid=(m // bn, n // bm),
        in_specs=[pl.BlockSpec((bm, k), lambda j, i: (i, 0)),
                  pl.BlockSpec((k, bn), lambda j, i: (0, j))],
        out_specs=pl.BlockSpec((bm, bn), lambda j, i: (i, j)),
        out_shape=jax.ShapeDtypeStruct((n, m), out_dtype),
        compiler_params=_cparams("parallel", "parallel"),
        name=name,
    )(x, w)


def _mm_slab_kernel(x_ref, w_ref, o_ref):
    res = _dot(x_ref[...], w_ref[...])
    for g in range(o_ref.shape[0]):
        o_ref[g] = res[:, g * 128:(g + 1) * 128]


def _mm_slabs(x, w, bm, name):
    n, k = x.shape
    m = w.shape[1]
    return pl.pallas_call(
        _mm_slab_kernel,
        grid=(n // bm,),
        in_specs=[pl.BlockSpec((bm, k), lambda i: (i, 0)), pl.BlockSpec((k, m), lambda i: (0, 0))],
        out_specs=pl.BlockSpec((m // 128, bm, 128), lambda i: (0, i, 0)),
        out_shape=jax.ShapeDtypeStruct((m // 128, n, 128), F32),
        compiler_params=_cparams("parallel"),
        name=name,
    )(x, w)


def _s5_matrices(b_re, b_im, a_re, a_im, log_dt, c_re, c_im):
    L = S5_CHUNK
    dt = jnp.exp(log_dt)[..., None]
    mag = jnp.exp(a_re * dt)
    abr = mag * jnp.cos(a_im * dt)
    abi = mag * jnp.sin(a_im * dt)
    den = a_re * a_re + a_im * a_im
    nr = abr - 1.0
    cr = (nr * a_re + abi * a_im) / den
    ci = (abi * a_re - nr * a_im) / den
    bbr = cr[..., None] * b_re - ci[..., None] * b_im
    bbi = cr[..., None] * b_im + ci[..., None] * b_re
    tau = jnp.arange(L + 1, dtype=F32)[:, None, None, None]
    pmag = jnp.exp(tau * a_re * dt)
    pr = pmag * jnp.cos(tau * a_im * dt)
    pi = pmag * jnp.sin(tau * a_im * dt)
    wr = pr[..., None] * bbr - pi[..., None] * bbi
    wi = pr[..., None] * bbi + pi[..., None] * bbr
    kern = (jnp.einsum('zgop,tzgpi->tzgoi', c_re, wr)
            - jnp.einsum('zgop,tzgpi->tzgoi', c_im, wi))
    s = jnp.arange(L)[:, None]
    t = jnp.arange(L)[None, :]
    lag = t - s
    kf = jnp.where((lag >= 0)[..., None, None, None], kern[jnp.clip(lag, 0, L), 0], 0.0)
    kb = jnp.where((lag <= 0)[..., None, None, None], kern[jnp.clip(-lag, 0, L), 1], 0.0)
    toep = (kf + kb).transpose(2, 0, 4, 1, 3).reshape(S5_GROUPS, L * S5_GROUP, L * S5_GROUP)
    wf_r, wf_i = wr[::-1][1:, 0], wi[::-1][1:, 0]
    wb_r, wb_i = wr[:L, 1], wi[:L, 1]
    bmat = jnp.concatenate([wf_r, wb_r, wf_i, wb_i], axis=2)
    bmat = bmat.transpose(1, 0, 3, 2).reshape(S5_GROUPS, L * S5_GROUP, 4 * S5_STATE)
    pf_r, pf_i = pr[1:, 0], pi[1:, 0]
    pb_r, pb_i = pr[::-1][:L, 1], pi[::-1][:L, 1]
    c0r, c0i, c1r, c1i = c_re[0], c_im[0], c_re[1], c_im[1]

    def cpow(cre, cim, p_r, p_i):
        re = cre[None] * p_r[:, :, None, :] - cim[None] * p_i[:, :, None, :]
        im = cre[None] * p_i[:, :, None, :] + cim[None] * p_r[:, :, None, :]
        return re, -im

    f_re, f_im = cpow(c0r, c0i, pf_r, pf_i)
    g_re, g_im = cpow(c1r, c1i, pb_r, pb_i)
    cmat = jnp.concatenate([f_re, g_re, f_im, g_im], axis=3)
    cmat = cmat.transpose(1, 3, 0, 2).reshape(S5_GROUPS, 4 * S5_STATE, L * S5_GROUP)
    lam_re = jnp.concatenate([pr[L, 0], pr[L, 1]], axis=-1)[:, None, :]
    lam_im = jnp.concatenate([pi[L, 0], pi[L, 1]], axis=-1)[:, None, :]
    return bmat, toep, cmat, lam_re, lam_im


def _s5_expand_kernel(src_ref, e_ref, o_ref, *, pieces):
    for j in range(src_ref.shape[0]):
        x = _dot(src_ref[j].astype(BF16), e_ref[j]).astype(BF16)
        for src, rows, dst, step in pieces:
            o_ref[0, dst + j * step:dst + j * step + rows, :] = x[src:src + rows, :]


def _s5_expand(per_group, expand, pieces):
    groups, rows, cols = per_group.shape
    gb, _, wide = expand.shape
    return pl.pallas_call(
        functools.partial(_s5_expand_kernel, pieces=pieces),
        grid=(groups // gb,),
        in_specs=[pl.BlockSpec((gb, rows, cols), lambda g: (g, 0, 0)),
                  pl.BlockSpec((gb, cols, wide), lambda g: (0, 0, 0))],
        out_specs=pl.BlockSpec((1, gb * rows, wide), lambda g: (g, 0, 0)),
        out_shape=jax.ShapeDtypeStruct((groups // gb, gb * rows, wide), BF16),
        compiler_params=_cparams("parallel"),
        name="s5_expand",
    )(per_group, expand)


def _s5_block_operators(bmat, toep, cmat, lam_re, lam_im):
    nb, gb, L, c, p = S5_GROUPS // S5_GB, S5_GB, S5_CHUNK, S5_GROUP, S5_STATE
    quarters = (0, 2, 1, 3)
    wide = L * gb * c
    r = jnp.arange(L * c)[None, :, None]
    col = jnp.arange(wide)[None, None, :]
    j = jnp.arange(gb)[:, None, None]
    e_tok = ((r // c == col // (gb * c)) & ((col // c) % gb == j) & (r % c == col % c)).astype(BF16)
    k_of_col = jnp.array(quarters)[col // (gb * p)]
    e_state = ((r // p == k_of_col) & ((col // p) % gb == j) & (r % p == col % p)).astype(BF16)
    tok_rows = tuple((s * c, c, s * gb * c, c) for s in range(L))
    state_rows = tuple((quarters[k] * p, p, k * gb * p, p) for k in range(4))
    wb = _s5_expand(bmat, e_state, tok_rows)
    wt = _s5_expand(toep, e_tok, tok_rows)
    wc = _s5_expand(cmat, e_tok, state_rows)
    ar_f = lam_re[:, 0, :p].reshape(nb, gb * p)
    ar_b = lam_re[:, 0, p:].reshape(nb, gb * p)
    ai_f = lam_im[:, 0, :p].reshape(nb, gb * p)
    ai_b = lam_im[:, 0, p:].reshape(nb, gb * p)
    a1 = jnp.stack([ar_f, ar_f, ar_b, ar_b] * 2, axis=1)
    a2 = jnp.stack([-ai_f, ai_f, -ai_b, ai_b] * 2, axis=1)
    return wb, wt, wc, a1, a2


def _s5_kernel(u_ref, wb_ref, wt_ref, wc_ref, a1_ref, a2_ref, y_ref, s_scr, xf_scr, xb_scr, *, n_chunks):
    L = S5_CHUNK
    C = n_chunks
    nseq = u_ref.shape[1]
    cp = -(-C // 8) * 8
    sw = S5_GB * S5_STATE
    tail = SEQ_TAIL // L

    def rows_of(parts):
        if cp > C:
            pad = jnp.zeros((cp - C, parts[0].shape[1]), F32)
            parts = [x for part in parts for x in (part, pad)]
        return jnp.concatenate(parts, axis=0).astype(BF16)

    u8 = rows_of([jnp.concatenate([u_ref[0, q, pl.ds(s, C, stride=L), :] for s in range(L)], axis=1)
                  for q in range(nseq)])
    s_all = _dot(u8, wb_ref[0])
    for q in range(nseq):
        for k in range(4):
            s_scr[:, 4 * q + k, :] = s_all[q * cp:q * cp + C, k * sw:(k + 1) * sw]
    if nseq == 1:
        s_scr[:, 4:8, :] = jnp.zeros((C, 4, sw), F32)
    a1 = a1_ref[0]
    a2 = a2_ref[0]
    row = lax.broadcasted_iota(jnp.int32, (8, sw), 0)
    even = (row % 2) == 0
    is_fwd = (row % 4) < 2

    def step(c, x):
        cf = (c + C - tail) % C
        cb = (2 * C - 1 - c - tail) % C
        xf_scr[cf] = x
        xb_scr[cb] = x
        s = jnp.where(is_fwd, s_scr[cf], s_scr[cb])
        swapped = jnp.where(even, pltpu.roll(x, 7, 0), pltpu.roll(x, 1, 0))
        return a1 * x + a2 * swapped + s

    lax.fori_loop(0, C, step, jnp.zeros((8, sw), F32))
    x_in = rows_of([jnp.concatenate([xf_scr[:, 4 * q, :], xf_scr[:, 4 * q + 1, :],
                                     xb_scr[:, 4 * q + 2, :], xb_scr[:, 4 * q + 3, :]], axis=1)
                    for q in range(nseq)])
    y8 = _dot(u8, wt_ref[0]) + _dot(x_in, wc_ref[0])
    for q in range(nseq):
        for t in range(L):
            y_ref[0, q, pl.ds(t, C, stride=L), :] = y8[q * cp:q * cp + C, t * 128:(t + 1) * 128]


def _s5_ssm(u3, ops):
    wb, wt, wc, a1, a2 = ops
    _, bsz, t_pad, _ = u3.shape
    n_chunks = t_pad // S5_CHUNK
    nb = S5_GROUPS // S5_GB
    lanes = S5_GB * S5_GROUP
    sw = S5_GB * S5_STATE
    nseq = 2 if (bsz % 2 == 0 and n_chunks <= S5_PAIR_MAX_CHUNKS) else 1
    blk = pl.BlockSpec((1, nseq, t_pad, lanes), lambda g, b: (g, b, 0, 0))
    mat = pl.BlockSpec((1,) + wb.shape[1:], lambda g, b: (g, 0, 0), pipeline_mode=pl.Buffered(1))
    vec = pl.BlockSpec((1, 8, sw), lambda g, b: (g, 0, 0))
    return pl.pallas_call(
        functools.partial(_s5_kernel, n_chunks=n_chunks),
        grid=(nb, bsz // nseq),
        in_specs=[blk, mat, mat, mat, vec, vec],
        out_specs=blk,
        out_shape=jax.ShapeDtypeStruct(u3.shape, F32),
        scratch_shapes=[pltpu.VMEM((n_chunks, 8, sw), F32)] * 3,
        compiler_params=_cparams("arbitrary", "arbitrary"),
        name="s5_ssm",
    )(u3, wb, wt, wc, a1, a2)


def _s5_post_kernel(y_ref, u_ref, d_ref, w_ref, b_ref, o_ref):
    slabs = range(y_ref.shape[0])
    y = (jnp.concatenate([y_ref[g] for g in slabs], axis=1)
         + jnp.concatenate([u_ref[g] for g in slabs], axis=1) * d_ref[...])
    act = y * (0.5 * (1.0 + jnp.tanh(math.sqrt(2.0 / math.pi) * (y + 0.044715 * (y * y * y)))))
    z = _dot(act.astype(BF16), w_ref[...]) + b_ref[...]
    o_ref[...] = (act * jax.nn.sigmoid(z)).astype(o_ref.dtype)


def _s5_post(y, u, d_skip, glu_w, glu_b, bm):
    slabs, n, lanes = y.shape
    row = pl.BlockSpec((bm, S5_WIDTH), lambda i: (i, 0))
    slab = pl.BlockSpec((slabs, bm, lanes), lambda i: (0, i, 0))
    vec = pl.BlockSpec((1, S5_WIDTH), lambda i: (0, 0))
    return pl.pallas_call(
        _s5_post_kernel,
        grid=(n // bm,),
        in_specs=[slab, slab, vec, pl.BlockSpec((S5_WIDTH, S5_WIDTH), lambda i: (0, 0)), vec],
        out_specs=row,
        out_shape=jax.ShapeDtypeStruct((n, S5_WIDTH), BF16),
        compiler_params=_cparams("parallel"),
        name="s5_post",
    )(y, u, d_skip.reshape(1, -1), glu_w, glu_b.reshape(1, -1))


def _head_sum(x, e_ref, et_ref):
    hi, lo = _split(x)
    s = _dot(hi, e_ref[...]) + _dot(lo, e_ref[...])
    shi, slo = _split(s)
    return _dot(shi, et_ref[...]) + _dot(slo, et_ref[...])


def _rwkv_prep_kernel(cur_ref, prev_ref, next_ref, mu_ref, w2_ref, a2_ref, g2_ref, w0_ref, a0_ref,
                      kk_ref, ka_ref, rk_ref, e_ref, et_ref,
                      r_o, kk_o, v_o, g_o, bv_o, lw_o, kd_o, bb_o, *, t, t_pad, bm):
    j = pl.program_id(1)
    p = cur_ref[0].astype(F32)
    halo = prev_ref.shape[1]
    row = lax.broadcasted_iota(jnp.int32, (bm, 1), 0)
    prev_row = prev_ref[0, halo - 1:halo, :].astype(F32)
    next_row = next_ref[0, 0:1, :].astype(F32)
    prev = jnp.where(row == 0, prev_row, pltpu.roll(p, 1, 0))
    nxt = jnp.where(row == bm - 1, next_row, pltpu.roll(p, bm - 1, 0))
    xs = p + mu_ref[0:1, :] * (prev - p) + mu_ref[1:2, :] * (nxt - p)
    w = RW_WIDTH
    r = xs[:, 0:w]
    k = xs[:, w:2 * w]
    v = xs[:, 2 * w:3 * w]
    lw = xs[:, 3 * w:3 * w + 128]
    la = xs[:, 3 * w + 128:3 * w + 256]
    lg = xs[:, 3 * w + 256:]
    w_log = _dot(jnp.tanh(lw).astype(BF16), w2_ref[...])
    a_lin = _dot(la.astype(BF16), a2_ref[...])
    g = _dot(jax.nn.sigmoid(lg).astype(BF16), g2_ref[...])
    kk = k * kk_ref[...]
    n2 = _head_sum(kk * kk, e_ref, et_ref)
    kk = kk / jnp.maximum(jnp.sqrt(n2), 1e-12)
    pos = j * bm + row
    valid = (pos < t) | (pos >= t_pad - N_META)
    v = jnp.where(valid, v, 0.0)
    kd_sum = jnp.zeros_like(k)
    for z in range(2):
        wl = w_log[:, z * w:(z + 1) * w] + w0_ref[z:z + 1, :]
        lw_o[z, 0] = -math.exp(-0.5) * jax.nn.sigmoid(wl)
        a = jax.nn.sigmoid(a_lin[:, z * w:(z + 1) * w] + a0_ref[z:z + 1, :])
        kd = k * (1.0 + (a - 1.0) * ka_ref[...])
        kd_o[z, 0] = kd.astype(BF16)
        bb_o[z, 0] = (kk * a).astype(BF16)
        kd_sum = kd_sum + kd
    bonus = _head_sum(r * kd_sum * rk_ref[...], e_ref, et_ref)
    r_o[0] = r.astype(BF16)
    kk_o[0] = kk.astype(BF16)
    v_o[0] = v.astype(BF16)
    g_o[0] = g.astype(BF16)
    bv_o[0] = (bonus * v).astype(BF16)


def _rwkv_prep(rw3, wts, t):
    bsz, t_pad, _ = rw3.shape
    bm = _row_block(t_pad, 320, 64)
    halo = 16
    nb8 = bm // halo
    n8 = t_pad // halo
    w = RW_WIDTH
    cur = pl.BlockSpec((1, bm, RW_COLS), lambda b, j: (b, j, 0))
    prev = pl.BlockSpec((1, halo, RW_COLS), lambda b, j: (b, (j * nb8 + n8 - 1) % n8, 0))
    nxt = pl.BlockSpec((1, halo, RW_COLS), lambda b, j: (b, ((j + 1) * nb8) % n8, 0))

    def full(a):
        return pl.BlockSpec(a.shape, lambda b, j: (0,) * a.ndim)

    shared = pl.BlockSpec((1, bm, w), lambda b, j: (b, j, 0))
    per_dir = pl.BlockSpec((2, 1, bm, w), lambda b, j: (0, b, j, 0))
    consts = [wts['mu'], wts['w2'], wts['a2'], wts['g2'], wts['w0'], wts['a0'],
              wts['k_k'], wts['k_a'], wts['r_k'], wts['head_e'], wts['head_et']]
    sds = jax.ShapeDtypeStruct
    return pl.pallas_call(
        functools.partial(_rwkv_prep_kernel, t=t, t_pad=t_pad, bm=bm),
        grid=(bsz, t_pad // bm),
        in_specs=[cur, prev, nxt] + [full(a) for a in consts],
        out_specs=[shared] * 5 + [per_dir] * 3,
        out_shape=[sds((bsz, t_pad, w), BF16)] * 5
        + [sds((2, bsz, t_pad, w), F32), sds((2, bsz, t_pad, w), BF16), sds((2, bsz, t_pad, w), BF16)],
        compiler_params=_cparams("parallel", "parallel"),
        name="rwkv_prep",
    )(rw3, rw3, rw3, *consts)


def _rwkv_chunk(fwd, r_ref, kk_ref, v_ref, lw_ref, kd_ref, bb_ref, y_ref, st_ref, z):
    L = RW_CHUNK
    hd = RW_HEAD
    gw = RW_HEADS_PER_STEP * hd
    n_groups = RW_WIDTH // gw
    row = lax.broadcasted_iota(jnp.int32, (L, L), 0)
    col = lax.broadcasted_iota(jnp.int32, (L, L), 1)
    tri = jnp.where((col <= row) if fwd else (col >= row), 1.0, 0.0).astype(BF16)
    grow = lax.broadcasted_iota(jnp.int32, (L, gw), 0)
    gcol = lax.broadcasted_iota(jnp.int32, (L, gw), 1) % L
    incl = (gcol <= grow) if fwd else (gcol >= grow)
    strict = (gcol < grow) if fwd else (gcol > grow)
    bd_mask = jnp.where(lax.broadcasted_iota(jnp.int32, (gw, gw), 0) // hd
                        == lax.broadcasted_iota(jnp.int32, (gw, gw), 1) // hd, 1.0, 0.0).astype(BF16)

    def bd(x):
        return jnp.concatenate([x] * RW_HEADS_PER_STEP, axis=0) * bd_mask

    def stack(x):
        return jnp.concatenate([x[:, h * hd:(h + 1) * hd] for h in range(RW_HEADS_PER_STEP)], axis=0)

    lw = lw_ref[0, 0]
    lw_hi, lw_lo = _split(lw)
    c = _dot(tri, lw_hi) + _dot(tri, lw_lo)
    e = c - lw
    c_tot = c[L - 1:L, :] if fwd else c[0:1, :]
    r = r_ref[0].astype(F32)
    kk = kk_ref[0].astype(F32)
    kd = kd_ref[0, 0].astype(F32)
    bb = bb_ref[0, 0].astype(F32)
    v = v_ref[0]
    q1 = (kk * jnp.exp(e)).astype(BF16)
    q2 = (r * jnp.exp(c)).astype(BF16)
    inv = jnp.exp(-c)
    k1 = (kd * inv).astype(BF16)
    k2 = (bb * inv).astype(BF16)
    rest = jnp.exp(c_tot - c)
    k1p = (kd * rest).astype(BF16)
    k2p = (bb * rest).astype(BF16)
    dec_tot = jnp.exp(c_tot)
    def group_chain(g):
        sl = slice(g * gw, (g + 1) * gw)
        s0 = st_ref[z, g]
        s0_hi, s0_lo = _split(s0)
        lhs = jnp.concatenate([q1[:, sl], q2[:, sl]], axis=0)
        rhs = jnp.concatenate([bd(k1[:, sl]), bd(k2[:, sl]), bd(s0_hi), bd(s0_lo)], axis=0)
        m1 = _dot_nt(lhs, rhs)
        yield
        a_kd = jnp.where(strict, m1[:L, 0:gw], 0.0)
        a_b = jnp.where(strict, m1[:L, gw:2 * gw], 0.0)
        q1s = m1[:L, 2 * gw:3 * gw] + m1[:L, 3 * gw:]
        b_kd = jnp.where(incl, m1[L:, 0:gw], 0.0)
        b_b = jnp.where(incl, m1[L:, gw:2 * gw], 0.0)
        q2s = m1[L:, 2 * gw:3 * gw] + m1[L:, 3 * gw:]
        vg = v[:, sl]
        v_bd = bd(vg)
        x = q1s + _dot(a_kd.astype(BF16), v_bd)
        yield
        m = -a_b
        levels = L.bit_length() - 1
        for lvl in range(levels):
            mb = m.astype(BF16)
            if lvl < levels - 1:
                rr = _dot(mb, jnp.concatenate([bd(x.astype(BF16)), bd(mb)], axis=1))
                x = x + rr[:, :gw]
                m = rr[:, gw:]
            else:
                x = x + _dot(mb, bd(x.astype(BF16)))
            yield
        ub = x.astype(BF16)
        y = q2s + _dot(jnp.concatenate([b_kd, -b_b], axis=1).astype(BF16),
                       jnp.concatenate([v_bd, bd(ub)], axis=0))
        y_ref[0, :, sl] = y.astype(y_ref.dtype)
        yield
        st_ref[z, g] = s0 * dec_tot[:, sl] + _dot_tn(
            jnp.concatenate([stack(vg), stack(ub)], axis=0),
            jnp.concatenate([bd(k1p[:, sl]), -bd(k2p[:, sl])], axis=0))

    return [group_chain(g) for g in range(n_groups)]


def _rwkv_scan_kernel(rf, kkf, vf, lwf, kdf, bbf, rb, kkb, vb, lwb, kdb, bbb, yf_ref, yb_ref, st_ref):
    @pl.when(pl.program_id(1) == 0)
    def _():
        st_ref[...] = jnp.zeros_like(st_ref)

    chains = (_rwkv_chunk(True, rf, kkf, vf, lwf, kdf, bbf, yf_ref, st_ref, 0)
              + _rwkv_chunk(False, rb, kkb, vb, lwb, kdb, bbb, yb_ref, st_ref, 1))
    while chains:
        alive = []
        for chain in chains:
            try:
                next(chain)
                alive.append(chain)
            except StopIteration:
                pass
        chains = alive


def _rwkv_scan(r, kk, v, lw, kd, bb):
    bsz, t_pad, w = r.shape
    L = RW_CHUNK
    nc = t_pad // L
    gw = RW_HEADS_PER_STEP * RW_HEAD
    grid = (bsz, nc)

    def block(fwd, j):
        logical = j if fwd else nc - 1 - j
        return (logical + nc - 1) % nc

    def shared(fwd):
        return pl.BlockSpec((1, L, w), lambda b, j: (b, block(fwd, j), 0))

    def per_dir(fwd):
        return pl.BlockSpec((1, 1, L, w), lambda b, j: (0 if fwd else 1, b, block(fwd, j), 0))

    in_specs = []
    for fwd in (True, False):
        in_specs += [shared(fwd), shared(fwd), shared(fwd), per_dir(fwd), per_dir(fwd), per_dir(fwd)]
    return pl.pallas_call(
        _rwkv_scan_kernel,
        grid=grid,
        in_specs=in_specs,
        out_specs=[shared(True), shared(False)],
        out_shape=[jax.ShapeDtypeStruct((bsz, t_pad, w), BF16)] * 2,
        scratch_shapes=[pltpu.VMEM((2, w // gw, RW_HEAD, gw), F32)],
        compiler_params=_cparams("parallel", "arbitrary"),
        name="rwkv_scan",
    )(r, kk, v, lw, kd, bb, r, kk, v, lw, kd, bb)


def _rwkv_post_kernel(yf_ref, yb_ref, bv_ref, g_ref, lg_ref, lb_ref, e_ref, et_ref, o_ref):
    y = yf_ref[...].astype(F32) + yb_ref[...].astype(F32)
    mean = _head_sum(y, e_ref, et_ref) * (1.0 / RW_HEAD)
    yc = y - mean
    var = _head_sum(yc * yc, e_ref, et_ref) * (1.0 / RW_HEAD)
    y = yc * lax.rsqrt(var + GN_EPS) * lg_ref[...] + lb_ref[...]
    o_ref[...] = ((y + bv_ref[...].astype(F32)) * g_ref[...].astype(F32)).astype(o_ref.dtype)


def _rwkv_post(yf, yb, bv, g, wts, bm):
    n, w = yf.shape
    row = pl.BlockSpec((bm, w), lambda i: (i, 0))
    vec = pl.BlockSpec((1, w), lambda i: (0, 0))
    e, et = wts['head_e'], wts['head_et']
    return pl.pallas_call(
        _rwkv_post_kernel,
        grid=(n // bm,),
        in_specs=[row, row, row, row, vec, vec,
                  pl.BlockSpec(e.shape, lambda i: (0, 0)), pl.BlockSpec(et.shape, lambda i: (0, 0))],
        out_specs=row,
        out_shape=jax.ShapeDtypeStruct((n, w), BF16),
        compiler_params=_cparams("parallel"),
        name="rwkv_post",
    )(yf, yb, bv, g, wts['lnx_g'], wts['lnx_b'], e, et)


def _merge_kernel(h_ref, s5_ref, rw_ref, wg0_ref, wg1_ref, gb_ref, p0_ref, p1_ref, o_ref):
    h = h_ref[...]
    g0 = jax.nn.sigmoid(_dot(h, wg0_ref[...]) + gb_ref[0:1, :])
    g1 = jax.nn.sigmoid(_dot(h, wg1_ref[...]) + gb_ref[1:2, :])
    merged = g0 * _dot(s5_ref[...], p0_ref[...]) + g1 * _dot(rw_ref[...], p1_ref[...])
    o_ref[...] = merged.astype(o_ref.dtype)


def _merge(h0b, s5_out, rw_out, wts, bm):
    n, d = h0b.shape
    bn = 1024
    nj = d // bn
    return pl.pallas_call(
        _merge_kernel,
        grid=(nj, n // bm),
        in_specs=[pl.BlockSpec((bm, d), lambda j, i: (i, 0)),
                  pl.BlockSpec((bm, S5_WIDTH), lambda j, i: (i, 0)),
                  pl.BlockSpec((bm, RW_WIDTH), lambda j, i: (i, 0)),
                  pl.BlockSpec((d, bn), lambda j, i: (0, j)),
                  pl.BlockSpec((d, bn), lambda j, i: (0, nj + j)),
                  pl.BlockSpec((2, bn), lambda j, i: (0, j)),
                  pl.BlockSpec((S5_WIDTH, bn), lambda j, i: (0, j)),
                  pl.BlockSpec((RW_WIDTH, bn), lambda j, i: (0, j))],
        out_specs=pl.BlockSpec((bm, bn), lambda j, i: (i, j)),
        out_shape=jax.ShapeDtypeStruct((n, d), BF16),
        compiler_params=_cparams("parallel", "parallel"),
        name="merge",
    )(h0b, s5_out, rw_out, wts['w_gate'], wts['w_gate'], wts['gate_b'], wts['proj_s5'], wts['proj_rwkv'])


def _pack_bf16_pairs(x):
    half = x.shape[1] // 2
    bits = lax.bitcast_convert_type(x.astype(BF16).astype(F32), jnp.uint32)
    return bits[:, half:] | (bits[:, :half] >> 16)


def _unpack_bf16_pairs(p, dtype):
    lo = lax.bitcast_convert_type(p << 16, F32)
    hi = lax.bitcast_convert_type(p & jnp.uint32(0xFFFF0000), F32)
    return jnp.concatenate([lo, hi], axis=1).astype(dtype)


def _out_kernel(m_ref, h_ref, w_ref, g_ref, b_ref, rh_ref, rl_ref, o_ref, op_ref, lg_ref):
    bm = m_ref.shape[0]
    halves = [slice(0, bm // 2), slice(bm // 2, bm)]
    xs = [ALPHA * h_ref[rows, :] + _dot(m_ref[rows, :], w_ref[...]) for rows in halves]
    for rows, x in zip(halves, xs):
        h1 = _layernorm(x, g_ref[...], b_ref[...])
        o_ref[rows, :] = h1
        op_ref[rows, :] = _pack_bf16_pairs(h1)
        hi, lo = _split(h1)
        lg_ref[rows, :] = _dot(hi, rh_ref[...]) + _dot(lo, rh_ref[...]) + _dot(hi, rl_ref[...])


def _out_proj(merged, h0, wts, bm):
    n, d = h0.shape
    row = pl.BlockSpec((bm, d), lambda i: (i, 0))
    vec = pl.BlockSpec((1, d), lambda i: (0, 0))
    rt = pl.BlockSpec((d, ROUTER_PAD), lambda i: (0, 0))
    return pl.pallas_call(
        _out_kernel,
        grid=(n // bm,),
        in_specs=[row, row, pl.BlockSpec((d, d), lambda i: (0, 0)), vec, vec, rt, rt],
        out_specs=[row, pl.BlockSpec((bm, d // 2), lambda i: (i, 0)),
                   pl.BlockSpec((bm, ROUTER_PAD), lambda i: (i, 0))],
        out_shape=[jax.ShapeDtypeStruct((n, d), F32), jax.ShapeDtypeStruct((n, d // 2), jnp.uint32),
                   jax.ShapeDtypeStruct((n, ROUTER_PAD), F32)],
        compiler_params=_cparams("parallel"),
        name="out_proj",
    )(merged, h0, wts['w_out'], wts['ln1_g'], wts['ln1_b'], wts['router_hi'], wts['router_lo'])


def _route(logits, wts, valid):
    i32 = jnp.int32
    lc = logits[:, :MOE_GROUPS] + wts['router_coarse_b']
    grp = jnp.argmax(lc, axis=-1).astype(i32)
    gate_c = jnp.max(jax.nn.softmax(lc, axis=-1), axis=-1)
    lf = (logits[:, MOE_GROUPS:MOE_GROUPS + N_EXPERTS] + wts['router_fine_b'])
    lf = lf.reshape(-1, MOE_GROUPS, EXPERTS_PER_GROUP)
    sel = grp[:, None, None] == jnp.arange(MOE_GROUPS, dtype=i32)[None, :, None]
    lf = jnp.sum(jnp.where(sel, lf, 0.0), axis=1)
    lane = jnp.arange(EXPERTS_PER_GROUP, dtype=i32)[None, :]
    i1 = jnp.argmax(lf, axis=-1).astype(i32)
    v1 = jnp.max(lf, axis=-1)
    rest = jnp.where(lane == i1[:, None], -jnp.inf, lf)
    i2 = jnp.argmax(rest, axis=-1).astype(i32)
    v2 = jnp.max(rest, axis=-1)
    top_v = jnp.stack([v1, v2], axis=-1)
    top_i = jnp.stack([i1, i2], axis=-1)
    w = gate_c[:, None] * jax.nn.softmax(top_v, axis=-1)
    expert = grp[:, None] * EXPERTS_PER_GROUP + top_i
    expert = jnp.where(valid[:, None], expert, N_EXPERTS)
    w = jnp.where(valid[:, None], w, 0.0)
    n_tok = logits.shape[0]
    n_asg = 2 * n_tok
    e_flat = expert.reshape(-1)
    order = jnp.argsort(e_flat).astype(i32)
    inv = jnp.argsort(order).astype(i32)
    bounds = jnp.sum(e_flat[None, :] < jnp.arange(N_EXPERTS + 1, dtype=i32)[:, None], axis=1, dtype=i32)
    start = bounds[:N_EXPERTS]
    counts = bounds[1:] - start
    padded = (counts + MOE_ROWS - 1) // MOE_ROWS * MOE_ROWS
    ex = jnp.arange(N_EXPERTS, dtype=i32)
    pend = jnp.sum(jnp.where(ex[None, :] <= ex[:, None], padded[None, :], 0), axis=1)
    pstart = pend - padded
    n_blocks = -(-n_asg // MOE_ROWS) + N_EXPERTS
    n_rows = n_blocks * MOE_ROWS
    n_used = pend[-1] // MOE_ROWS
    blk = jnp.minimum(jnp.arange(n_blocks, dtype=i32), n_used - 1)
    blk_exp = jnp.sum(pend[None, :] <= (blk * MOE_ROWS)[:, None], axis=1, dtype=i32)
    blk_exp = jnp.minimum(blk_exp, N_EXPERTS - 1)
    experts = jnp.arange(N_EXPERTS, dtype=i32)

    def lookup(table, idx):
        return jnp.sum(jnp.where(idx[..., None] == experts, table, 0), axis=-1)

    pos = jnp.where(e_flat < N_EXPERTS, lookup(pstart - start, e_flat) + inv, 0)
    d = jnp.arange(n_rows, dtype=i32).reshape(n_blocks, MOE_ROWS)
    k = d - lookup(pstart, blk_exp)[:, None]
    src = jnp.clip(k + lookup(start, blk_exp)[:, None], 0, n_asg - 1)
    row_tok = jnp.where((k < lookup(counts, blk_exp)[:, None]) & (d < pend[-1]), order[src] // 2, 0)
    return row_tok, pos, w, blk_exp, n_used.reshape(1).astype(i32), n_blocks


def _row_gather_start(src_hbm, idx_ref, n_rows, dst, sem, stride=1, offset=0):
    for r in range(n_rows):
        row = idx_ref[0, 0, stride * r + offset]
        pltpu.make_async_copy(src_hbm.at[pl.ds(row, 1)], dst.at[pl.ds(r, 1)], sem).start(priority=r % 2)


def _row_gather_wait(src_hbm, n_rows, dst, sem):
    def wait(r, carry):
        pltpu.make_async_copy(src_hbm.at[pl.ds(0, 1)], dst.at[pl.ds(r, 1)], sem).wait()
        return carry

    lax.fori_loop(0, n_rows, wait, 0, unroll=8)


def _expert_kernel(nused_ref, bexp_ref, idx_ref, nidx_ref, x_hbm, wg_ref, wu_ref, wd_ref, o_ref, buf, sem):
    i = pl.program_id(0)
    n_used = nused_ref[0]
    slot = i % 2

    @pl.when((i == 0) & (n_used > 0))
    def _():
        _row_gather_start(x_hbm, idx_ref, MOE_ROWS, buf.at[0], sem.at[0])

    @pl.when(i + 1 < n_used)
    def _():
        _row_gather_start(x_hbm, nidx_ref, MOE_ROWS, buf.at[1 - slot], sem.at[1 - slot])

    @pl.when(i < n_used)
    def _():
        _row_gather_wait(x_hbm, MOE_ROWS, buf.at[slot], sem.at[slot])
        x = _unpack_bf16_pairs(buf[slot], BF16)
        hb = jax.nn.silu(_dot(x, wg_ref[0])) * _dot(x, wu_ref[0])
        o_ref[...] = _pack_bf16_pairs(_dot(hb.astype(BF16), wd_ref[0]))

    @pl.when(i >= n_used)
    def _():
        o_ref[...] = jnp.zeros_like(o_ref)


def _moe_experts(h1p, row_tok, blk_exp, n_used, n_blocks, wts):
    d = 2 * h1p.shape[1]
    idx = row_tok.reshape(n_blocks, 1, MOE_ROWS)
    last = n_blocks - 1
    smem = pltpu.SMEM
    return pl.pallas_call(
        _expert_kernel,
        grid_spec=pltpu.PrefetchScalarGridSpec(
            num_scalar_prefetch=2,
            grid=(n_blocks,),
            in_specs=[pl.BlockSpec((1, 1, MOE_ROWS), lambda i, nu, be: (i, 0, 0), memory_space=smem),
                      pl.BlockSpec((1, 1, MOE_ROWS), lambda i, nu, be: (jnp.minimum(i + 1, last), 0, 0),
                                   memory_space=smem),
                      pl.BlockSpec(memory_space=pl.ANY),
                      pl.BlockSpec((1, d, D_EXPERT), lambda i, nu, be: (be[i], 0, 0)),
                      pl.BlockSpec((1, d, D_EXPERT), lambda i, nu, be: (be[i], 0, 0)),
                      pl.BlockSpec((1, D_EXPERT, d), lambda i, nu, be: (be[i], 0, 0))],
            out_specs=pl.BlockSpec((MOE_ROWS, d // 2), lambda i, nu, be: (i, 0)),
            scratch_shapes=[pltpu.VMEM((2, MOE_ROWS, d // 2), jnp.uint32), pltpu.SemaphoreType.DMA((2,))],
        ),
        out_shape=jax.ShapeDtypeStruct((n_blocks * MOE_ROWS, d // 2), jnp.uint32),
        compiler_params=_cparams("arbitrary"),
        name="moe_experts",
    )(n_used, blk_exp, idx, idx, h1p, wts['exp_w_gate'], wts['exp_w_up'], wts['exp_w_down'])


def _combine_kernel(pos_ref, npos_ref, eo_hbm, h_ref, w_ref, g_ref, b_ref, o_ref, buf, sem, *, n_steps, bm):
    i = pl.program_id(0)
    slot = i % 2

    def start(p_ref, s):
        for k in range(2):
            _row_gather_start(eo_hbm, p_ref, bm, buf.at[s, k], sem.at[s], stride=2, offset=k)

    @pl.when(i == 0)
    def _():
        start(pos_ref, 0)

    @pl.when(i + 1 < n_steps)
    def _():
        start(npos_ref, 1 - slot)

    for k in range(2):
        _row_gather_wait(eo_hbm, bm, buf.at[slot, k], sem.at[slot])
    w = w_ref[0]
    moe = (w[:, 0:1] * _unpack_bf16_pairs(buf[slot, 0], F32)
           + w[:, 1:2] * _unpack_bf16_pairs(buf[slot, 1], F32))
    o_ref[0] = _layernorm(ALPHA * h_ref[0] + moe, g_ref[...], b_ref[...])


def _moe_combine(eo, pos, w, h1, wts, bsz, t, t_pad):
    n, d = h1.shape
    bm = _row_block(t, 256)
    per_seq = t // bm
    n_steps = bsz * per_seq
    last = n_steps - 1
    vec = pl.BlockSpec((1, d), lambda i: (0, 0))
    smem = pltpu.SMEM
    pos3 = pos.reshape(bsz, t_pad, 2)[:, :t].reshape(n_steps, 1, 2 * bm)

    def rows(i):
        return (i // per_seq, i % per_seq, 0)

    return pl.pallas_call(
        functools.partial(_combine_kernel, n_steps=n_steps, bm=bm),
        grid=(n_steps,),
        in_specs=[pl.BlockSpec((1, 1, 2 * bm), lambda i: (i, 0, 0), memory_space=smem),
                  pl.BlockSpec((1, 1, 2 * bm), lambda i: (jnp.minimum(i + 1, last), 0, 0), memory_space=smem),
                  pl.BlockSpec(memory_space=pl.ANY),
                  pl.BlockSpec((1, bm, d), rows),
                  pl.BlockSpec((1, bm, 2), rows), vec, vec],
        out_specs=pl.BlockSpec((1, bm, d), rows),
        out_shape=jax.ShapeDtypeStruct((bsz, t, d), F32),
        scratch_shapes=[pltpu.VMEM((2, 2, bm) + eo.shape[1:], eo.dtype), pltpu.SemaphoreType.DMA((2,))],
        compiler_params=_cparams("arbitrary"),
        name="moe_combine",
    )(pos3, pos3, eo, h1.reshape(bsz, t_pad, d), w.reshape(bsz, t_pad, 2), wts['ln2_g'], wts['ln2_b'])


def _prepare_weights(p):
    l = 0
    w_in = p['w_in'][l]
    c0 = S5_WIDTH
    c1 = c0 + 3 * RW_WIDTH + 2 * RW_DECAY_LORA + 2 * RW_ICLR_LORA + RW_GATE_LORA
    gpad = RW_GATE_PAD - RW_GATE_LORA
    wts = {}
    wts['w_u'] = w_in[:, :c0].astype(BF16)
    wts['w_rw'] = jnp.pad(w_in[:, c0:c1], ((0, 0), (0, gpad))).astype(BF16)
    wts['w_gate'] = w_in[:, c1:].astype(BF16)
    wts['mu'] = jnp.pad(p['shift_mu'][l], ((0, 0), (0, gpad)))
    z = jnp.zeros((RW_DECAY_LORA, RW_WIDTH), F32)
    wts['w2'] = jnp.block([[p['rw_w2'][l, 0], z], [z, p['rw_w2'][l, 1]]]).astype(BF16)
    wts['a2'] = jnp.block([[p['rw_a2'][l, 0], z], [z, p['rw_a2'][l, 1]]]).astype(BF16)
    wts['g2'] = jnp.pad(p['rw_g2'][l], ((0, gpad), (0, 0))).astype(BF16)
    wts['w0'] = p['rw_w0'][l]
    wts['a0'] = p['rw_a0'][l]
    wts['k_k'] = p['rw_k_k'][l].reshape(1, -1)
    wts['k_a'] = p['rw_k_a'][l].reshape(1, -1)
    wts['r_k'] = p['rw_r_k'][l].reshape(1, -1)
    wts['lnx_g'] = p['rw_lnx_g'][l].reshape(1, -1)
    wts['lnx_b'] = p['rw_lnx_b'][l].reshape(1, -1)
    head = jnp.arange(RW_WIDTH) // RW_HEAD
    e = (head[:, None] == jnp.arange(RW_HEADS)[None, :]).astype(BF16)
    wts['head_e'] = e
    wts['head_et'] = e.T
    wts['s5'] = _s5_block_operators(*_s5_matrices(
        p['s5_B_re'][l], p['s5_B_im'][l], p['s5_A_re'][l], p['s5_A_im'][l],
        p['s5_log_dt'][l], p['s5_C_re'][l], p['s5_C_im'][l]))
    wts['s5_D'] = p['s5_D'][l]
    wts['glu_w'] = p['s5_glu_w'][l].astype(BF16)
    wts['glu_b'] = p['s5_glu_b'][l]
    wts['proj_s5'] = p['proj_s5'][l].astype(BF16)
    wts['proj_rwkv'] = p['proj_rwkv'][l].astype(BF16)
    wts['gate_b'] = p['gate_b'][l]
    wts['w_out'] = p['w_out'][l].astype(BF16)
    wts['ln1_g'] = p['ln1_g'][l].reshape(1, -1)
    wts['ln1_b'] = p['ln1_b'][l].reshape(1, -1)
    router = jnp.concatenate([p['router_coarse'][l], p['router_fine'][l]], axis=1)
    router = jnp.pad(router, ((0, 0), (0, ROUTER_PAD - router.shape[1])))
    wts['router_hi'], wts['router_lo'] = _split(router)
    wts['router_coarse_b'] = p['router_coarse_b'][l]
    wts['router_fine_b'] = p['router_fine_b'][l]
    wts['exp_w_gate'] = p['exp_w_gate'][l].astype(BF16)
    wts['exp_w_up'] = p['exp_w_up'][l].astype(BF16)
    wts['exp_w_down'] = p['exp_w_down'][l].astype(BF16)
    wts['ln2_g'] = p['ln2_g'][l].reshape(1, -1)
    wts['ln2_b'] = p['ln2_b'][l].reshape(1, -1)
    return wts


def _encode(x, p, wts):
    bsz, t, d = x.shape
    t_pad = t + SEQ_TAIL
    n = bsz * t_pad
    h0, h0b = _ln_in(x, p['meta'], p['ln_in_g'], p['ln_in_b'])
    h0 = h0.reshape(n, d)
    h0b = h0b.reshape(n, d)
    bm = _row_block(t_pad, 1024)
    u = _mm_slabs(h0b, wts['w_u'], bm, "proj_s5_in")
    rw = _mm(h0b, wts['w_rw'], bm, RW_COLS // 4, BF16, "proj_rwkv_in")
    slabs = u.shape[0]
    y_ssm = _s5_ssm(u.reshape(slabs, bsz, t_pad, 128), wts['s5']).reshape(slabs, n, 128)
    s5_out = _s5_post(y_ssm, u, wts['s5_D'], wts['glu_w'], wts['glu_b'], bm)
    r, kk, v, g, bv, lw, kd, bb = _rwkv_prep(rw.reshape(bsz, t_pad, RW_COLS), wts, t)
    yf, yb = _rwkv_scan(r, kk, v, lw, kd, bb)
    rw_out = _rwkv_post(yf.reshape(n, -1), yb.reshape(n, -1), bv.reshape(n, -1), g.reshape(n, -1), wts, bm)
    merged = _merge(h0b, s5_out, rw_out, wts, bm)
    h1, h1p, logits = _out_proj(merged, h0, wts, _row_block(t_pad, 512))
    seq_pos = jnp.arange(n, dtype=jnp.int32) % t_pad
    valid = (seq_pos < t) | (seq_pos >= t_pad - N_META)
    row_tok, pos, w, blk_exp, n_used, n_blocks = _route(logits, wts, valid)
    eo = _moe_experts(h1p, row_tok, blk_exp, n_used, n_blocks, wts)
    return _moe_combine(eo, pos, w, h1, wts, bsz, t, t_pad)


def kernel(x_prompt, x_sample, meta, ln_in_g, ln_in_b, w_in, shift_mu, s5_B_re, s5_B_im, s5_A_re, s5_A_im, s5_log_dt, s5_C_re, s5_C_im, s5_D, s5_glu_w, s5_glu_b, rw_w0, rw_w2, rw_a0, rw_a2, rw_g2, rw_k_k, rw_k_a, rw_r_k, rw_lnx_g, rw_lnx_b, proj_s5, proj_rwkv, gate_b, w_out, ln1_g, ln1_b, router_coarse, router_coarse_b, router_fine, router_fine_b, exp_w_gate, exp_w_up, exp_w_down, ln2_g, ln2_b):
    p = {
        'meta': meta, 'ln_in_g': ln_in_g, 'ln_in_b': ln_in_b, 'w_in': w_in, 'shift_mu': shift_mu,
        's5_B_re': s5_B_re, 's5_B_im': s5_B_im, 's5_A_re': s5_A_re, 's5_A_im': s5_A_im,
        's5_log_dt': s5_log_dt, 's5_C_re': s5_C_re, 's5_C_im': s5_C_im, 's5_D': s5_D,
        's5_glu_w': s5_glu_w, 's5_glu_b': s5_glu_b,
        'rw_w0': rw_w0, 'rw_w2': rw_w2, 'rw_a0': rw_a0, 'rw_a2': rw_a2, 'rw_g2': rw_g2,
        'rw_k_k': rw_k_k, 'rw_k_a': rw_k_a, 'rw_r_k': rw_r_k, 'rw_lnx_g': rw_lnx_g, 'rw_lnx_b': rw_lnx_b,
        'proj_s5': proj_s5, 'proj_rwkv': proj_rwkv, 'gate_b': gate_b, 'w_out': w_out,
        'ln1_g': ln1_g, 'ln1_b': ln1_b,
        'router_coarse': router_coarse, 'router_coarse_b': router_coarse_b,
        'router_fine': router_fine, 'router_fine_b': router_fine_b,
        'exp_w_gate': exp_w_gate, 'exp_w_up': exp_w_up, 'exp_w_down': exp_w_down,
        'ln2_g': ln2_g, 'ln2_b': ln2_b,
    }
    wts = _prepare_weights(p)
    return (_encode(x_prompt, p, wts), _encode(x_sample, p, wts))
```

```python
import functools
import math

import jax
import jax.numpy as jnp
from jax import lax
from jax.experimental import pallas as pl
from jax.experimental.pallas import tpu as pltpu

F32 = jnp.float32
BF16 = jnp.bfloat16

D_MODEL = 2048
N_META = 16
S5_WIDTH = 1024
S5_GROUP = 16
S5_GROUPS = 64
S5_STATE = 64
S5_GB = 8
S5_PAIR_MAX_CHUNKS = 160
S5_CHUNK = 16
RW_WIDTH = 1024
RW_HEAD = 64
RW_HEADS = 16
RW_DECAY_LORA = 64
RW_ICLR_LORA = 64
RW_GATE_LORA = 160
RW_GATE_PAD = 256
RW_COLS = 3 * RW_WIDTH + 2 * RW_DECAY_LORA + 2 * RW_ICLR_LORA + RW_GATE_PAD
RW_CHUNK = 64
RW_HEADS_PER_STEP = 2
MOE_GROUPS = 4
EXPERTS_PER_GROUP = 8
N_EXPERTS = 32
D_EXPERT = 512
MOE_ROWS = 256
ROUTER_PAD = 128
DEPTH = 1
ALPHA = (2 * DEPTH) ** 0.25
LN_EPS = 1e-5
GN_EPS = 64e-5
SEQ_TAIL = 64
VMEM_LIMIT = 56 * 1024 * 1024


def _cparams(*sem):
    return pltpu.CompilerParams(dimension_semantics=sem, vmem_limit_bytes=VMEM_LIMIT)


def _row_block(t_pad, cap, mult=8):
    best = mult
    for d in range(mult, cap + 1, mult):
        if t_pad % d == 0:
            best = d
    return best


def _dot(a, b):
    return jnp.dot(a, b, preferred_element_type=F32)


def _dot_nt(a, b):
    return lax.dot_general(a, b, (((1,), (1,)), ((), ())), preferred_element_type=F32)


def _dot_tn(a, b):
    return lax.dot_general(a, b, (((0,), (0,)), ((), ())), preferred_element_type=F32)


def _split(x):
    hi = x.astype(BF16)
    lo = (x - hi.astype(F32)).astype(BF16)
    return hi, lo


def _layernorm(x, g, b):
    mu = jnp.mean(x, axis=-1, keepdims=True)
    xc = x - mu
    var = jnp.mean(xc * xc, axis=-1, keepdims=True)
    return xc * lax.rsqrt(var + LN_EPS) * g + b


def _ln_in_kernel(x_ref, m_ref, g_ref, b_ref, of_ref, ob_ref, *, n_token_blocks):
    j = pl.program_id(1)
    bx, d = x_ref.shape[1], x_ref.shape[2]

    @pl.when(j < n_token_blocks)
    def _():
        y = _layernorm(x_ref[0], g_ref[...], b_ref[...])
        of_ref[0] = y
        ob_ref[0] = y.astype(BF16)

    @pl.when(j == n_token_blocks)
    def _():
        ym = _layernorm(m_ref[...], g_ref[...], b_ref[...])
        y = jnp.concatenate([jnp.zeros((SEQ_TAIL - N_META, d), F32), ym, jnp.zeros((bx - SEQ_TAIL, d), F32)],
                            axis=0)
        of_ref[0] = y
        ob_ref[0] = y.astype(BF16)


def _ln_in(x, meta, g, b):
    bsz, t, d = x.shape
    assert t % SEQ_TAIL == 0
    t_pad = t + SEQ_TAIL
    bx = _row_block(t, 512, SEQ_TAIL)
    nxb = t // bx
    row = pl.BlockSpec((1, bx, d), lambda i, j: (i, j, 0))
    vec = pl.BlockSpec((1, d), lambda i, j: (0, 0))
    return pl.pallas_call(
        functools.partial(_ln_in_kernel, n_token_blocks=nxb),
        grid=(bsz, nxb + 1),
        in_specs=[pl.BlockSpec((1, bx, d), lambda i, j: (i, jnp.minimum(j, nxb - 1), 0)),
                  pl.BlockSpec((N_META, d), lambda i, j: (0, 0)), vec, vec],
        out_specs=[row, row],
        out_shape=[jax.ShapeDtypeStruct((bsz, t_pad, d), F32), jax.ShapeDtypeStruct((bsz, t_pad, d), BF16)],
        compiler_params=_cparams("parallel", "parallel"),
        name="ln_in",
    )(x, meta, g.reshape(1, d), b.reshape(1, d))


def _mm_kernel(x_ref, w_ref, o_ref):
    o_ref[...] = _dot(x_ref[...], w_ref[...]).astype(o_ref.dtype)


def _mm(x, w, bm, bn, out_dtype, name):
    n, k = x.shape
    m = w.shape[1]
    return pl.pallas_call(
        _mm_kernel,
        grid=(m // bn, n // bm),
        in_specs=[pl.BlockSpec((bm, k), lambda j, i: (i, 0)),
                  pl.BlockSpec((k, bn), lambda j, i: (0, j))],
        out_specs=pl.BlockSpec((bm, bn), lambda j, i: (i, j)),
        out_shape=jax.ShapeDtypeStruct((n, m), out_dtype),
        compiler_params=_cparams("parallel", "parallel"),
        name=name,
    )(x, w)


def _mm_slab_kernel(x_ref, w_ref, o_ref):
    res = _dot(x_ref[...], w_ref[...])
    for g in range(o_ref.shape[0]):
        o_ref[g] = res[:, g * 128:(g + 1) * 128]


def _mm_slabs(x, w, bm, name):
    n, k = x.shape
    m = w.shape[1]
    return pl.pallas_call(
        _mm_slab_kernel,
        grid=(n // bm,),
        in_specs=[pl.BlockSpec((bm, k), lambda i: (i, 0)), pl.BlockSpec((k, m), lambda i: (0, 0))],
        out_specs=pl.BlockSpec((m // 128, bm, 128), lambda i: (0, i, 0)),
        out_shape=jax.ShapeDtypeStruct((m // 128, n, 128), F32),
        compiler_params=_cparams("parallel"),
        name=name,
    )(x, w)


def _s5_matrices(b_re, b_im, a_re, a_im, log_dt, c_re, c_im):
    L = S5_CHUNK
    dt = jnp.exp(log_dt)[..., None]
    mag = jnp.exp(a_re * dt)
    abr = mag * jnp.cos(a_im * dt)
    abi = mag * jnp.sin(a_im * dt)
    den = a_re * a_re + a_im * a_im
    nr = abr - 1.0
    cr = (nr * a_re + abi * a_im) / den
    ci = (abi * a_re - nr * a_im) / den
    bbr = cr[..., None] * b_re - ci[..., None] * b_im
    bbi = cr[..., None] * b_im + ci[..., None] * b_re
    tau = jnp.arange(L + 1, dtype=F32)[:, None, None, None]
    pmag = jnp.exp(tau * a_re * dt)
    pr = pmag * jnp.cos(tau * a_im * dt)
    pi = pmag * jnp.sin(tau * a_im * dt)
    wr = pr[..., None] * bbr - pi[..., None] * bbi
    wi = pr[..., None] * bbi + pi[..., None] * bbr
    kern = (jnp.einsum('zgop,tzgpi->tzgoi', c_re, wr)
            - jnp.einsum('zgop,tzgpi->tzgoi', c_im, wi))
    s = jnp.arange(L)[:, None]
    t = jnp.arange(L)[None, :]
    lag = t - s
    kf = jnp.where((lag >= 0)[..., None, None, None], kern[jnp.clip(lag, 0, L), 0], 0.0)
    kb = jnp.where((lag <= 0)[..., None, None, None], kern[jnp.clip(-lag, 0, L), 1], 0.0)
    toep = (kf + kb).transpose(2, 0, 4, 1, 3).reshape(S5_GROUPS, L * S5_GROUP, L * S5_GROUP)
    wf_r, wf_i = wr[::-1][1:, 0], wi[::-1][1:, 0]
    wb_r, wb_i = wr[:L, 1], wi[:L, 1]
    bmat = jnp.concatenate([wf_r, wb_r, wf_i, wb_i], axis=2)
    bmat = bmat.transpose(1, 0, 3, 2).reshape(S5_GROUPS, L * S5_GROUP, 4 * S5_STATE)
    pf_r, pf_i = pr[1:, 0], pi[1:, 0]
    pb_r, pb_i = pr[::-1][:L, 1], pi[::-1][:L, 1]
    c0r, c0i, c1r, c1i = c_re[0], c_im[0], c_re[1], c_im[1]

    def cpow(cre, cim, p_r, p_i):
        re = cre[None] * p_r[:, :, None, :] - cim[None] * p_i[:, :, None, :]
        im = cre[None] * p_i[:, :, None, :] + cim[None] * p_r[:, :, None, :]
        return re, -im

    f_re, f_im = cpow(c0r, c0i, pf_r, pf_i)
    g_re, g_im = cpow(c1r, c1i, pb_r, pb_i)
    cmat = jnp.concatenate([f_re, g_re, f_im, g_im], axis=3)
    cmat = cmat.transpose(1, 3, 0, 2).reshape(S5_GROUPS, 4 * S5_STATE, L * S5_GROUP)
    lam_re = jnp.concatenate([pr[L, 0], pr[L, 1]], axis=-1)[:, None, :]
    lam_im = jnp.concatenate([pi[L, 0], pi[L, 1]], axis=-1)[:, None, :]
    return bmat, toep, cmat, lam_re, lam_im


def _s5_expand_kernel(src_ref, e_ref, o_ref, *, pieces):
    for j in range(src_ref.shape[0]):
        x = _dot(src_ref[j].astype(BF16), e_ref[j]).astype(BF16)
        for src, rows, dst, step in pieces:
            o_ref[0, dst + j * step:dst + j * step + rows, :] = x[src:src + rows, :]


def _s5_expand(per_group, expand, pieces):
    groups, rows, cols = per_group.shape
    gb, _, wide = expand.shape
    return pl.pallas_call(
        functools.partial(_s5_expand_kernel, pieces=pieces),
        grid=(groups // gb,),
        in_specs=[pl.BlockSpec((gb, rows, cols), lambda g: (g, 0, 0)),
                  pl.BlockSpec((gb, cols, wide), lambda g: (0, 0, 0))],
        out_specs=pl.BlockSpec((1, gb * rows, wide), lambda g: (g, 0, 0)),
        out_shape=jax.ShapeDtypeStruct((groups // gb, gb * rows, wide), BF16),
        compiler_params=_cparams("parallel"),
        name="s5_expand",
    )(per_group, expand)


def _s5_block_operators(bmat, toep, cmat, lam_re, lam_im):
    nb, gb, L, c, p = S5_GROUPS // S5_GB, S5_GB, S5_CHUNK, S5_GROUP, S5_STATE
    quarters = (0, 2, 1, 3)
    wide = L * gb * c
    r = jnp.arange(L * c)[None, :, None]
    col = jnp.arange(wide)[None, None, :]
    j = jnp.arange(gb)[:, None, None]
    e_tok = ((r // c == col // (gb * c)) & ((col // c) % gb == j) & (r % c == col % c)).astype(BF16)
    k_of_col = jnp.array(quarters)[col // (gb * p)]
    e_state = ((r // p == k_of_col) & ((col // p) % gb == j) & (r % p == col % p)).astype(BF16)
    tok_rows = tuple((s * c, c, s * gb * c, c) for s in range(L))
    state_rows = tuple((quarters[k] * p, p, k * gb * p, p) for k in range(4))
    wb = _s5_expand(bmat, e_state, tok_rows)
    wt = _s5_expand(toep, e_tok, tok_rows)
    wc = _s5_expand(cmat, e_tok, state_rows)
    ar_f = lam_re[:, 0, :p].reshape(nb, gb * p)
    ar_b = lam_re[:, 0, p:].reshape(nb, gb * p)
    ai_f = lam_im[:, 0, :p].reshape(nb, gb * p)
    ai_b = lam_im[:, 0, p:].reshape(nb, gb * p)
    a1 = jnp.stack([ar_f, ar_f, ar_b, ar_b] * 2, axis=1)
    a2 = jnp.stack([-ai_f, ai_f, -ai_b, ai_b] * 2, axis=1)
    return wb, wt, wc, a1, a2


def _s5_kernel(u_ref, wb_ref, wt_ref, wc_ref, a1_ref, a2_ref, y_ref, s_scr, xf_scr, xb_scr, *, n_chunks):
    L = S5_CHUNK
    C = n_chunks
    nseq = u_ref.shape[1]
    cp = -(-C // 8) * 8
    sw = S5_GB * S5_STATE
    tail = SEQ_TAIL // L

    def rows_of(parts):
        if cp > C:
            pad = jnp.zeros((cp - C, parts[0].shape[1]), F32)
            parts = [x for part in parts for x in (part, pad)]
        return jnp.concatenate(parts, axis=0).astype(BF16)

    u8 = rows_of([jnp.concatenate([u_ref[0, q, pl.ds(s, C, stride=L), :] for s in range(L)], axis=1)
                  for q in range(nseq)])
    s_all = _dot(u8, wb_ref[0])
    for q in range(nseq):
        for k in range(4):
            s_scr[:, 4 * q + k, :] = s_all[q * cp:q * cp + C, k * sw:(k + 1) * sw]
    if nseq == 1:
        s_scr[:, 4:8, :] = jnp.zeros((C, 4, sw), F32)
    a1 = a1_ref[0]
    a2 = a2_ref[0]
    row = lax.broadcasted_iota(jnp.int32, (8, sw), 0)
    even = (row % 2) == 0
    is_fwd = (row % 4) < 2

    def step(c, x):
        cf = (c + C - tail) % C
        cb = (2 * C - 1 - c - tail) % C
        xf_scr[cf] = x
        xb_scr[cb] = x
        s = jnp.where(is_fwd, s_scr[cf], s_scr[cb])
        swapped = jnp.where(even, pltpu.roll(x, 7, 0), pltpu.roll(x, 1, 0))
        return a1 * x + a2 * swapped + s

    lax.fori_loop(0, C, step, jnp.zeros((8, sw), F32))
    x_in = rows_of([jnp.concatenate([xf_scr[:, 4 * q, :], xf_scr[:, 4 * q + 1, :],
                                     xb_scr[:, 4 * q + 2, :], xb_scr[:, 4 * q + 3, :]], axis=1)
                    for q in range(nseq)])
    y8 = _dot(u8, wt_ref[0]) + _dot(x_in, wc_ref[0])
    for q in range(nseq):
        for t in range(L):
            y_ref[0, q, pl.ds(t, C, stride=L), :] = y8[q * cp:q * cp + C, t * 128:(t + 1) * 128]


def _s5_ssm(u3, ops):
    wb, wt, wc, a1, a2 = ops
    _, bsz, t_pad, _ = u3.shape
    n_chunks = t_pad // S5_CHUNK
    nb = S5_GROUPS // S5_GB
    lanes = S5_GB * S5_GROUP
    sw = S5_GB * S5_STATE
    nseq = 2 if (bsz % 2 == 0 and n_chunks <= S5_PAIR_MAX_CHUNKS) else 1
    blk = pl.BlockSpec((1, nseq, t_pad, lanes), lambda g, b: (g, b, 0, 0))
    mat = pl.BlockSpec((1,) + wb.shape[1:], lambda g, b: (g, 0, 0), pipeline_mode=pl.Buffered(1))
    vec = pl.BlockSpec((1, 8, sw), lambda g, b: (g, 0, 0))
    return pl.pallas_call(
        functools.partial(_s5_kernel, n_chunks=n_chunks),
        grid=(nb, bsz // nseq),
        in_specs=[blk, mat, mat, mat, vec, vec],
        out_specs=blk,
        out_shape=jax.ShapeDtypeStruct(u3.shape, F32),
        scratch_shapes=[pltpu.VMEM((n_chunks, 8, sw), F32)] * 3,
        compiler_params=_cparams("arbitrary", "arbitrary"),
        name="s5_ssm",
    )(u3, wb, wt, wc, a1, a2)


def _s5_post_kernel(y_ref, u_ref, d_ref, w_ref, b_ref, o_ref):
    slabs = range(y_ref.shape[0])
    y = (jnp.concatenate([y_ref[g] for g in slabs], axis=1)
         + jnp.concatenate([u_ref[g] for g in slabs], axis=1) * d_ref[...])
    act = y * (0.5 * (1.0 + jnp.tanh(math.sqrt(2.0 / math.pi) * (y + 0.044715 * (y * y * y)))))
    z = _dot(act.astype(BF16), w_ref[...]) + b_ref[...]
    o_ref[...] = (act * jax.nn.sigmoid(z)).astype(o_ref.dtype)


def _s5_post(y, u, d_skip, glu_w, glu_b, bm):
    slabs, n, lanes = y.shape
    row = pl.BlockSpec((bm, S5_WIDTH), lambda i: (i, 0))
    slab = pl.BlockSpec((slabs, bm, lanes), lambda i: (0, i, 0))
    vec = pl.BlockSpec((1, S5_WIDTH), lambda i: (0, 0))
    return pl.pallas_call(
        _s5_post_kernel,
        grid=(n // bm,),
        in_specs=[slab, slab, vec, pl.BlockSpec((S5_WIDTH, S5_WIDTH), lambda i: (0, 0)), vec],
        out_specs=row,
        out_shape=jax.ShapeDtypeStruct((n, S5_WIDTH), BF16),
        compiler_params=_cparams("parallel"),
        name="s5_post",
    )(y, u, d_skip.reshape(1, -1), glu_w, glu_b.reshape(1, -1))


def _head_sum(x, e_ref, et_ref):
    hi, lo = _split(x)
    s = _dot(hi, e_ref[...]) + _dot(lo, e_ref[...])
    shi, slo = _split(s)
    return _dot(shi, et_ref[...]) + _dot(slo, et_ref[...])


def _rwkv_prep_kernel(cur_ref, prev_ref, next_ref, mu_ref, w2_ref, a2_ref, g2_ref, w0_ref, a0_ref,
                      kk_ref, ka_ref, rk_ref, e_ref, et_ref,
                      r_o, kk_o, v_o, g_o, bv_o, lw_o, kd_o, bb_o, *, t, t_pad, bm):
    j = pl.program_id(1)
    p = cur_ref[0]
    row = lax.broadcasted_iota(jnp.int32, (bm, 1), 0)
    prev_row = prev_ref[0, 7:8, :]
    next_row = next_ref[0, 0:1, :]
    prev = jnp.where(row == 0, prev_row, pltpu.roll(p, 1, 0))
    nxt = jnp.where(row == bm - 1, next_row, pltpu.roll(p, bm - 1, 0))
    xs = p + mu_ref[0:1, :] * (prev - p) + mu_ref[1:2, :] * (nxt - p)
    w = RW_WIDTH
    r = xs[:, 0:w]
    k = xs[:, w:2 * w]
    v = xs[:, 2 * w:3 * w]
    lw = xs[:, 3 * w:3 * w + 128]
    la = xs[:, 3 * w + 128:3 * w + 256]
    lg = xs[:, 3 * w + 256:]
    w_log = _dot(jnp.tanh(lw).astype(BF16), w2_ref[...])
    a_lin = _dot(la.astype(BF16), a2_ref[...])
    g = _dot(jax.nn.sigmoid(lg).astype(BF16), g2_ref[...])
    kk = k * kk_ref[...]
    n2 = _head_sum(kk * kk, e_ref, et_ref)
    kk = kk / jnp.maximum(jnp.sqrt(n2), 1e-12)
    pos = j * bm + row
    valid = (pos < t) | (pos >= t_pad - N_META)
    v = jnp.where(valid, v, 0.0)
    kd_sum = jnp.zeros_like(k)
    for z in range(2):
        wl = w_log[:, z * w:(z + 1) * w] + w0_ref[z:z + 1, :]
        lw_o[z, 0] = -math.exp(-0.5) * jax.nn.sigmoid(wl)
        a = jax.nn.sigmoid(a_lin[:, z * w:(z + 1) * w] + a0_ref[z:z + 1, :])
        kd = k * (1.0 + (a - 1.0) * ka_ref[...])
        kd_o[z, 0] = kd.astype(BF16)
        bb_o[z, 0] = (kk * a).astype(BF16)
        kd_sum = kd_sum + kd
    bonus = _head_sum(r * kd_sum * rk_ref[...], e_ref, et_ref)
    r_o[0] = r.astype(BF16)
    kk_o[0] = kk.astype(BF16)
    v_o[0] = v.astype(BF16)
    g_o[0] = g.astype(BF16)
    bv_o[0] = (bonus * v).astype(BF16)


def _rwkv_prep(rw3, wts, t):
    bsz, t_pad, _ = rw3.shape
    bm = _row_block(t_pad, 320, 64)
    nb8 = bm // 8
    n8 = t_pad // 8
    w = RW_WIDTH
    cur = pl.BlockSpec((1, bm, RW_COLS), lambda b, j: (b, j, 0))
    prev = pl.BlockSpec((1, 8, RW_COLS), lambda b, j: (b, (j * nb8 + n8 - 1) % n8, 0))
    nxt = pl.BlockSpec((1, 8, RW_COLS), lambda b, j: (b, ((j + 1) * nb8) % n8, 0))

    def full(a):
        return pl.BlockSpec(a.shape, lambda b, j: (0,) * a.ndim)

    shared = pl.BlockSpec((1, bm, w), lambda b, j: (b, j, 0))
    per_dir = pl.BlockSpec((2, 1, bm, w), lambda b, j: (0, b, j, 0))
    consts = [wts['mu'], wts['w2'], wts['a2'], wts['g2'], wts['w0'], wts['a0'],
              wts['k_k'], wts['k_a'], wts['r_k'], wts['head_e'], wts['head_et']]
    sds = jax.ShapeDtypeStruct
    return pl.pallas_call(
        functools.partial(_rwkv_prep_kernel, t=t, t_pad=t_pad, bm=bm),
        grid=(bsz, t_pad // bm),
        in_specs=[cur, prev, nxt] + [full(a) for a in consts],
        out_specs=[shared] * 5 + [per_dir] * 3,
        out_shape=[sds((bsz, t_pad, w), BF16)] * 5
        + [sds((2, bsz, t_pad, w), F32), sds((2, bsz, t_pad, w), BF16), sds((2, bsz, t_pad, w), BF16)],
        compiler_params=_cparams("parallel", "parallel"),
        name="rwkv_prep",
    )(rw3, rw3, rw3, *consts)


def _rwkv_chunk(fwd, r_ref, kk_ref, v_ref, lw_ref, kd_ref, bb_ref, y_ref, st_ref, z, q):
    L = RW_CHUNK
    hd = RW_HEAD
    gw = RW_HEADS_PER_STEP * hd
    n_groups = RW_WIDTH // gw
    row = lax.broadcasted_iota(jnp.int32, (L, L), 0)
    col = lax.broadcasted_iota(jnp.int32, (L, L), 1)
    tri = jnp.where((col <= row) if fwd else (col >= row), 1.0, 0.0).astype(BF16)
    grow = lax.broadcasted_iota(jnp.int32, (L, gw), 0)
    gcol = lax.broadcasted_iota(jnp.int32, (L, gw), 1) % L
    incl = (gcol <= grow) if fwd else (gcol >= grow)
    strict = (gcol < grow) if fwd else (gcol > grow)
    bd_mask = jnp.where(lax.broadcasted_iota(jnp.int32, (gw, gw), 0) // hd
                        == lax.broadcasted_iota(jnp.int32, (gw, gw), 1) // hd, 1.0, 0.0).astype(BF16)

    def bd(x):
        return jnp.concatenate([x] * RW_HEADS_PER_STEP, axis=0) * bd_mask

    def stack(x):
        return jnp.concatenate([x[:, h * hd:(h + 1) * hd] for h in range(RW_HEADS_PER_STEP)], axis=0)

    lw = lw_ref[0, q]
    lw_hi, lw_lo = _split(lw)
    c = _dot(tri, lw_hi) + _dot(tri, lw_lo)
    e = c - lw
    c_tot = c[L - 1:L, :] if fwd else c[0:1, :]
    r = r_ref[q].astype(F32)
    kk = kk_ref[q].astype(F32)
    kd = kd_ref[0, q].astype(F32)
    bb = bb_ref[0, q].astype(F32)
    v = v_ref[q]
    q1 = (kk * jnp.exp(e)).astype(BF16)
    q2 = (r * jnp.exp(c)).astype(BF16)
    inv = jnp.exp(-c)
    k1 = (kd * inv).astype(BF16)
    k2 = (bb * inv).astype(BF16)
    rest = jnp.exp(c_tot - c)
    k1p = (kd * rest).astype(BF16)
    k2p = (bb * rest).astype(BF16)
    dec_tot = jnp.exp(c_tot)
    def group_chain(g):
        sl = slice(g * gw, (g + 1) * gw)
        s0 = st_ref[q, z, g]
        s0_hi, s0_lo = _split(s0)
        lhs = jnp.concatenate([q1[:, sl], q2[:, sl]], axis=0)
        rhs = jnp.concatenate([bd(k1[:, sl]), bd(k2[:, sl]), bd(s0_hi), bd(s0_lo)], axis=0)
        m1 = _dot_nt(lhs, rhs)
        yield
        a_kd = jnp.where(strict, m1[:L, 0:gw], 0.0)
        a_b = jnp.where(strict, m1[:L, gw:2 * gw], 0.0)
        q1s = m1[:L, 2 * gw:3 * gw] + m1[:L, 3 * gw:]
        b_kd = jnp.where(incl, m1[L:, 0:gw], 0.0)
        b_b = jnp.where(incl, m1[L:, gw:2 * gw], 0.0)
        q2s = m1[L:, 2 * gw:3 * gw] + m1[L:, 3 * gw:]
        vg = v[:, sl]
        v_bd = bd(vg)
        x = q1s + _dot(a_kd.astype(BF16), v_bd)
        yield
        m = -a_b
        levels = L.bit_length() - 1
        for lvl in range(levels):
            mb = m.astype(BF16)
            if lvl < levels - 1:
                rr = _dot(mb, jnp.concatenate([bd(x.astype(BF16)), bd(mb)], axis=1))
                x = x + rr[:, :gw]
                m = rr[:, gw:]
            else:
                x = x + _dot(mb, bd(x.astype(BF16)))
            yield
        ub = x.astype(BF16)
        y = q2s + _dot(jnp.concatenate([b_kd, -b_b], axis=1).astype(BF16),
                       jnp.concatenate([v_bd, bd(ub)], axis=0))
        y_ref[q, :, sl] = y
        yield
        st_ref[q, z, g] = s0 * dec_tot[:, sl] + _dot_tn(
            jnp.concatenate([stack(vg), stack(ub)], axis=0),
            jnp.concatenate([bd(k1p[:, sl]), -bd(k2p[:, sl])], axis=0))

    return [group_chain(g) for g in range(n_groups)]


def _rwkv_scan_kernel(rf, kkf, vf, lwf, kdf, bbf, rb, kkb, vb, lwb, kdb, bbb, yf_ref, yb_ref, st_ref):
    @pl.when(pl.program_id(1) == 0)
    def _():
        st_ref[...] = jnp.zeros_like(st_ref)

    chains = []
    for q in range(rf.shape[0]):
        chains += _rwkv_chunk(True, rf, kkf, vf, lwf, kdf, bbf, yf_ref, st_ref, 0, q)
        chains += _rwkv_chunk(False, rb, kkb, vb, lwb, kdb, bbb, yb_ref, st_ref, 1, q)
    while chains:
        alive = []
        for chain in chains:
            try:
                next(chain)
                alive.append(chain)
            except StopIteration:
                pass
        chains = alive


def _rwkv_scan(r, kk, v, lw, kd, bb):
    bsz, t_pad, w = r.shape
    L = RW_CHUNK
    nc = t_pad // L
    gw = RW_HEADS_PER_STEP * RW_HEAD
    nseq = 2 if bsz % 2 == 0 else 1
    grid = (bsz // nseq, nc)

    def block(fwd, j):
        logical = j if fwd else nc - 1 - j
        return (logical + nc - 1) % nc

    def shared(fwd):
        return pl.BlockSpec((nseq, L, w), lambda b, j: (b, block(fwd, j), 0))

    def per_dir(fwd):
        return pl.BlockSpec((1, nseq, L, w), lambda b, j: (0 if fwd else 1, b, block(fwd, j), 0))

    in_specs = []
    for fwd in (True, False):
        in_specs += [shared(fwd), shared(fwd), shared(fwd), per_dir(fwd), per_dir(fwd), per_dir(fwd)]
    return pl.pallas_call(
        _rwkv_scan_kernel,
        grid=grid,
        in_specs=in_specs,
        out_specs=[shared(True), shared(False)],
        out_shape=[jax.ShapeDtypeStruct((bsz, t_pad, w), F32)] * 2,
        scratch_shapes=[pltpu.VMEM((nseq, 2, w // gw, RW_HEAD, gw), F32)],
        compiler_params=_cparams("parallel", "arbitrary"),
        name="rwkv_scan",
    )(r, kk, v, lw, kd, bb, r, kk, v, lw, kd, bb)


def _rwkv_post_kernel(yf_ref, yb_ref, bv_ref, g_ref, lg_ref, lb_ref, e_ref, et_ref, o_ref):
    y = yf_ref[...] + yb_ref[...]
    mean = _head_sum(y, e_ref, et_ref) * (1.0 / RW_HEAD)
    yc = y - mean
    var = _head_sum(yc * yc, e_ref, et_ref) * (1.0 / RW_HEAD)
    y = yc * lax.rsqrt(var + GN_EPS) * lg_ref[...] + lb_ref[...]
    o_ref[...] = ((y + bv_ref[...].astype(F32)) * g_ref[...].astype(F32)).astype(o_ref.dtype)


def _rwkv_post(yf, yb, bv, g, wts, bm):
    n, w = yf.shape
    row = pl.BlockSpec((bm, w), lambda i: (i, 0))
    vec = pl.BlockSpec((1, w), lambda i: (0, 0))
    e, et = wts['head_e'], wts['head_et']
    return pl.pallas_call(
        _rwkv_post_kernel,
        grid=(n // bm,),
        in_specs=[row, row, row, row, vec, vec,
                  pl.BlockSpec(e.shape, lambda i: (0, 0)), pl.BlockSpec(et.shape, lambda i: (0, 0))],
        out_specs=row,
        out_shape=jax.ShapeDtypeStruct((n, w), BF16),
        compiler_params=_cparams("parallel"),
        name="rwkv_post",
    )(yf, yb, bv, g, wts['lnx_g'], wts['lnx_b'], e, et)


def _merge_kernel(h_ref, s5_ref, rw_ref, wg0_ref, wg1_ref, gb_ref, p0_ref, p1_ref, o_ref):
    h = h_ref[...]
    g0 = jax.nn.sigmoid(_dot(h, wg0_ref[...]) + gb_ref[0:1, :])
    g1 = jax.nn.sigmoid(_dot(h, wg1_ref[...]) + gb_ref[1:2, :])
    merged = g0 * _dot(s5_ref[...], p0_ref[...]) + g1 * _dot(rw_ref[...], p1_ref[...])
    o_ref[...] = merged.astype(o_ref.dtype)


def _merge(h0b, s5_out, rw_out, wts, bm):
    n, d = h0b.shape
    bn = 1024
    nj = d // bn
    return pl.pallas_call(
        _merge_kernel,
        grid=(nj, n // bm),
        in_specs=[pl.BlockSpec((bm, d), lambda j, i: (i, 0)),
                  pl.BlockSpec((bm, S5_WIDTH), lambda j, i: (i, 0)),
                  pl.BlockSpec((bm, RW_WIDTH), lambda j, i: (i, 0)),
                  pl.BlockSpec((d, bn), lambda j, i: (0, j)),
                  pl.BlockSpec((d, bn), lambda j, i: (0, nj + j)),
                  pl.BlockSpec((2, bn), lambda j, i: (0, j)),
                  pl.BlockSpec((S5_WIDTH, bn), lambda j, i: (0, j)),
                  pl.BlockSpec((RW_WIDTH, bn), lambda j, i: (0, j))],
        out_specs=pl.BlockSpec((bm, bn), lambda j, i: (i, j)),
        out_shape=jax.ShapeDtypeStruct((n, d), BF16),
        compiler_params=_cparams("parallel", "parallel"),
        name="merge",
    )(h0b, s5_out, rw_out, wts['w_gate'], wts['w_gate'], wts['gate_b'], wts['proj_s5'], wts['proj_rwkv'])


def _pack_bf16_pairs(x):
    half = x.shape[1] // 2
    bits = lax.bitcast_convert_type(x.astype(BF16).astype(F32), jnp.uint32)
    return bits[:, half:] | (bits[:, :half] >> 16)


def _unpack_bf16_pairs(p, dtype):
    lo = lax.bitcast_convert_type(p << 16, F32)
    hi = lax.bitcast_convert_type(p & jnp.uint32(0xFFFF0000), F32)
    return jnp.concatenate([lo, hi], axis=1).astype(dtype)


def _out_kernel(m_ref, h_ref, w_ref, g_ref, b_ref, rh_ref, rl_ref, o_ref, op_ref, lg_ref):
    bm = m_ref.shape[0]
    halves = [slice(0, bm // 2), slice(bm // 2, bm)]
    xs = [ALPHA * h_ref[rows, :] + _dot(m_ref[rows, :], w_ref[...]) for rows in halves]
    for rows, x in zip(halves, xs):
        h1 = _layernorm(x, g_ref[...], b_ref[...])
        o_ref[rows, :] = h1
        op_ref[rows, :] = _pack_bf16_pairs(h1)
        hi, lo = _split(h1)
        lg_ref[rows, :] = _dot(hi, rh_ref[...]) + _dot(lo, rh_ref[...]) + _dot(hi, rl_ref[...])


def _out_proj(merged, h0, wts, bm):
    n, d = h0.shape
    row = pl.BlockSpec((bm, d), lambda i: (i, 0))
    vec = pl.BlockSpec((1, d), lambda i: (0, 0))
    rt = pl.BlockSpec((d, ROUTER_PAD), lambda i: (0, 0))
    return pl.pallas_call(
        _out_kernel,
        grid=(n // bm,),
        in_specs=[row, row, pl.BlockSpec((d, d), lambda i: (0, 0)), vec, vec, rt, rt],
        out_specs=[row, pl.BlockSpec((bm, d // 2), lambda i: (i, 0)),
                   pl.BlockSpec((bm, ROUTER_PAD), lambda i: (i, 0))],
        out_shape=[jax.ShapeDtypeStruct((n, d), F32), jax.ShapeDtypeStruct((n, d // 2), jnp.uint32),
                   jax.ShapeDtypeStruct((n, ROUTER_PAD), F32)],
        compiler_params=_cparams("parallel"),
        name="out_proj",
    )(merged, h0, wts['w_out'], wts['ln1_g'], wts['ln1_b'], wts['router_hi'], wts['router_lo'])


def _route(logits, wts, valid):
    i32 = jnp.int32
    lc = logits[:, :MOE_GROUPS] + wts['router_coarse_b']
    grp = jnp.argmax(lc, axis=-1).astype(i32)
    gate_c = jnp.max(jax.nn.softmax(lc, axis=-1), axis=-1)
    lf = (logits[:, MOE_GROUPS:MOE_GROUPS + N_EXPERTS] + wts['router_fine_b'])
    lf = lf.reshape(-1, MOE_GROUPS, EXPERTS_PER_GROUP)
    sel = grp[:, None, None] == jnp.arange(MOE_GROUPS, dtype=i32)[None, :, None]
    lf = jnp.sum(jnp.where(sel, lf, 0.0), axis=1)
    lane = jnp.arange(EXPERTS_PER_GROUP, dtype=i32)[None, :]
    i1 = jnp.argmax(lf, axis=-1).astype(i32)
    v1 = jnp.max(lf, axis=-1)
    rest = jnp.where(lane == i1[:, None], -jnp.inf, lf)
    i2 = jnp.argmax(rest, axis=-1).astype(i32)
    v2 = jnp.max(rest, axis=-1)
    top_v = jnp.stack([v1, v2], axis=-1)
    top_i = jnp.stack([i1, i2], axis=-1)
    w = gate_c[:, None] * jax.nn.softmax(top_v, axis=-1)
    expert = grp[:, None] * EXPERTS_PER_GROUP + top_i
    expert = jnp.where(valid[:, None], expert, N_EXPERTS)
    w = jnp.where(valid[:, None], w, 0.0)
    n_tok = logits.shape[0]
    n_asg = 2 * n_tok
    e_flat = expert.reshape(-1)
    order = jnp.argsort(e_flat).astype(i32)
    inv = jnp.argsort(order).astype(i32)
    bounds = jnp.sum(e_flat[None, :] < jnp.arange(N_EXPERTS + 1, dtype=i32)[:, None], axis=1, dtype=i32)
    start = bounds[:N_EXPERTS]
    counts = bounds[1:] - start
    padded = (counts + MOE_ROWS - 1) // MOE_ROWS * MOE_ROWS
    ex = jnp.arange(N_EXPERTS, dtype=i32)
    pend = jnp.sum(jnp.where(ex[None, :] <= ex[:, None], padded[None, :], 0), axis=1)
    pstart = pend - padded
    n_blocks = -(-n_asg // MOE_ROWS) + N_EXPERTS
    n_rows = n_blocks * MOE_ROWS
    n_used = pend[-1] // MOE_ROWS
    blk = jnp.minimum(jnp.arange(n_blocks, dtype=i32), n_used - 1)
    blk_exp = jnp.sum(pend[None, :] <= (blk * MOE_ROWS)[:, None], axis=1, dtype=i32)
    blk_exp = jnp.minimum(blk_exp, N_EXPERTS - 1)
    experts = jnp.arange(N_EXPERTS, dtype=i32)

    def lookup(table, idx):
        return jnp.sum(jnp.where(idx[..., None] == experts, table, 0), axis=-1)

    pos = jnp.where(e_flat < N_EXPERTS, lookup(pstart - start, e_flat) + inv, 0)
    d = jnp.arange(n_rows, dtype=i32).reshape(n_blocks, MOE_ROWS)
    k = d - lookup(pstart, blk_exp)[:, None]
    src = jnp.clip(k + lookup(start, blk_exp)[:, None], 0, n_asg - 1)
    row_tok = jnp.where((k < lookup(counts, blk_exp)[:, None]) & (d < pend[-1]), order[src] // 2, 0)
    return row_tok, pos, w, blk_exp, n_used.reshape(1).astype(i32), n_blocks


def _row_gather_start(src_hbm, idx_ref, n_rows, dst, sem, stride=1, offset=0):
    for r in range(n_rows):
        row = idx_ref[0, 0, stride * r + offset]
        pltpu.make_async_copy(src_hbm.at[pl.ds(row, 1)], dst.at[pl.ds(r, 1)], sem).start(priority=r % 2)


def _row_gather_wait(src_hbm, n_rows, dst, sem):
    def wait(r, carry):
        pltpu.make_async_copy(src_hbm.at[pl.ds(0, 1)], dst.at[pl.ds(r, 1)], sem).wait()
        return carry

    lax.fori_loop(0, n_rows, wait, 0, unroll=8)


def _expert_kernel(nused_ref, bexp_ref, idx_ref, nidx_ref, x_hbm, wg_ref, wu_ref, wd_ref, o_ref, buf, sem):
    i = pl.program_id(0)
    n_used = nused_ref[0]
    slot = i % 2

    @pl.when((i == 0) & (n_used > 0))
    def _():
        _row_gather_start(x_hbm, idx_ref, MOE_ROWS, buf.at[0], sem.at[0])

    @pl.when(i + 1 < n_used)
    def _():
        _row_gather_start(x_hbm, nidx_ref, MOE_ROWS, buf.at[1 - slot], sem.at[1 - slot])

    @pl.when(i < n_used)
    def _():
        _row_gather_wait(x_hbm, MOE_ROWS, buf.at[slot], sem.at[slot])
        x = _unpack_bf16_pairs(buf[slot], BF16)
        hb = jax.nn.silu(_dot(x, wg_ref[0])) * _dot(x, wu_ref[0])
        o_ref[...] = _pack_bf16_pairs(_dot(hb.astype(BF16), wd_ref[0]))

    @pl.when(i >= n_used)
    def _():
        o_ref[...] = jnp.zeros_like(o_ref)


def _moe_experts(h1p, row_tok, blk_exp, n_used, n_blocks, wts):
    d = 2 * h1p.shape[1]
    idx = row_tok.reshape(n_blocks, 1, MOE_ROWS)
    last = n_blocks - 1
    smem = pltpu.SMEM
    return pl.pallas_call(
        _expert_kernel,
        grid_spec=pltpu.PrefetchScalarGridSpec(
            num_scalar_prefetch=2,
            grid=(n_blocks,),
            in_specs=[pl.BlockSpec((1, 1, MOE_ROWS), lambda i, nu, be: (i, 0, 0), memory_space=smem),
                      pl.BlockSpec((1, 1, MOE_ROWS), lambda i, nu, be: (jnp.minimum(i + 1, last), 0, 0),
                                   memory_space=smem),
                      pl.BlockSpec(memory_space=pl.ANY),
                      pl.BlockSpec((1, d, D_EXPERT), lambda i, nu, be: (be[i], 0, 0)),
                      pl.BlockSpec((1, d, D_EXPERT), lambda i, nu, be: (be[i], 0, 0)),
                      pl.BlockSpec((1, D_EXPERT, d), lambda i, nu, be: (be[i], 0, 0))],
            out_specs=pl.BlockSpec((MOE_ROWS, d // 2), lambda i, nu, be: (i, 0)),
            scratch_shapes=[pltpu.VMEM((2, MOE_ROWS, d // 2), jnp.uint32), pltpu.SemaphoreType.DMA((2,))],
        ),
        out_shape=jax.ShapeDtypeStruct((n_blocks * MOE_ROWS, d // 2), jnp.uint32),
        compiler_params=_cparams("arbitrary"),
        name="moe_experts",
    )(n_used, blk_exp, idx, idx, h1p, wts['exp_w_gate'], wts['exp_w_up'], wts['exp_w_down'])


def _combine_kernel(pos_ref, npos_ref, eo_hbm, h_ref, w_ref, g_ref, b_ref, o_ref, buf, sem, *, n_steps, bm):
    i = pl.program_id(0)
    slot = i % 2

    def start(p_ref, s):
        for k in range(2):
            _row_gather_start(eo_hbm, p_ref, bm, buf.at[s, k], sem.at[s], stride=2, offset=k)

    @pl.when(i == 0)
    def _():
        start(pos_ref, 0)

    @pl.when(i + 1 < n_steps)
    def _():
        start(npos_ref, 1 - slot)

    for k in range(2):
        _row_gather_wait(eo_hbm, bm, buf.at[slot, k], sem.at[slot])
    w = w_ref[0]
    moe = (w[:, 0:1] * _unpack_bf16_pairs(buf[slot, 0], F32)
           + w[:, 1:2] * _unpack_bf16_pairs(buf[slot, 1], F32))
    o_ref[0] = _layernorm(ALPHA * h_ref[0] + moe, g_ref[...], b_ref[...])


def _moe_combine(eo, pos, w, h1, wts, bsz, t, t_pad):
    n, d = h1.shape
    bm = _row_block(t, 256)
    per_seq = t // bm
    n_steps = bsz * per_seq
    last = n_steps - 1
    vec = pl.BlockSpec((1, d), lambda i: (0, 0))
    smem = pltpu.SMEM
    pos3 = pos.reshape(bsz, t_pad, 2)[:, :t].reshape(n_steps, 1, 2 * bm)

    def rows(i):
        return (i // per_seq, i % per_seq, 0)

    return pl.pallas_call(
        functools.partial(_combine_kernel, n_steps=n_steps, bm=bm),
        grid=(n_steps,),
        in_specs=[pl.BlockSpec((1, 1, 2 * bm), lambda i: (i, 0, 0), memory_space=smem),
                  pl.BlockSpec((1, 1, 2 * bm), lambda i: (jnp.minimum(i + 1, last), 0, 0), memory_space=smem),
                  pl.BlockSpec(memory_space=pl.ANY),
                  pl.BlockSpec((1, bm, d), rows),
                  pl.BlockSpec((1, bm, 2), rows), vec, vec],
        out_specs=pl.BlockSpec((1, bm, d), rows),
        out_shape=jax.ShapeDtypeStruct((bsz, t, d), F32),
        scratch_shapes=[pltpu.VMEM((2, 2, bm) + eo.shape[1:], eo.dtype), pltpu.SemaphoreType.DMA((2,))],
        compiler_params=_cparams("arbitrary"),
        name="moe_combine",
    )(pos3, pos3, eo, h1.reshape(bsz, t_pad, d), w.reshape(bsz, t_pad, 2), wts['ln2_g'], wts['ln2_b'])


def _prepare_weights(p):
    l = 0
    w_in = p['w_in'][l]
    c0 = S5_WIDTH
    c1 = c0 + 3 * RW_WIDTH + 2 * RW_DECAY_LORA + 2 * RW_ICLR_LORA + RW_GATE_LORA
    gpad = RW_GATE_PAD - RW_GATE_LORA
    wts = {}
    wts['w_u'] = w_in[:, :c0].astype(BF16)
    wts['w_rw'] = jnp.pad(w_in[:, c0:c1], ((0, 0), (0, gpad))).astype(BF16)
    wts['w_gate'] = w_in[:, c1:].astype(BF16)
    wts['mu'] = jnp.pad(p['shift_mu'][l], ((0, 0), (0, gpad)))
    z = jnp.zeros((RW_DECAY_LORA, RW_WIDTH), F32)
    wts['w2'] = jnp.block([[p['rw_w2'][l, 0], z], [z, p['rw_w2'][l, 1]]]).astype(BF16)
    wts['a2'] = jnp.block([[p['rw_a2'][l, 0], z], [z, p['rw_a2'][l, 1]]]).astype(BF16)
    wts['g2'] = jnp.pad(p['rw_g2'][l], ((0, gpad), (0, 0))).astype(BF16)
    wts['w0'] = p['rw_w0'][l]
    wts['a0'] = p['rw_a0'][l]
    wts['k_k'] = p['rw_k_k'][l].reshape(1, -1)
    wts['k_a'] = p['rw_k_a'][l].reshape(1, -1)
    wts['r_k'] = p['rw_r_k'][l].reshape(1, -1)
    wts['lnx_g'] = p['rw_lnx_g'][l].reshape(1, -1)
    wts['lnx_b'] = p['rw_lnx_b'][l].reshape(1, -1)
    head = jnp.arange(RW_WIDTH) // RW_HEAD
    e = (head[:, None] == jnp.arange(RW_HEADS)[None, :]).astype(BF16)
    wts['head_e'] = e
    wts['head_et'] = e.T
    wts['s5'] = _s5_block_operators(*_s5_matrices(
        p['s5_B_re'][l], p['s5_B_im'][l], p['s5_A_re'][l], p['s5_A_im'][l],
        p['s5_log_dt'][l], p['s5_C_re'][l], p['s5_C_im'][l]))
    wts['s5_D'] = p['s5_D'][l]
    wts['glu_w'] = p['s5_glu_w'][l].astype(BF16)
    wts['glu_b'] = p['s5_glu_b'][l]
    wts['proj_s5'] = p['proj_s5'][l].astype(BF16)
    wts['proj_rwkv'] = p['proj_rwkv'][l].astype(BF16)
    wts['gate_b'] = p['gate_b'][l]
    wts['w_out'] = p['w_out'][l].astype(BF16)
    wts['ln1_g'] = p['ln1_g'][l].reshape(1, -1)
    wts['ln1_b'] = p['ln1_b'][l].reshape(1, -1)
    router = jnp.concatenate([p['router_coarse'][l], p['router_fine'][l]], axis=1)
    router = jnp.pad(router, ((0, 0), (0, ROUTER_PAD - router.shape[1])))
    wts['router_hi'], wts['router_lo'] = _split(router)
    wts['router_coarse_b'] = p['router_coarse_b'][l]
    wts['router_fine_b'] = p['router_fine_b'][l]
    wts['exp_w_gate'] = p['exp_w_gate'][l].astype(BF16)
    wts['exp_w_up'] = p['exp_w_up'][l].astype(BF16)
    wts['exp_w_down'] = p['exp_w_down'][l].astype(BF16)
    wts['ln2_g'] = p['ln2_g'][l].reshape(1, -1)
    wts['ln2_b'] = p['ln2_b'][l].reshape(1, -1)
    return wts


def _encode(x, p, wts):
    bsz, t, d = x.shape
    t_pad = t + SEQ_TAIL
    n = bsz * t_pad
    h0, h0b = _ln_in(x, p['meta'], p['ln_in_g'], p['ln_in_b'])
    h0 = h0.reshape(n, d)
    h0b = h0b.reshape(n, d)
    bm = _row_block(t_pad, 1024)
    u = _mm_slabs(h0b, wts['w_u'], bm, "proj_s5_in")
    rw = _mm(h0b, wts['w_rw'], bm, RW_COLS // 4, F32, "proj_rwkv_in")
    slabs = u.shape[0]
    y_ssm = _s5_ssm(u.reshape(slabs, bsz, t_pad, 128), wts['s5']).reshape(slabs, n, 128)
    s5_out = _s5_post(y_ssm, u, wts['s5_D'], wts['glu_w'], wts['glu_b'], bm)
    r, kk, v, g, bv, lw, kd, bb = _rwkv_prep(rw.reshape(bsz, t_pad, RW_COLS), wts, t)
    yf, yb = _rwkv_scan(r, kk, v, lw, kd, bb)
    rw_out = _rwkv_post(yf.reshape(n, -1), yb.reshape(n, -1), bv.reshape(n, -1), g.reshape(n, -1), wts, bm)
    merged = _merge(h0b, s5_out, rw_out, wts, bm)
    h1, h1p, logits = _out_proj(merged, h0, wts, _row_block(t_pad, 512))
    seq_pos = jnp.arange(n, dtype=jnp.int32) % t_pad
    valid = (seq_pos < t) | (seq_pos >= t_pad - N_META)
    row_tok, pos, w, blk_exp, n_used, n_blocks = _route(logits, wts, valid)
    eo = _moe_experts(h1p, row_tok, blk_exp, n_used, n_blocks, wts)
    return _moe_combine(eo, pos, w, h1, wts, bsz, t, t_pad)


def kernel(x_prompt, x_sample, meta, ln_in_g, ln_in_b, w_in, shift_mu, s5_B_re, s5_B_im, s5_A_re, s5_A_im, s5_log_dt, s5_C_re, s5_C_im, s5_D, s5_glu_w, s5_glu_b, rw_w0, rw_w2, rw_a0, rw_a2, rw_g2, rw_k_k, rw_k_a, rw_r_k, rw_lnx_g, rw_lnx_b, proj_s5, proj_rwkv, gate_b, w_out, ln1_g, ln1_b, router_coarse, router_coarse_b, router_fine, router_fine_b, exp_w_gate, exp_w_up, exp_w_down, ln2_g, ln2_b):
    p = {
        'meta': meta, 'ln_in_g': ln_in_g, 'ln_in_b': ln_in_b, 'w_in': w_in, 'shift_mu': shift_mu,
        's5_B_re': s5_B_re, 's5_B_im': s5_B_im, 's5_A_re': s5_A_re, 's5_A_im': s5_A_im,
        's5_log_dt': s5_log_dt, 's5_C_re': s5_C_re, 's5_C_im': s5_C_im, 's5_D': s5_D,
        's5_glu_w': s5_glu_w, 's5_glu_b': s5_glu_b,
        'rw_w0': rw_w0, 'rw_w2': rw_w2, 'rw_a0': rw_a0, 'rw_a2': rw_a2, 'rw_g2': rw_g2,
        'rw_k_k': rw_k_k, 'rw_k_a': rw_k_a, 'rw_r_k': rw_r_k, 'rw_lnx_g': rw_lnx_g, 'rw_lnx_b': rw_lnx_b,
        'proj_s5': proj_s5, 'proj_rwkv': proj_rwkv, 'gate_b': gate_b, 'w_out': w_out,
        'ln1_g': ln1_g, 'ln1_b': ln1_b,
        'router_coarse': router_coarse, 'router_coarse_b': router_coarse_b,
        'router_fine': router_fine, 'router_fine_b': router_fine_b,
        'exp_w_gate': exp_w_gate, 'exp_w_up': exp_w_up, 'exp_w_down': exp_w_down,
        'ln2_g': ln2_g, 'ln2_b': ln2_b,
    }
    wts = _prepare_weights(p)
    return (_encode(x_prompt, p, wts), _encode(x_sample, p, wts))
```
